```python
import jax, jax.numpy as jnp
from jax import lax
import numpy as np

D_MODEL = 2048
BATCH = 1
SEQ = 8192
DEPTH = 1

CHUNK = 64
HEAD_DIM = 128
N_HEADS_SB = 8
N_HEADS_FOX = 8
D_SB = N_HEADS_SB * HEAD_DIM
D_FOX = N_HEADS_FOX * HEAD_DIM
N_IN = 3 * D_SB + 3 * D_FOX + N_HEADS_FOX + 2 * D_MODEL
Q_BLOCK = 128
N_EXPERTS = 32
TOP_K = 4
D_FF = D_MODEL
SWIGLU_LIMIT = 7.0
SWIGLU_ALPHA = 1.702
EXPERT_BLOCK = 128
RMS_EPS = 1e-5
FORGET_BIAS_INIT = 3.0

kernel_name = "hybrid_stickbreak_fox_moe_block"


def rmsnorm(x, g):
    xf = x.astype(jnp.float32)
    xf = xf * lax.rsqrt(jnp.mean(xf * xf, axis=-1, keepdims=True) + RMS_EPS)
    return (xf * g.astype(jnp.float32)).astype(x.dtype)


def to_heads(t, n_heads):
    b, s, _ = t.shape
    return t.reshape(b, s, n_heads, HEAD_DIM).transpose(0, 2, 1, 3)


def merge_block_heads(o):
    nb, b, h, qb, dh = o.shape
    return o.transpose(1, 0, 3, 2, 4).reshape(b, nb * qb, h * dh)


def stick_breaking_attention(q, k, v):
    _, _, s_len, dh = q.shape
    n_blocks = s_len // Q_BLOCK
    scale = dh ** -0.5
    kpos = jnp.arange(s_len)

    def block(i):
        q0 = i * Q_BLOCK
        qb = lax.dynamic_slice_in_dim(q, q0, Q_BLOCK, axis=2)
        z = jnp.einsum('bhqd,bhkd->bhqk', qb, k, preferred_element_type=jnp.float32) * scale
        qpos = q0 + jnp.arange(Q_BLOCK)
        strict = kpos[None, :] < qpos[:, None]
        log_keep = jnp.where(strict, jax.nn.log_sigmoid(-z), 0.0)
        log_between = lax.cumsum(log_keep, axis=3, reverse=True) - log_keep
        a = jnp.where(strict, jnp.exp(jax.nn.log_sigmoid(z) + log_between), 0.0)
        return jnp.einsum('bhqk,bhkd->bhqd', a.astype(v.dtype), v)

    return merge_block_heads(lax.map(block, jnp.arange(n_blocks)))


def forgetting_attention(q, k, v, log_f):
    _, _, s_len, dh = q.shape
    n_blocks = s_len // Q_BLOCK
    scale = dh ** -0.5
    kpos = jnp.arange(s_len)
    cum_f = jnp.cumsum(log_f, axis=-1)

    def block(i):
        q0 = i * Q_BLOCK
        qb = lax.dynamic_slice_in_dim(q, q0, Q_BLOCK, axis=2)
        fq = lax.dynamic_slice_in_dim(cum_f, q0, Q_BLOCK, axis=2)
        z = jnp.einsum('bhqd,bhkd->bhqk', qb, k, preferred_element_type=jnp.float32) * scale
        logits = z + fq[..., :, None] - cum_f[..., None, :]
        qpos = q0 + jnp.arange(Q_BLOCK)
        causal = kpos[None, :] <= qpos[:, None]
        p = jax.nn.softmax(jnp.where(causal, logits, -jnp.inf), axis=-1)
        return jnp.einsum('bhqk,bhkd->bhqd', p.astype(v.dtype), v)

    return merge_block_heads(lax.map(block, jnp.arange(n_blocks)))


def hybrid_mixer(h, w_in, b_forget, b_gate, w_proj_sb, w_proj_fox, w_out):
    p = h @ w_in
    cuts = np.cumsum([D_SB, D_SB, D_SB, D_FOX, D_FOX, D_FOX, N_HEADS_FOX]).tolist()
    q_sb, k_sb, v_sb, q_fx, k_fx, v_fx, f_fx, gate_logits = jnp.split(p, cuts, axis=-1)
    o_sb = stick_breaking_attention(to_heads(q_sb, N_HEADS_SB), to_heads(k_sb, N_HEADS_SB),
                                    to_heads(v_sb, N_HEADS_SB))
    log_f = jax.nn.log_sigmoid((f_fx + b_forget).astype(jnp.float32)).transpose(0, 2, 1)
    o_fx = forgetting_attention(to_heads(q_fx, N_HEADS_FOX), to_heads(k_fx, N_HEADS_FOX),
                                to_heads(v_fx, N_HEADS_FOX), log_f)
    g_sb, g_fx = jnp.split(jax.nn.sigmoid(gate_logits + b_gate), 2, axis=-1)
    mixed = g_sb * (o_sb @ w_proj_sb) + g_fx * (o_fx @ w_proj_fox)
    return mixed @ w_out


def moe_ffn(h, w_router, b_router, w_mlp1, b_mlp1, w_mlp2, b_mlp2):
    b, s, d = h.shape
    n_tok = b * s
    hf = h.reshape(n_tok, d)
    logits = hf.astype(jnp.float32) @ w_router.astype(jnp.float32) + b_router.astype(jnp.float32)
    top_v, top_i = lax.top_k(logits, TOP_K)
    top_w = jax.nn.softmax(top_v, axis=-1)
    n_assign = n_tok * TOP_K
    flat_e = top_i.reshape(-1)
    flat_tok = jnp.repeat(jnp.arange(n_tok), TOP_K)
    flat_w = top_w.reshape(-1)
    order = jnp.argsort(flat_e)
    se, stok, sw = flat_e[order], flat_tok[order], flat_w[order]
    counts = jnp.bincount(flat_e, length=N_EXPERTS)
    padded = (counts + EXPERT_BLOCK - 1) // EXPERT_BLOCK * EXPERT_BLOCK
    pad_end = jnp.cumsum(padded)
    pad_start = pad_end - padded
    raw_start = jnp.cumsum(counts) - counts
    dest = pad_start[se] + (jnp.arange(n_assign) - raw_start[se])
    n_slots = -(-n_assign // EXPERT_BLOCK) * EXPERT_BLOCK + N_EXPERTS * EXPERT_BLOCK
    n_blocks = n_slots // EXPERT_BLOCK
    slot_tok = jnp.full((n_slots,), n_tok, dtype=jnp.int32).at[dest].set(stok.astype(jnp.int32))
    slot_w = jnp.zeros((n_slots,), jnp.float32).at[dest].set(sw)
    block_e = jnp.minimum(jnp.searchsorted(pad_end, jnp.arange(n_blocks) * EXPERT_BLOCK, side='right'),
                          N_EXPERTS - 1)
    h_pad = jnp.concatenate([hf, jnp.zeros((1, d), hf.dtype)], axis=0)

    def expert_block(args):
        tok, e = args
        xb = h_pad[tok]
        gu = xb @ w_mlp1[e] + b_mlp1[e]
        gate = jnp.minimum(gu[:, 0::2], SWIGLU_LIMIT)
        lin = jnp.clip(gu[:, 1::2], -SWIGLU_LIMIT, SWIGLU_LIMIT)
        act = gate * jax.nn.sigmoid(SWIGLU_ALPHA * gate) * (lin + 1.0)
        return act @ w_mlp2[e] + b_mlp2[e]

    ys = lax.map(expert_block, (slot_tok.reshape(n_blocks, EXPERT_BLOCK), block_e))
    ys = ys.reshape(n_slots, d) * slot_w[:, None].astype(ys.dtype)
    out = jnp.zeros((n_tok + 1, d), ys.dtype).at[slot_tok].add(ys)[:n_tok]
    return out.reshape(b, s, d).astype(h.dtype)


def setup_inputs(seed: int = 0) -> dict:
    key = jax.random.key(seed)
    ks = jax.random.split(key, 16)
    f32 = jnp.float32
    nrm = lambda k, shape, scale: jax.random.normal(k, shape, f32) * scale
    return {
        "x": nrm(ks[0], (BATCH, SEQ, D_MODEL), 1.0),
        "g_mix": 1.0 + nrm(ks[1], (DEPTH, D_MODEL), 0.02),
        "w_in": nrm(ks[2], (DEPTH, D_MODEL, N_IN), D_MODEL ** -0.5),
        "b_forget": FORGET_BIAS_INIT + nrm(ks[3], (DEPTH, N_HEADS_FOX), 0.5),
        "b_gate": nrm(ks[4], (DEPTH, 2 * D_MODEL), 0.02),
        "w_proj_sb": nrm(ks[5], (DEPTH, D_SB, D_MODEL), D_SB ** -0.5),
        "w_proj_fox": nrm(ks[6], (DEPTH, D_FOX, D_MODEL), D_FOX ** -0.5),
        "w_out": nrm(ks[7], (DEPTH, D_MODEL, D_MODEL), D_MODEL ** -0.5),
        "g_ffn": 1.0 + nrm(ks[8], (DEPTH, D_MODEL), 0.02),
        "w_router": nrm(ks[9], (DEPTH, D_MODEL, N_EXPERTS), D_MODEL ** -0.5),
        "b_router": nrm(ks[10], (DEPTH, N_EXPERTS), 0.01),
        "w_mlp1": nrm(ks[11], (DEPTH, N_EXPERTS, D_MODEL, 2 * D_FF), D_MODEL ** -0.5),
        "b_mlp1": nrm(ks[12], (DEPTH, N_EXPERTS, 2 * D_FF), 0.02),
        "w_mlp2": nrm(ks[13], (DEPTH, N_EXPERTS, D_FF, D_MODEL), D_FF ** -0.5),
        "b_mlp2": nrm(ks[14], (DEPTH, N_EXPERTS, D_MODEL), 0.02),
        "g_final": 1.0 + nrm(ks[15], (D_MODEL,), 0.02),
    }


def reference(x, g_mix, w_in, b_forget, b_gate, w_proj_sb, w_proj_fox, w_out, g_ffn,
              w_router, b_router, w_mlp1, b_mlp1, w_mlp2, b_mlp2, g_final):
    for layer in range(DEPTH):
        h = rmsnorm(x, g_mix[layer])
        x = x + hybrid_mixer(h, w_in[layer], b_forget[layer], b_gate[layer],
                             w_proj_sb[layer], w_proj_fox[layer], w_out[layer])
        h = rmsnorm(x, g_ffn[layer])
        x = x + moe_ffn(h, w_router[layer], b_router[layer], w_mlp1[layer], b_mlp1[layer],
                        w_mlp2[layer], b_mlp2[layer])
    return rmsnorm(x, g_final)
```

```python
import functools

import jax
import jax.numpy as jnp
from jax import lax
from jax.experimental import pallas as pl
from jax.experimental.pallas import tpu as pltpu

F32 = jnp.float32
BF16 = jnp.bfloat16
U32 = jnp.uint32
I32 = jnp.int32

LANES = 128
SUBLANES = 8
HEAD_DIM = 128
N_HEADS = 8
D_HEADS = N_HEADS * HEAD_DIM
N_EXPERTS = 32
TOP_K = 4
SWIGLU_LIMIT = 7.0
SWIGLU_ALPHA = 1.702
RMS_EPS = 1e-5
NEG_BIG = -1e30
VMEM_LIMIT = 56 * 1024 * 1024


def _pick(n, candidates):
    for c in candidates:
        if n % c == 0:
            return c
    raise ValueError(f"no tile in {candidates} divides {n}")


def _params(sem, vmem=VMEM_LIMIT):
    return pltpu.CompilerParams(dimension_semantics=sem, vmem_limit_bytes=vmem)


def _rms(x, g):
    ms = jnp.mean(x * x, axis=-1, keepdims=True)
    return x * lax.rsqrt(ms + RMS_EPS) * g


def _inproj_kernel(x_ref, g_ref, w_ref, wf_ref, p_ref, f_ref, h_scr):
    @pl.when(pl.program_id(1) == 0)
    def _():
        hb = _rms(x_ref[...], g_ref[...]).astype(BF16)
        h_scr[...] = hb
        f_ref[...] = jnp.dot(hb, wf_ref[...], preferred_element_type=F32)

    p_ref[...] = jnp.dot(h_scr[...], w_ref[...], preferred_element_type=F32).astype(BF16)


def _in_proj(x, g, w_main, w_f):
    s, d = x.shape
    n = w_main.shape[1]
    tm = _pick(s, (1024, 512, 256, 128))
    tn = _pick(n, (1024, 512, 256, 128))
    return pl.pallas_call(
        _inproj_kernel,
        grid=(s // tm, n // tn),
        in_specs=[
            pl.BlockSpec((tm, d), lambda m, j: (m, 0)),
            pl.BlockSpec((1, d), lambda m, j: (0, 0)),
            pl.BlockSpec((d, tn), lambda m, j: (0, j)),
            pl.BlockSpec((d, LANES), lambda m, j: (0, 0)),
        ],
        out_specs=[
            pl.BlockSpec((tm, tn), lambda m, j: (m, j)),
            pl.BlockSpec((tm, LANES), lambda m, j: (m, 0)),
        ],
        out_shape=[
            jax.ShapeDtypeStruct((s, n), BF16),
            jax.ShapeDtypeStruct((s, LANES), F32),
        ],
        scratch_shapes=[pltpu.VMEM((tm, d), BF16)],
        compiler_params=_params(("parallel", "arbitrary")),
        name="in_proj",
    )(x, g, w_main, w_f)


_NT = (((1,), (1,)), ((), ()))


def _sb_kernel(q_ref, k_ref, vt_ref, o_ref, *, tq, tk, scale):
    i = pl.program_id(1)
    q = q_ref[...]
    row = lax.broadcasted_iota(I32, (tk, tk), 0)
    col = lax.broadcasted_iota(I32, (tk, tk), 1)
    ut = (col >= row).astype(BF16)
    dpos = (lax.broadcasted_iota(I32, (tk, tq), 1) - lax.broadcasted_iota(I32, (tk, tq), 0)
            + i * tq)
    nkv = ((i + 1) * tq) // tk

    def body(jj, carry):
        acc, csum = carry
        j = nkv - 1 - jj
        k = k_ref[pl.ds(pl.multiple_of(j * tk, tk), tk), :]
        z = lax.dot_general(k, q, _NT, preferred_element_type=F32) * scale
        strict = dpos > j * tk
        softplus = jnp.maximum(z, 0.0) + jnp.log1p(jnp.exp(-jnp.abs(z)))
        lk = jnp.where(strict, -softplus, 0.0)
        hi = lk.astype(BF16)
        lo = (lk - hi.astype(F32)).astype(BF16)
        cs = (jnp.dot(ut, hi, preferred_element_type=F32)
              + jnp.dot(ut, lo, preferred_element_type=F32))
        a = jnp.where(strict, jnp.exp(z + cs + csum), 0.0).astype(BF16)
        acc = acc + jnp.dot(vt_ref[j], a, preferred_element_type=F32)
        return acc, csum + cs[0:1, :]

    acc, _ = lax.fori_loop(
        0, nkv, body, (jnp.zeros((HEAD_DIM, tq), F32), jnp.zeros((1, tq), F32)))
    o_ref[...] = acc.T.astype(o_ref.dtype)


def _fox_kernel(q_ref, k_ref, vt_ref, fq_ref, fk_ref, o_ref, *, tq, tk, scale):
    i = pl.program_id(1)
    q = q_ref[...]
    fq = fq_ref[...]
    dpos = (lax.broadcasted_iota(I32, (tk, tq), 1) - lax.broadcasted_iota(I32, (tk, tq), 0)
            + i * tq)
    nkv = ((i + 1) * tq) // tk

    def body(j, carry):
        acc, m, l = carry
        ks = pl.multiple_of(j * tk, tk)
        k = k_ref[pl.ds(ks, tk), :]
        z = lax.dot_general(k, q, _NT, preferred_element_type=F32) * scale
        fk = fk_ref[pl.ds(ks, tk), :]
        fk = jnp.concatenate([fk] * (tq // LANES), axis=1)
        logits = jnp.where(dpos >= j * tk, z + (fq - fk), NEG_BIG)
        m_new = jnp.maximum(m, jnp.max(logits, axis=0, keepdims=True))
        alpha = jnp.exp(m - m_new)
        p = jnp.exp(logits - m_new)
        l = alpha * l + jnp.sum(p, axis=0, keepdims=True)
        acc = alpha * acc + jnp.dot(vt_ref[j], p.astype(BF16), preferred_element_type=F32)
        return acc, m_new, l

    acc, _, l = lax.fori_loop(
        0, nkv, body,
        (jnp.zeros((HEAD_DIM, tq), F32), jnp.full((1, tq), NEG_BIG, F32), jnp.zeros((1, tq), F32)))
    o_ref[...] = (acc / l).T.astype(o_ref.dtype)


def _attention(p, vt, q_col, k_col, cum_f=None):
    s = p.shape[0]
    tq = _pick(s, (256, 128))
    tk = 128
    scale = HEAD_DIM ** -0.5
    in_specs = [
        pl.BlockSpec((tq, HEAD_DIM), lambda h, i: (i, q_col + h)),
        pl.BlockSpec((s, HEAD_DIM), lambda h, i: (0, k_col + h)),
        pl.BlockSpec((None, s // tk, HEAD_DIM, tk), lambda h, i: (h, 0, 0, 0)),
    ]
    args = [p, p, vt]
    if cum_f is None:
        kern = functools.partial(_sb_kernel, tq=tq, tk=tk, scale=scale)
        name = "sb_attn"
    else:
        kern = functools.partial(_fox_kernel, tq=tq, tk=tk, scale=scale)
        name = "fox_attn"
        in_specs += [
            pl.BlockSpec((None, 1, tq), lambda h, i: (h, 0, i)),
            pl.BlockSpec((None, s, LANES), lambda h, i: (h, 0, 0)),
        ]
        args += list(cum_f)
    return pl.pallas_call(
        kern,
        grid=(N_HEADS, s // tq),
        in_specs=in_specs,
        out_specs=pl.BlockSpec((tq, HEAD_DIM), lambda h, i: (i, h)),
        out_shape=jax.ShapeDtypeStruct((s, D_HEADS), BF16),
        compiler_params=_params(("parallel", "arbitrary")),
        name=name,
    )(*args)


def _heads_t(v, tk):
    s = v.shape[0]
    return v.reshape(s // tk, tk, N_HEADS, HEAD_DIM).transpose(2, 0, 3, 1)


def _bf16_pair_pack(h):
    bits = lax.bitcast_convert_type(h, U32)
    rnd = bits + jnp.uint32(0x7FFF) + ((bits >> 16) & jnp.uint32(1))
    half = h.shape[1] // 2
    return (rnd[:, half:] & jnp.uint32(0xFFFF0000)) | (rnd[:, :half] >> 16)


def _bf16_pair_unpack(w):
    lo = lax.bitcast_convert_type(w << 16, F32)
    hi = lax.bitcast_convert_type(w & jnp.uint32(0xFFFF0000), F32)
    return jnp.concatenate([lo, hi], axis=1).astype(BF16)


def _mix_kernel(osb_ref, ofx_ref, ga_ref, gb_ref, bg_ref, wpa_ref, wpb_ref, wo_ref, x_ref, g_ref,
                wrh_ref, wrl_ref, br_ref,
                x2_ref, hp_ref, ri_ref, rw_ref, cnt_ref, cnt_scr, *, tm, d, cap):
    t = pl.program_id(0)

    @pl.when(t == 0)
    def _():
        cnt_scr[...] = jnp.zeros_like(cnt_scr)

    bg = bg_ref[...]
    pa = jnp.dot(osb_ref[...], wpa_ref[...], preferred_element_type=F32)
    pb = jnp.dot(ofx_ref[...], wpb_ref[...], preferred_element_type=F32)
    ga = jax.nn.sigmoid(ga_ref[...].astype(F32) + bg[:, :d])
    gb = jax.nn.sigmoid(gb_ref[...].astype(F32) + bg[:, d:])
    mixed = (ga * pa + gb * pb).astype(BF16)
    x2 = x_ref[...] + jnp.dot(mixed, wo_ref[...], preferred_element_type=F32)
    x2_ref[...] = x2
    h2 = _rms(x2, g_ref[...])
    hp_ref[...] = _bf16_pair_pack(h2)

    hh = h2.astype(BF16)
    hl = (h2 - hh.astype(F32)).astype(BF16)
    logits = (jnp.dot(hh, wrh_ref[...], preferred_element_type=F32)
              + jnp.dot(hh, wrl_ref[...], preferred_element_type=F32)
              + jnp.dot(hl, wrh_ref[...], preferred_element_type=F32)
              + br_ref[...])

    lane = lax.broadcasted_iota(I32, (tm, LANES), 1).astype(F32)
    work = logits
    sel = jnp.zeros((tm, LANES), F32)
    vals, idxs = [], []
    for _ in range(TOP_K):
        mx = jnp.max(work, axis=1, keepdims=True)
        idx = jnp.min(jnp.where(work == mx, lane, float(LANES)), axis=1, keepdims=True)
        hit = lane == idx
        vals.append(mx)
        idxs.append(idx)
        work = jnp.where(hit, -jnp.inf, work)
        sel = jnp.where(hit, 1.0, sel)
    exps = [jnp.exp(v - vals[0]) for v in vals]
    denom = exps[0] + exps[1] + exps[2] + exps[3]

    row = lax.broadcasted_iota(I32, (tm, tm), 0)
    col = lax.broadcasted_iota(I32, (tm, tm), 1)
    tri = (col < row).astype(BF16)
    rank = jnp.dot(tri, sel.astype(BF16), preferred_element_type=F32) + cnt_scr[...]
    cnt_new = cnt_scr[...] + jnp.sum(sel, axis=0, keepdims=True)
    cnt_scr[...] = cnt_new
    cnt_ref[...] = cnt_new
    slot = rank + lane * float(cap)
    ri = jnp.zeros((tm, LANES), F32)
    rw = jnp.zeros((tm, LANES), F32)
    for r in range(TOP_K):
        slot_r = jnp.sum(jnp.where(lane == idxs[r], slot, 0.0), axis=1, keepdims=True)
        ri = jnp.where(lane == float(r), slot_r, ri)
        rw = jnp.where(lane == float(r), exps[r] / denom, rw)
    ri_ref[...] = ri.astype(I32)
    rw_ref[...] = rw


def _mix_route(o_sb, o_fx, p, gate_col, b_gate, w_pa, w_pb, w_out, x, g_ffn, wr_hi, wr_lo, br, cap):
    s, d = x.shape
    tm = _pick(s, (256, 128))
    const = lambda shape: pl.BlockSpec(shape, lambda t: (0, 0), pipeline_mode=pl.Buffered(1))
    kern = functools.partial(_mix_kernel, tm=tm, d=d, cap=cap)
    return pl.pallas_call(
        kern,
        grid=(s // tm,),
        in_specs=[
            pl.BlockSpec((tm, D_HEADS), lambda t: (t, 0)),
            pl.BlockSpec((tm, D_HEADS), lambda t: (t, 0)),
            pl.BlockSpec((tm, d), lambda t: (t, gate_col)),
            pl.BlockSpec((tm, d), lambda t: (t, gate_col + 1)),
            const((1, 2 * d)),
            const((D_HEADS, d)),
            const((D_HEADS, d)),
            const((d, d)),
            pl.BlockSpec((tm, d), lambda t: (t, 0)),
            const((1, d)),
            const((d, LANES)),
            const((d, LANES)),
            const((1, LANES)),
        ],
        out_specs=[
            pl.BlockSpec((tm, d), lambda t: (t, 0)),
            pl.BlockSpec((tm, d // 2), lambda t: (t, 0)),
            pl.BlockSpec((tm, LANES), lambda t: (t, 0)),
            pl.BlockSpec((tm, LANES), lambda t: (t, 0)),
            pl.BlockSpec((1, LANES), lambda t: (0, 0)),
        ],
        out_shape=[
            jax.ShapeDtypeStruct((s, d), F32),
            jax.ShapeDtypeStruct((s, d // 2), U32),
            jax.ShapeDtypeStruct((s, LANES), I32),
            jax.ShapeDtypeStruct((s, LANES), F32),
            jax.ShapeDtypeStruct((1, LANES), F32),
        ],
        scratch_shapes=[pltpu.VMEM((1, LANES), F32)],
        compiler_params=_params(("arbitrary",)),
        name="mix_route",
    )(o_sb, o_fx, p, p, b_gate, w_pa, w_pb, w_out, x, g_ffn, wr_hi, wr_lo, br)


def _dispatch_kernel(slot_ref, cnt_ref, hp_ref, xs_ref, zero_scr, sem, zsem, *, tm, ch, cap):
    t = pl.program_id(0)
    base = t * tm

    def row_copy(r, k):
        return pltpu.make_async_copy(
            hp_ref.at[pl.ds(r, 1), :],
            xs_ref.at[pl.ds(slot_ref[(base + r) * TOP_K + k], 1), :],
            sem)

    def start(r, c):
        for k in range(TOP_K):
            row_copy(r, k).start()
        return c

    def wait(r, c):
        for k in range(TOP_K):
            row_copy(r, k).wait()
        return c

    @pl.when(t == 0)
    def _():
        zero_scr[...] = jnp.zeros_like(zero_scr)

        def pad_copy(e):
            first = pl.multiple_of(e * cap + (cnt_ref[e] // SUBLANES) * SUBLANES, SUBLANES)
            return pltpu.make_async_copy(zero_scr, xs_ref.at[pl.ds(first, ch), :], zsem)

        def zstart(e, c):
            pad_copy(e).start()
            return c

        def zwait(e, c):
            pad_copy(e).wait()
            return c

        lax.fori_loop(0, N_EXPERTS, zstart, 0)
        lax.fori_loop(0, N_EXPERTS, zwait, 0)

    lax.fori_loop(0, tm, start, 0)
    lax.fori_loop(0, tm, wait, 0)


def _dispatch(slots, counts, hp, cap, ch):
    s, half = hp.shape
    tm = _pick(s, (512, 256, 128))
    kern = functools.partial(_dispatch_kernel, tm=tm, ch=ch, cap=cap)
    return pl.pallas_call(
        kern,
        grid_spec=pltpu.PrefetchScalarGridSpec(
            num_scalar_prefetch=2,
            grid=(s // tm,),
            in_specs=[pl.BlockSpec((tm, half), lambda t, sl, cn: (t, 0))],
            out_specs=pl.BlockSpec(memory_space=pl.ANY),
            scratch_shapes=[
                pltpu.VMEM((ch, half), U32),
                pltpu.SemaphoreType.DMA(()),
                pltpu.SemaphoreType.DMA(()),
            ],
        ),
        out_shape=jax.ShapeDtypeStruct((N_EXPERTS * cap, half), U32),
        compiler_params=_params(("arbitrary",)),
        name="dispatch",
    )(slots, counts, hp)


def _expert_kernel(sbe_ref, sbblk_ref, sbnch_ref, x_ref, w1_ref, b1_ref, w2_ref, b2_ref, y_ref,
                   xb_scr, w1b_scr, w2b_scr, *, ch, ff):
    s = pl.program_id(0)
    j = pl.program_id(1)
    nch = sbnch_ref[s]
    half = LANES // 2
    even = (lax.broadcasted_iota(I32, (ch, LANES), 1) % 2) == 0

    def half_act(g):
        lin = pltpu.roll(g, LANES - 1, 1)
        gate = jnp.minimum(g, SWIGLU_LIMIT)
        lin = jnp.clip(lin, -SWIGLU_LIMIT, SWIGLU_LIMIT)
        return gate * jax.nn.sigmoid(SWIGLU_ALPHA * gate) * (lin + 1.0)

    @pl.when(nch > 0)
    def _():
        @pl.when(j == 0)
        def _():
            xb_scr[...] = _bf16_pair_unpack(x_ref[...])

        w1b_scr[...] = w1_ref[...].astype(BF16)
        prow = lax.broadcasted_iota(I32, (LANES, LANES), 0)
        pcol = lax.broadcasted_iota(I32, (LANES, LANES), 1)
        perm = (pcol == prow // 2 + half * (prow % 2)).astype(BF16)
        for u in range(ff // LANES):
            grp = w2_ref[pl.ds(u * LANES, LANES), :].astype(BF16)
            w2b_scr[pl.ds(u * LANES, LANES), :] = jnp.dot(
                perm, grp, preferred_element_type=F32).astype(BF16)

        def chunk(c, carry):
            r0 = pl.multiple_of(c * ch, ch)
            xk = xb_scr[pl.ds(r0, ch), :]
            gu = jnp.dot(xk, w1b_scr[...], preferred_element_type=F32) + b1_ref[...]
            pieces = []
            for u in range(ff // LANES):
                a = half_act(gu[:, (2 * u) * LANES:(2 * u + 1) * LANES])
                b = half_act(gu[:, (2 * u + 1) * LANES:(2 * u + 2) * LANES])
                pieces.append(jnp.where(even, a, pltpu.roll(b, 1, 1)))
            act = jnp.concatenate(pieces, axis=1).astype(BF16)
            yk = jnp.dot(act, w2b_scr[...], preferred_element_type=F32)

            @pl.when(j == 0)
            def _():
                y_ref[pl.ds(r0, ch), :] = yk + b2_ref[...]

            @pl.when(j > 0)
            def _():
                y_ref[pl.ds(r0, ch), :] += yk

            return carry

        lax.fori_loop(0, nch, chunk, 0)


def _experts(sb_e, sb_blk, sb_nch, xs, w1, b1, w2, b2, r_rows, ch, cap):
    ne, d, ff2 = w1.shape
    dff = ff2 // 2
    ff = _pick(dff, (256, 128))
    nj = dff // ff
    ns = sb_e.shape[0]
    half = d // 2
    blocks_per_expert = cap // r_rows

    def jmap(s, j, nch):
        return jnp.where(nch[s] > 0, j, nj - 1)

    kern = functools.partial(_expert_kernel, ch=ch, ff=ff)
    return pl.pallas_call(
        kern,
        grid_spec=pltpu.PrefetchScalarGridSpec(
            num_scalar_prefetch=3,
            grid=(ns, nj),
            in_specs=[
                pl.BlockSpec((r_rows, half),
                             lambda s, j, e, b, n: (e[s] * blocks_per_expert + b[s], 0)),
                pl.BlockSpec((None, d, 2 * ff), lambda s, j, e, b, n: (e[s], 0, jmap(s, j, n))),
                pl.BlockSpec((None, 1, 2 * ff), lambda s, j, e, b, n: (e[s], 0, jmap(s, j, n))),
                pl.BlockSpec((None, ff, d), lambda s, j, e, b, n: (e[s], jmap(s, j, n), 0)),
                pl.BlockSpec((None, 1, d), lambda s, j, e, b, n: (e[s], 0, 0)),
            ],
            out_specs=pl.BlockSpec((r_rows, d),
                                   lambda s, j, e, b, n: (e[s] * blocks_per_expert + b[s], 0)),
            scratch_shapes=[
                pltpu.VMEM((r_rows, d), BF16),
                pltpu.VMEM((d, 2 * ff), BF16),
                pltpu.VMEM((ff, d), BF16),
            ],
        ),
        out_shape=jax.ShapeDtypeStruct((ne * cap, d), F32),
        compiler_params=_params(("arbitrary", "arbitrary")),
        name="experts",
    )(sb_e, sb_blk, sb_nch, xs, w1, b1.reshape(ne, 1, ff2), w2, b2.reshape(ne, 1, d))


def _combine_kernel(slot_ref, x2_ref, rw_ref, g_ref, ys_ref, o_ref, gbuf, sem, *, tm, final_norm):
    t = pl.program_id(0)
    base = t * tm

    def row_copy(r, k):
        return pltpu.make_async_copy(
            ys_ref.at[pl.ds(slot_ref[(base + r) * TOP_K + k], 1), :],
            gbuf.at[pl.ds(k * tm + r, 1), :],
            sem)

    def start(r, c):
        for k in range(TOP_K):
            row_copy(r, k).start()
        return c

    def wait(r, c):
        for k in range(TOP_K):
            row_copy(r, k).wait()
        return c

    lax.fori_loop(0, tm, start, 0)
    lax.fori_loop(0, tm, wait, 0)

    rw = rw_ref[...]
    out = x2_ref[...]
    for k in range(TOP_K):
        out = out + rw[:, k:k + 1] * gbuf[pl.ds(k * tm, tm), :]
    if final_norm:
        out = _rms(out, g_ref[...])
    o_ref[...] = out


def _combine(slots, x2, rw, g, ys, final_norm):
    s, d = x2.shape
    tm = _pick(s, (256, 128))
    kern = functools.partial(_combine_kernel, tm=tm, final_norm=final_norm)
    return pl.pallas_call(
        kern,
        grid_spec=pltpu.PrefetchScalarGridSpec(
            num_scalar_prefetch=1,
            grid=(s // tm,),
            in_specs=[
                pl.BlockSpec((tm, d), lambda t, sl: (t, 0)),
                pl.BlockSpec((tm, LANES), lambda t, sl: (t, 0)),
                pl.BlockSpec((1, d), lambda t, sl: (0, 0)),
                pl.BlockSpec(memory_space=pl.ANY),
            ],
            out_specs=pl.BlockSpec((tm, d), lambda t, sl: (t, 0)),
            scratch_shapes=[pltpu.VMEM((TOP_K * tm, d), F32), pltpu.SemaphoreType.DMA(())],
        ),
        out_shape=jax.ShapeDtypeStruct((s, d), F32),
        compiler_params=_params(("arbitrary",)),
        name="combine",
    )(slots, x2, rw, g, ys)


def _superblocks(counts, r_rows, ch, ns):
    nsb = (counts + r_rows - 1) // r_rows
    ends = jnp.cumsum(nsb)
    total = ends[-1]
    sidx = jnp.arange(ns, dtype=I32)
    last = jnp.maximum(total - 1, 0)
    sclip = jnp.minimum(sidx, last)
    e = jnp.minimum(jnp.searchsorted(ends, sclip, side="right"), N_EXPERTS - 1).astype(I32)
    blk = sclip - (ends[e] - nsb[e])
    rows = jnp.clip(counts[e] - blk * r_rows, 0, r_rows)
    nch = jnp.where(sidx < total, (rows + ch - 1) // ch, 0)
    return e, blk.astype(I32), nch.astype(I32)


def _layer(x, g_mix, w_in, b_forget, b_gate, w_pa, w_pb, w_out, g_ffn,
           w_router, b_router, w1, b1, w2, b2, g_out, final_norm):
    s, d = x.shape
    qkv = 6 * D_HEADS
    tk = 128

    w_main = jnp.concatenate([w_in[:, :qkv], w_in[:, qkv + N_HEADS:]], axis=1).astype(BF16)
    w_f = jnp.pad(w_in[:, qkv:qkv + N_HEADS], ((0, 0), (0, LANES - N_HEADS))).astype(BF16)
    p, f_log = _in_proj(x, g_mix.reshape(1, d), w_main, w_f)

    log_f = jax.nn.log_sigmoid(f_log[:, :N_HEADS] + b_forget)
    cum_f = jnp.cumsum(log_f, axis=0).T
    cum_q = cum_f.reshape(N_HEADS, 1, s)
    cum_k = jnp.broadcast_to(cum_f[:, :, None], (N_HEADS, s, LANES))

    nb = D_HEADS // HEAD_DIM
    o_sb = _attention(p, _heads_t(p[:, 2 * D_HEADS:3 * D_HEADS], tk), 0, nb)
    o_fx = _attention(p, _heads_t(p[:, 5 * D_HEADS:6 * D_HEADS], tk), 3 * nb, 4 * nb,
                      cum_f=(cum_q, cum_k))

    r_rows = 1024
    ch = 256
    cap = -(-(s + ch) // r_rows) * r_rows
    ns =(s * TOP_K) // r_rows + N_EXPERTS
    wr = jnp.pad(w_router, ((0, 0), (0, LANES - N_EXPERTS)))
    wr_hi = wr.astype(BF16)
    wr_lo = (wr - wr_hi.astype(F32)).astype(BF16)
    br = jnp.pad(b_router, (0, LANES - N_EXPERTS), constant_values=NEG_BIG).reshape(1, LANES)
    x2, hp, ri, rw, cnt = _mix_route(
        o_sb, o_fx, p, qkv // d, b_gate.reshape(1, 2 * d), w_pa.astype(BF16), w_pb.astype(BF16),
        w_out.astype(BF16), x, g_ffn.reshape(1, d), wr_hi, wr_lo, br, cap)

    slots = ri[:, :TOP_K].reshape(-1)
    counts = cnt[0, :N_EXPERTS].astype(I32)
    xs = _dispatch(slots, counts, hp, cap, ch)
    sb_e, sb_blk, sb_nch = _superblocks(counts, r_rows, ch, ns)
    ys = _experts(sb_e, sb_blk, sb_nch, xs, w1, b1, w2, b2, r_rows, ch, cap)
    return _combine(slots, x2, rw, g_out.reshape(1, d), ys, final_norm)


def kernel(x, g_mix, w_in, b_forget, b_gate, w_proj_sb, w_proj_fox, w_out, g_ffn, w_router,
           b_router, w_mlp1, b_mlp1, w_mlp2, b_mlp2, g_final):
    b, s, d = x.shape
    depth = g_mix.shape[0]
    outs = []
    for bi in range(b):
        xb = x[bi]
        for layer in range(depth):
            last = layer == depth - 1
            xb = _layer(xb, g_mix[layer], w_in[layer], b_forget[layer], b_gate[layer],
                        w_proj_sb[layer], w_proj_fox[layer], w_out[layer], g_ffn[layer],
                        w_router[layer], b_router[layer], w_mlp1[layer], b_mlp1[layer],
                        w_mlp2[layer], b_mlp2[layer], g_final, last)
        outs.append(xb)
    return jnp.stack(outs, axis=0)
```

```python
import functools

import jax
import jax.numpy as jnp
from jax import lax
from jax.experimental import pallas as pl
from jax.experimental.pallas import tpu as pltpu

F32 = jnp.float32
BF16 = jnp.bfloat16
U32 = jnp.uint32
I32 = jnp.int32

LANES = 128
SUBLANES = 8
HEAD_DIM = 128
N_HEADS = 8
D_HEADS = N_HEADS * HEAD_DIM
N_EXPERTS = 32
TOP_K = 4
SWIGLU_LIMIT = 7.0
SWIGLU_ALPHA = 1.702
RMS_EPS = 1e-5
NEG_BIG = -1e30
VMEM_LIMIT = 56 * 1024 * 1024


def _pick(n, candidates):
    for c in candidates:
        if n % c == 0:
            return c
    raise ValueError(f"no tile in {candidates} divides {n}")


def _params(sem, vmem=VMEM_LIMIT):
    return pltpu.CompilerParams(dimension_semantics=sem, vmem_limit_bytes=vmem)


def _rms(x, g):
    ms = jnp.mean(x * x, axis=-1, keepdims=True)
    return x * lax.rsqrt(ms + RMS_EPS) * g


def _inproj_kernel(x_ref, g_ref, w_ref, wf_ref, p_ref, f_ref, h_scr):
    @pl.when(pl.program_id(1) == 0)
    def _():
        hb = _rms(x_ref[...], g_ref[...]).astype(BF16)
        h_scr[...] = hb
        f_ref[...] = jnp.dot(hb, wf_ref[...], preferred_element_type=F32)

    p_ref[...] = jnp.dot(h_scr[...], w_ref[...], preferred_element_type=F32).astype(BF16)


def _in_proj(x, g, w_main, w_f):
    s, d = x.shape
    n = w_main.shape[1]
    tm = _pick(s, (1024, 512, 256, 128))
    tn = _pick(n, (1024, 512, 256, 128))
    return pl.pallas_call(
        _inproj_kernel,
        grid=(s // tm, n // tn),
        in_specs=[
            pl.BlockSpec((tm, d), lambda m, j: (m, 0)),
            pl.BlockSpec((1, d), lambda m, j: (0, 0)),
            pl.BlockSpec((d, tn), lambda m, j: (0, j)),
            pl.BlockSpec((d, LANES), lambda m, j: (0, 0)),
        ],
        out_specs=[
            pl.BlockSpec((tm, tn), lambda m, j: (m, j)),
            pl.BlockSpec((tm, LANES), lambda m, j: (m, 0)),
        ],
        out_shape=[
            jax.ShapeDtypeStruct((s, n), BF16),
            jax.ShapeDtypeStruct((s, LANES), F32),
        ],
        scratch_shapes=[pltpu.VMEM((tm, d), BF16)],
        compiler_params=_params(("parallel", "arbitrary")),
        name="in_proj",
    )(x, g, w_main, w_f)


_NT = (((1,), (1,)), ((), ()))


def _sb_kernel(q_ref, k_ref, vt_ref, o_ref, acc_scr, *, tq, tk, scale):
    i = pl.program_id(1)
    nsub = tq // tk
    q = q_ref[...]
    row = lax.broadcasted_iota(I32, (tk, 2 * tk), 0)
    col = lax.broadcasted_iota(I32, (tk, 2 * tk), 1)
    ut2 = ((col % tk) >= row).astype(BF16)
    diff = lax.broadcasted_iota(I32, (tk, tq), 1) - lax.broadcasted_iota(I32, (tk, tq), 0)

    def sub_block(kb, csum, mask):
        k = k_ref[pl.ds(pl.multiple_of(kb * tk, tk), tk), :]
        z = lax.dot_general(k, q, _NT, preferred_element_type=F32) * scale
        lk = -(jnp.maximum(z, 0.0) + jnp.log(1.0 + jnp.exp(-jnp.abs(z))))
        if mask is not None:
            lk = jnp.where(mask, lk, 0.0)
        hi = lk.astype(BF16)
        lo = (lk - hi.astype(F32)).astype(BF16)
        cs = jnp.dot(ut2, jnp.concatenate([hi, lo], axis=0), preferred_element_type=F32)
        a = jnp.exp(z + cs + csum)
        if mask is not None:
            a = jnp.where(mask, a, 0.0)
        return a.astype(BF16), csum + cs[0:1, :]

    def tile(t, csum, masked):
        parts = [None] * nsub
        for u in reversed(range(nsub)):
            parts[u], csum = sub_block(t * nsub + u, csum, (diff > u * tk) if masked else None)
        acc_scr[...] += jnp.dot(vt_ref[t], jnp.concatenate(parts, axis=0),
                                preferred_element_type=F32)
        return csum

    acc_scr[...] = jnp.zeros_like(acc_scr)
    csum = tile(i, jnp.zeros((1, tq), F32), True)
    lax.fori_loop(0, i, lambda jj, c: tile(i - 1 - jj, c, False), csum)
    o_ref[...] = acc_scr[...].T.astype(o_ref.dtype)


def _fox_kernel(q_ref, k_ref, vt_ref, fq_ref, fk_ref, o_ref, acc_scr, *, tq, tk, scale):
    i = pl.program_id(1)
    nsub = tq // tk
    q = q_ref[...]
    fq = fq_ref[...]
    diff = lax.broadcasted_iota(I32, (tk, tq), 1) - lax.broadcasted_iota(I32, (tk, tq), 0)

    def tile(t, carry, masked):
        m, l = carry
        logits = []
        for u in range(nsub):
            ks = pl.multiple_of((t * nsub + u) * tk, tk)
            z = lax.dot_general(k_ref[pl.ds(ks, tk), :], q, _NT,
                                preferred_element_type=F32) * scale
            fk = fk_ref[pl.ds(ks, tk), :]
            s = z + (fq - jnp.concatenate([fk] * (tq // LANES), axis=1))
            if masked:
                s = jnp.where(diff >= u * tk, s, NEG_BIG)
            logits.append(s)
        top = logits[0]
        for s in logits[1:]:
            top = jnp.maximum(top, s)
        m_new = jnp.maximum(m, jnp.max(top, axis=0, keepdims=True))
        alpha = jnp.exp(m - m_new)
        ps = [jnp.exp(s - m_new) for s in logits]
        tot = ps[0]
        for p in ps[1:]:
            tot = tot + p
        l = alpha * l + jnp.sum(tot, axis=0, keepdims=True)
        pcat = jnp.concatenate([p.astype(BF16) for p in ps], axis=0)
        acc_scr[...] = alpha * acc_scr[...] + jnp.dot(vt_ref[t], pcat, preferred_element_type=F32)
        return m_new, l

    acc_scr[...] = jnp.zeros_like(acc_scr)
    carry = tile(i, (jnp.full((1, tq), NEG_BIG, F32), jnp.zeros((1, tq), F32)), True)
    _, l = lax.fori_loop(0, i, lambda jj, c: tile(i - 1 - jj, c, False), carry)
    o_ref[...] = (acc_scr[...] / l).T.astype(o_ref.dtype)


def _attn_tile(s):
    return _pick(s, (512, 256, 128))


def _attention(p, vt, q_col, k_col, cum_f=None):
    s = p.shape[0]
    tq = _attn_tile(s)
    tk = 128
    scale = HEAD_DIM ** -0.5
    in_specs = [
        pl.BlockSpec((tq, HEAD_DIM), lambda h, i: (i, q_col + h)),
        pl.BlockSpec((s, HEAD_DIM), lambda h, i: (0, k_col + h)),
        pl.BlockSpec((None, s // tq, HEAD_DIM, tq), lambda h, i: (h, 0, 0, 0)),
    ]
    args = [p, p, vt]
    if cum_f is None:
        kern = functools.partial(_sb_kernel, tq=tq, tk=tk, scale=scale)
        name = "sb_attn"
    else:
        kern = functools.partial(_fox_kernel, tq=tq, tk=tk, scale=scale)
        name = "fox_attn"
        in_specs += [
            pl.BlockSpec((None, 1, tq), lambda h, i: (h, 0, i)),
            pl.BlockSpec((None, s, LANES), lambda h, i: (h, 0, 0)),
        ]
        args += list(cum_f)
    return pl.pallas_call(
        kern,
        grid=(N_HEADS, s // tq),
        in_specs=in_specs,
        out_specs=pl.BlockSpec((tq, HEAD_DIM), lambda h, i: (i, h)),
        out_shape=jax.ShapeDtypeStruct((s, D_HEADS), BF16),
        scratch_shapes=[pltpu.VMEM((HEAD_DIM, tq), F32)],
        compiler_params=_params(("parallel", "arbitrary")),
        name=name,
    )(*args)


def _heads_t(v):
    s = v.shape[0]
    tq = _attn_tile(s)
    return v.reshape(s // tq, tq, N_HEADS, HEAD_DIM).transpose(2, 0, 3, 1)


def _bf16_pair_pack(h):
    bits = lax.bitcast_convert_type(h, U32)
    rnd = bits + jnp.uint32(0x7FFF) + ((bits >> 16) & jnp.uint32(1))
    half = h.shape[1] // 2
    return (rnd[:, half:] & jnp.uint32(0xFFFF0000)) | (rnd[:, :half] >> 16)


def _bf16_pair_unpack(w):
    lo = lax.bitcast_convert_type(w << 16, F32)
    hi = lax.bitcast_convert_type(w & jnp.uint32(0xFFFF0000), F32)
    return jnp.concatenate([lo, hi], axis=1).astype(BF16)


def _mix_kernel(osb_ref, ofx_ref, ga_ref, gb_ref, bg_ref, wpa_ref, wpb_ref, wo_ref, x_ref, g_ref,
                wrh_ref, wrl_ref, br_ref,
                x2_ref, hp_ref, ri_ref, rw_ref, cnt_ref, cnt_scr, *, tm, d, cap):
    t = pl.program_id(0)

    @pl.when(t == 0)
    def _():
        cnt_scr[...] = jnp.zeros_like(cnt_scr)

    bg = bg_ref[...]
    pa = jnp.dot(osb_ref[...], wpa_ref[...], preferred_element_type=F32)
    pb = jnp.dot(ofx_ref[...], wpb_ref[...], preferred_element_type=F32)
    ga = jax.nn.sigmoid(ga_ref[...].astype(F32) + bg[:, :d])
    gb = jax.nn.sigmoid(gb_ref[...].astype(F32) + bg[:, d:])
    mixed = (ga * pa + gb * pb).astype(BF16)
    x2 = x_ref[...] + jnp.dot(mixed, wo_ref[...], preferred_element_type=F32)
    x2_ref[...] = x2
    h2 = _rms(x2, g_ref[...])
    hp_ref[...] = _bf16_pair_pack(h2)

    hh = h2.astype(BF16)
    hl = (h2 - hh.astype(F32)).astype(BF16)
    logits = (jnp.dot(hh, wrh_ref[...], preferred_element_type=F32)
              + jnp.dot(hh, wrl_ref[...], preferred_element_type=F32)
              + jnp.dot(hl, wrh_ref[...], preferred_element_type=F32)
              + br_ref[...])

    lane = lax.broadcasted_iota(I32, (tm, LANES), 1).astype(F32)
    work = logits
    sel = jnp.zeros((tm, LANES), F32)
    vals, idxs = [], []
    for _ in range(TOP_K):
        mx = jnp.max(work, axis=1, keepdims=True)
        idx = jnp.min(jnp.where(work == mx, lane, float(LANES)), axis=1, keepdims=True)
        hit = lane == idx
        vals.append(mx)
        idxs.append(idx)
        work = jnp.where(hit, -jnp.inf, work)
        sel = jnp.where(hit, 1.0, sel)
    exps = [jnp.exp(v - vals[0]) for v in vals]
    denom = exps[0] + exps[1] + exps[2] + exps[3]

    row = lax.broadcasted_iota(I32, (tm, tm), 0)
    col = lax.broadcasted_iota(I32, (tm, tm), 1)
    tri = (col < row).astype(BF16)
    rank = jnp.dot(tri, sel.astype(BF16), preferred_element_type=F32) + cnt_scr[...]
    cnt_new = cnt_scr[...] + jnp.sum(sel, axis=0, keepdims=True)
    cnt_scr[...] = cnt_new
    cnt_ref[...] = cnt_new
    slot = rank + lane * float(cap)
    ri = jnp.zeros((tm, LANES), F32)
    rw = jnp.zeros((tm, LANES), F32)
    for r in range(TOP_K):
        slot_r = jnp.sum(jnp.where(lane == idxs[r], slot, 0.0), axis=1, keepdims=True)
        ri = jnp.where(lane == float(r), slot_r, ri)
        rw = jnp.where(lane == float(r), exps[r] / denom, rw)
    ri_ref[...] = ri.astype(I32)
    rw_ref[...] = rw


def _mix_route(o_sb, o_fx, p, gate_col, b_gate, w_pa, w_pb, w_out, x, g_ffn, wr_hi, wr_lo, br, cap):
    s, d = x.shape
    tm = _pick(s, (256, 128))
    const = lambda shape: pl.BlockSpec(shape, lambda t: (0, 0), pipeline_mode=pl.Buffered(1))
    kern = functools.partial(_mix_kernel, tm=tm, d=d, cap=cap)
    return pl.pallas_call(
        kern,
        grid=(s // tm,),
        in_specs=[
            pl.BlockSpec((tm, D_HEADS), lambda t: (t, 0)),
            pl.BlockSpec((tm, D_HEADS), lambda t: (t, 0)),
            pl.BlockSpec((tm, d), lambda t: (t, gate_col)),
            pl.BlockSpec((tm, d), lambda t: (t, gate_col + 1)),
            const((1, 2 * d)),
            const((D_HEADS, d)),
            const((D_HEADS, d)),
            const((d, d)),
            pl.BlockSpec((tm, d), lambda t: (t, 0)),
            const((1, d)),
            const((d, LANES)),
            const((d, LANES)),
            const((1, LANES)),
        ],
        out_specs=[
            pl.BlockSpec((tm, d), lambda t: (t, 0)),
            pl.BlockSpec((tm, d // 2), lambda t: (t, 0)),
            pl.BlockSpec((tm, LANES), lambda t: (t, 0)),
            pl.BlockSpec((tm, LANES), lambda t: (t, 0)),
            pl.BlockSpec((1, LANES), lambda t: (0, 0)),
        ],
        out_shape=[
            jax.ShapeDtypeStruct((s, d), F32),
            jax.ShapeDtypeStruct((s, d // 2), U32),
            jax.ShapeDtypeStruct((s, LANES), I32),
            jax.ShapeDtypeStruct((s, LANES), F32),
            jax.ShapeDtypeStruct((1, LANES), F32),
        ],
        scratch_shapes=[pltpu.VMEM((1, LANES), F32)],
        compiler_params=_params(("arbitrary",)),
        name="mix_route",
    )(o_sb, o_fx, p, p, b_gate, w_pa, w_pb, w_out, x, g_ffn, wr_hi, wr_lo, br)


def _dispatch_kernel(slot_ref, cnt_ref, hp_ref, xs_ref, zero_scr, sem, zsem, *, tm, ch, cap):
    t = pl.program_id(0)
    base = t * tm

    def row_copy(r, k):
        return pltpu.make_async_copy(
            hp_ref.at[pl.ds(r, 1), :],
            xs_ref.at[pl.ds(slot_ref[(base + r) * TOP_K + k], 1), :],
            sem)

    def start(r, c):
        for k in range(TOP_K):
            row_copy(r, k).start()
        return c

    def wait(r, c):
        for k in range(TOP_K):
            row_copy(r, k).wait()
        return c

    @pl.when(t == 0)
    def _():
        zero_scr[...] = jnp.zeros_like(zero_scr)

        def pad_copy(e):
            first = pl.multiple_of(e * cap + (cnt_ref[e] // SUBLANES) * SUBLANES, SUBLANES)
            return pltpu.make_async_copy(zero_scr, xs_ref.at[pl.ds(first, ch), :], zsem)

        def zstart(e, c):
            pad_copy(e).start()
            return c

        def zwait(e, c):
            pad_copy(e).wait()
            return c

        lax.fori_loop(0, N_EXPERTS, zstart, 0)
        lax.fori_loop(0, N_EXPERTS, zwait, 0)

    lax.fori_loop(0, tm, start, 0)
    lax.fori_loop(0, tm, wait, 0)


def _dispatch(slots, counts, hp, cap, ch):
    s, half = hp.shape
    tm = _pick(s, (512, 256, 128))
    kern = functools.partial(_dispatch_kernel, tm=tm, ch=ch, cap=cap)
    return pl.pallas_call(
        kern,
        grid_spec=pltpu.PrefetchScalarGridSpec(
            num_scalar_prefetch=2,
            grid=(s // tm,),
            in_specs=[pl.BlockSpec((tm, half), lambda t, sl, cn: (t, 0))],
            out_specs=pl.BlockSpec(memory_space=pl.ANY),
            scratch_shapes=[
                pltpu.VMEM((ch, half), U32),
                pltpu.SemaphoreType.DMA(()),
                pltpu.SemaphoreType.DMA(()),
            ],
        ),
        out_shape=jax.ShapeDtypeStruct((N_EXPERTS * cap, half), U32),
        compiler_params=_params(("arbitrary",)),
        name="dispatch",
    )(slots, counts, hp)


def _expert_kernel(sbe_ref, sbblk_ref, sbnch_ref, x_ref, w1_ref, b1_ref, w2_ref, b2_ref, y_ref,
                   xb_scr, w1b_scr, w2b_scr, *, ch, ff):
    s = pl.program_id(0)
    j = pl.program_id(1)
    nch = sbnch_ref[s]
    half = LANES // 2
    even = (lax.broadcasted_iota(I32, (ch, LANES), 1) % 2) == 0

    def half_act(g):
        lin = pltpu.roll(g, LANES - 1, 1)
        gate = jnp.minimum(g, SWIGLU_LIMIT)
        lin = jnp.clip(lin, -SWIGLU_LIMIT, SWIGLU_LIMIT)
        return gate * jax.nn.sigmoid(SWIGLU_ALPHA * gate) * (lin + 1.0)

    @pl.when(nch > 0)
    def _():
        @pl.when(j == 0)
        def _():
            xb_scr[...] = _bf16_pair_unpack(x_ref[...])

        w1b_scr[...] = w1_ref[...].astype(BF16)
        prow = lax.broadcasted_iota(I32, (LANES, LANES), 0)
        pcol = lax.broadcasted_iota(I32, (LANES, LANES), 1)
        perm = (pcol == prow // 2 + half * (prow % 2)).astype(BF16)
        for u in range(ff // LANES):
            grp = w2_ref[pl.ds(u * LANES, LANES), :].astype(BF16)
            w2b_scr[pl.ds(u * LANES, LANES), :] = jnp.dot(
                perm, grp, preferred_element_type=F32).astype(BF16)

        def chunk(c, carry):
            r0 = pl.multiple_of(c * ch, ch)
            xk = xb_scr[pl.ds(r0, ch), :]
            gu = jnp.dot(xk, w1b_scr[...], preferred_element_type=F32) + b1_ref[...]
            pieces = []
            for u in range(ff // LANES):
                a = half_act(gu[:, (2 * u) * LANES:(2 * u + 1) * LANES])
                b = half_act(gu[:, (2 * u + 1) * LANES:(2 * u + 2) * LANES])
                pieces.append(jnp.where(even, a, pltpu.roll(b, 1, 1)))
            act = jnp.concatenate(pieces, axis=1).astype(BF16)
            yk = jnp.dot(act, w2b_scr[...], preferred_element_type=F32)

            @pl.when(j == 0)
            def _():
                y_ref[pl.ds(r0, ch), :] = yk + b2_ref[...]

            @pl.when(j > 0)
            def _():
                y_ref[pl.ds(r0, ch), :] += yk

            return carry

        lax.fori_loop(0, nch, chunk, 0)


def _experts(sb_e, sb_blk, sb_nch, xs, w1, b1, w2, b2, r_rows, ch, cap):
    ne, d, ff2 = w1.shape
    dff = ff2 // 2
    ff = _pick(dff, (256, 128))
    nj = dff // ff
    ns = sb_e.shape[0]
    half = d // 2
    blocks_per_expert = cap // r_rows

    def jmap(s, j, nch):
        return jnp.where(nch[s] > 0, j, nj - 1)

    kern = functools.partial(_expert_kernel, ch=ch, ff=ff)
    return pl.pallas_call(
        kern,
        grid_spec=pltpu.PrefetchScalarGridSpec(
            num_scalar_prefetch=3,
            grid=(ns, nj),
            in_specs=[
                pl.BlockSpec((r_rows, half),
                             lambda s, j, e, b, n: (e[s] * blocks_per_expert + b[s], 0)),
                pl.BlockSpec((None, d, 2 * ff), lambda s, j, e, b, n: (e[s], 0, jmap(s, j, n))),
                pl.BlockSpec((None, 1, 2 * ff), lambda s, j, e, b, n: (e[s], 0, jmap(s, j, n))),
                pl.BlockSpec((None, ff, d), lambda s, j, e, b, n: (e[s], jmap(s, j, n), 0)),
                pl.BlockSpec((None, 1, d), lambda s, j, e, b, n: (e[s], 0, 0)),
            ],
            out_specs=pl.BlockSpec((r_rows, d),
                                   lambda s, j, e, b, n: (e[s] * blocks_per_expert + b[s], 0)),
            scratch_shapes=[
                pltpu.VMEM((r_rows, d), BF16),
                pltpu.VMEM((d, 2 * ff), BF16),
                pltpu.VMEM((ff, d), BF16),
            ],
        ),
        out_shape=jax.ShapeDtypeStruct((ne * cap, d), F32),
        compiler_params=_params(("arbitrary", "arbitrary")),
        name="experts",
    )(sb_e, sb_blk, sb_nch, xs, w1, b1.reshape(ne, 1, ff2), w2, b2.reshape(ne, 1, d))


def _combine_kernel(slot_ref, x2_ref, rw_ref, g_ref, ys_ref, o_ref, gbuf, sem, *, tm, final_norm):
    t = pl.program_id(0)
    base = t * tm

    def row_copy(r, k):
        return pltpu.make_async_copy(
            ys_ref.at[pl.ds(slot_ref[(base + r) * TOP_K + k], 1), :],
            gbuf.at[pl.ds(k * tm + r, 1), :],
            sem)

    def start(r, c):
        for k in range(TOP_K):
            row_copy(r, k).start()
        return c

    def wait(r, c):
        for k in range(TOP_K):
            row_copy(r, k).wait()
        return c

    lax.fori_loop(0, tm, start, 0)
    lax.fori_loop(0, tm, wait, 0)

    rw = rw_ref[...]
    out = x2_ref[...]
    for k in range(TOP_K):
        out = out + rw[:, k:k + 1] * gbuf[pl.ds(k * tm, tm), :]
    if final_norm:
        out = _rms(out, g_ref[...])
    o_ref[...] = out


def _combine(slots, x2, rw, g, ys, final_norm):
    s, d = x2.shape
    tm = _pick(s, (256, 128))
    kern = functools.partial(_combine_kernel, tm=tm, final_norm=final_norm)
    return pl.pallas_call(
        kern,
        grid_spec=pltpu.PrefetchScalarGridSpec(
            num_scalar_prefetch=1,
            grid=(s // tm,),
            in_specs=[
                pl.BlockSpec((tm, d), lambda t, sl: (t, 0)),
                pl.BlockSpec((tm, LANES), lambda t, sl: (t, 0)),
                pl.BlockSpec((1, d), lambda t, sl: (0, 0)),
                pl.BlockSpec(memory_space=pl.ANY),
            ],
            out_specs=pl.BlockSpec((tm, d), lambda t, sl: (t, 0)),
            scratch_shapes=[pltpu.VMEM((TOP_K * tm, d), F32), pltpu.SemaphoreType.DMA(())],
        ),
        out_shape=jax.ShapeDtypeStruct((s, d), F32),
        compiler_params=_params(("arbitrary",)),
        name="combine",
    )(slots, x2, rw, g, ys)


def _superblocks(counts, r_rows, ch, ns):
    nsb = (counts + r_rows - 1) // r_rows
    ends = jnp.cumsum(nsb)
    total = ends[-1]
    sidx = jnp.arange(ns, dtype=I32)
    last = jnp.maximum(total - 1, 0)
    sclip = jnp.minimum(sidx, last)
    e = jnp.minimum(jnp.searchsorted(ends, sclip, side="right"), N_EXPERTS - 1).astype(I32)
    blk = sclip - (ends[e] - nsb[e])
    rows = jnp.clip(counts[e] - blk * r_rows, 0, r_rows)
    nch = jnp.where(sidx < total, (rows + ch - 1) // ch, 0)
    return e, blk.astype(I32), nch.astype(I32)


def _layer(x, g_mix, w_in, b_forget, b_gate, w_pa, w_pb, w_out, g_ffn,
           w_router, b_router, w1, b1, w2, b2, g_out, final_norm):
    s, d = x.shape
    qkv = 6 * D_HEADS

    w_main = jnp.concatenate([w_in[:, :qkv], w_in[:, qkv + N_HEADS:]], axis=1).astype(BF16)
    w_f = jnp.pad(w_in[:, qkv:qkv + N_HEADS], ((0, 0), (0, LANES - N_HEADS))).astype(BF16)
    p, f_log = _in_proj(x, g_mix.reshape(1, d), w_main, w_f)

    log_f = jax.nn.log_sigmoid(f_log[:, :N_HEADS] + b_forget)
    cum_f = jnp.cumsum(log_f, axis=0).T
    cum_q = cum_f.reshape(N_HEADS, 1, s)
    cum_k = jnp.broadcast_to(cum_f[:, :, None], (N_HEADS, s, LANES))

    nb = D_HEADS // HEAD_DIM
    o_sb = _attention(p, _heads_t(p[:, 2 * D_HEADS:3 * D_HEADS]), 0, nb)
    o_fx = _attention(p, _heads_t(p[:, 5 * D_HEADS:6 * D_HEADS]), 3 * nb, 4 * nb,
                      cum_f=(cum_q, cum_k))

    r_rows = 1024
    ch = 256
    cap = -(-(s + ch) // r_rows) * r_rows
    ns =(s * TOP_K) // r_rows + N_EXPERTS
    wr = jnp.pad(w_router, ((0, 0), (0, LANES - N_EXPERTS)))
    wr_hi = wr.astype(BF16)
    wr_lo = (wr - wr_hi.astype(F32)).astype(BF16)
    br = jnp.pad(b_router, (0, LANES - N_EXPERTS), constant_values=NEG_BIG).reshape(1, LANES)
    x2, hp, ri, rw, cnt = _mix_route(
        o_sb, o_fx, p, qkv // d, b_gate.reshape(1, 2 * d), w_pa.astype(BF16), w_pb.astype(BF16),
        w_out.astype(BF16), x, g_ffn.reshape(1, d), wr_hi, wr_lo, br, cap)

    slots = ri[:, :TOP_K].reshape(-1)
    counts = cnt[0, :N_EXPERTS].astype(I32)
    xs = _dispatch(slots, counts, hp, cap, ch)
    sb_e, sb_blk, sb_nch = _superblocks(counts, r_rows, ch, ns)
    ys = _experts(sb_e, sb_blk, sb_nch, xs, w1, b1, w2, b2, r_rows, ch, cap)
    return _combine(slots, x2, rw, g_out.reshape(1, d), ys, final_norm)


def kernel(x, g_mix, w_in, b_forget, b_gate, w_proj_sb, w_proj_fox, w_out, g_ffn, w_router,
           b_router, w_mlp1, b_mlp1, w_mlp2, b_mlp2, g_final):
    b, s, d = x.shape
    depth = g_mix.shape[0]
    outs = []
    for bi in range(b):
        xb = x[bi]
        for layer in range(depth):
            last = layer == depth - 1
            xb = _layer(xb, g_mix[layer], w_in[layer], b_forget[layer], b_gate[layer],
                        w_proj_sb[layer], w_proj_fox[layer], w_out[layer], g_ffn[layer],
                        w_router[layer], b_router[layer], w_mlp1[layer], b_mlp1[layer],
                        w_mlp2[layer], b_mlp2[layer], g_final, last)
        outs.append(xb)
    return jnp.stack(outs, axis=0)
```

```python
import functools

import jax
import jax.numpy as jnp
from jax import lax
from jax.experimental import pallas as pl
from jax.experimental.pallas import tpu as pltpu

F32 = jnp.float32
BF16 = jnp.bfloat16
U32 = jnp.uint32
I32 = jnp.int32

LANES = 128
SUBLANES = 8
HEAD_DIM = 128
N_HEADS = 8
D_HEADS = N_HEADS * HEAD_DIM
N_EXPERTS = 32
TOP_K = 4
SWIGLU_LIMIT = 7.0
SWIGLU_ALPHA = 1.702
RMS_EPS = 1e-5
NEG_BIG = -1e30
F32_EXP_UNDERFLOW = 105.0
VMEM_LIMIT = 56 * 1024 * 1024


def _pick(n, candidates):
    for c in candidates:
        if n % c == 0:
            return c
    raise ValueError(f"no tile in {candidates} divides {n}")


def _params(sem, vmem=VMEM_LIMIT):
    return pltpu.CompilerParams(dimension_semantics=sem, vmem_limit_bytes=vmem)


def _rms(x, g):
    ms = jnp.mean(x * x, axis=-1, keepdims=True)
    return x * lax.rsqrt(ms + RMS_EPS) * g


def _inproj_kernel(x_ref, g_ref, w_ref, wf_ref, p_ref, f_ref, h_scr):
    @pl.when(pl.program_id(1) == 0)
    def _():
        hb = _rms(x_ref[...], g_ref[...]).astype(BF16)
        h_scr[...] = hb
        f_ref[...] = jnp.dot(hb, wf_ref[...], preferred_element_type=F32)

    p_ref[...] = jnp.dot(h_scr[...], w_ref[...], preferred_element_type=F32).astype(BF16)


def _in_proj(x, g, w_main, w_f):
    s, d = x.shape
    n = w_main.shape[1]
    tm = _pick(s, (1024, 512, 256, 128))
    tn = _pick(n, (1024, 512, 256, 128))
    return pl.pallas_call(
        _inproj_kernel,
        grid=(s // tm, n // tn),
        in_specs=[
            pl.BlockSpec((tm, d), lambda m, j: (m, 0)),
            pl.BlockSpec((1, d), lambda m, j: (0, 0)),
            pl.BlockSpec((d, tn), lambda m, j: (0, j)),
            pl.BlockSpec((d, LANES), lambda m, j: (0, 0)),
        ],
        out_specs=[
            pl.BlockSpec((tm, tn), lambda m, j: (m, j)),
            pl.BlockSpec((tm, LANES), lambda m, j: (m, 0)),
        ],
        out_shape=[
            jax.ShapeDtypeStruct((s, n), BF16),
            jax.ShapeDtypeStruct((s, LANES), F32),
        ],
        scratch_shapes=[pltpu.VMEM((tm, d), BF16)],
        compiler_params=_params(("parallel", "arbitrary")),
        name="in_proj",
    )(x, g, w_main, w_f)


_NT = (((1,), (1,)), ((), ()))


def _sb_kernel(q_ref, k_ref, vt_ref, o_ref, acc_scr, *, tq, tk, scale):
    i = pl.program_id(1)
    nsub = tq // tk
    q = q_ref[...]
    row = lax.broadcasted_iota(I32, (tk, 2 * tk), 0)
    col = lax.broadcasted_iota(I32, (tk, 2 * tk), 1)
    ut2 = ((col % tk) >= row).astype(BF16)
    diff = lax.broadcasted_iota(I32, (tk, tq), 1) - lax.broadcasted_iota(I32, (tk, tq), 0)

    def sub_block(kb, csum, mask):
        k = k_ref[pl.ds(pl.multiple_of(kb * tk, tk), tk), :]
        z = lax.dot_general(k, q, _NT, preferred_element_type=F32) * scale
        lk = -(jnp.maximum(z, 0.0) + jnp.log(1.0 + jnp.exp(-jnp.abs(z))))
        if mask is not None:
            lk = jnp.where(mask, lk, 0.0)
        hi = lk.astype(BF16)
        lo = (lk - hi.astype(F32)).astype(BF16)
        cs = jnp.dot(ut2, jnp.concatenate([hi, lo], axis=0), preferred_element_type=F32)
        a = jnp.exp(z + cs + csum)
        if mask is not None:
            a = jnp.where(mask, a, 0.0)
        return a.astype(BF16), csum + cs[0:1, :]

    def tile(t, csum, masked):
        parts = [None] * nsub
        for u in reversed(range(nsub)):
            parts[u], csum = sub_block(t * nsub + u, csum, (diff > u * tk) if masked else None)
        acc_scr[...] += jnp.dot(vt_ref[t], jnp.concatenate(parts, axis=0),
                                preferred_element_type=F32)
        return csum

    acc_scr[...] = jnp.zeros_like(acc_scr)
    csum = tile(i, jnp.zeros((1, tq), F32), True)

    def live(state):
        jj, c = state
        return jnp.logical_and(jj < i, jnp.max(c) > -F32_EXP_UNDERFLOW)

    lax.while_loop(live, lambda st: (st[0] + 1, tile(i - 1 - st[0], st[1], False)),
                   (jnp.int32(0), csum))
    o_ref[...] = acc_scr[...].T.astype(o_ref.dtype)


def _fox_kernel(q_ref, k_ref, vt_ref, fq_ref, fk_ref, o_ref, acc_scr, *, tq, tk, scale):
    i = pl.program_id(1)
    nsub = tq // tk
    q = q_ref[...]
    fq = fq_ref[...]
    diff = lax.broadcasted_iota(I32, (tk, tq), 1) - lax.broadcasted_iota(I32, (tk, tq), 0)

    def tile(t, carry, masked):
        m, l = carry
        logits = []
        for u in range(nsub):
            ks = pl.multiple_of((t * nsub + u) * tk, tk)
            z = lax.dot_general(k_ref[pl.ds(ks, tk), :], q, _NT,
                                preferred_element_type=F32) * scale
            fk = fk_ref[pl.ds(ks, tk), :]
            s = z + (fq - jnp.concatenate([fk] * (tq // LANES), axis=1))
            if masked:
                s = jnp.where(diff >= u * tk, s, NEG_BIG)
            logits.append(s)
        top = logits[0]
        for s in logits[1:]:
            top = jnp.maximum(top, s)
        m_new = jnp.maximum(m, jnp.max(top, axis=0, keepdims=True))
        alpha = jnp.exp(m - m_new)
        ps = [jnp.exp(s - m_new) for s in logits]
        tot = ps[0]
        for p in ps[1:]:
            tot = tot + p
        l = alpha * l + jnp.sum(tot, axis=0, keepdims=True)
        pcat = jnp.concatenate([p.astype(BF16) for p in ps], axis=0)
        acc_scr[...] = alpha * acc_scr[...] + jnp.dot(vt_ref[t], pcat, preferred_element_type=F32)
        return m_new, l

    acc_scr[...] = jnp.zeros_like(acc_scr)
    carry = tile(i, (jnp.full((1, tq), NEG_BIG, F32), jnp.zeros((1, tq), F32)), True)
    _, l = lax.fori_loop(0, i, lambda jj, c: tile(i - 1 - jj, c, False), carry)
    o_ref[...] = (acc_scr[...] / l).T.astype(o_ref.dtype)


def _attn_tile(s):
    return _pick(s, (512, 256, 128))


def _attention(p, vt, q_col, k_col, cum_f=None):
    s = p.shape[0]
    tq = _attn_tile(s)
    tk = 128
    scale = HEAD_DIM ** -0.5
    in_specs = [
        pl.BlockSpec((tq, HEAD_DIM), lambda h, i: (i, q_col + h)),
        pl.BlockSpec((s, HEAD_DIM), lambda h, i: (0, k_col + h)),
        pl.BlockSpec((None, s // tq, HEAD_DIM, tq), lambda h, i: (h, 0, 0, 0)),
    ]
    args = [p, p, vt]
    if cum_f is None:
        kern = functools.partial(_sb_kernel, tq=tq, tk=tk, scale=scale)
        name = "sb_attn"
    else:
        kern = functools.partial(_fox_kernel, tq=tq, tk=tk, scale=scale)
        name = "fox_attn"
        in_specs += [
            pl.BlockSpec((None, 1, tq), lambda h, i: (h, 0, i)),
            pl.BlockSpec((None, s, LANES), lambda h, i: (h, 0, 0)),
        ]
        args += list(cum_f)
    return pl.pallas_call(
        kern,
        grid=(N_HEADS, s // tq),
        in_specs=in_specs,
        out_specs=pl.BlockSpec((tq, HEAD_DIM), lambda h, i: (i, h)),
        out_shape=jax.ShapeDtypeStruct((s, D_HEADS), BF16),
        scratch_shapes=[pltpu.VMEM((HEAD_DIM, tq), F32)],
        compiler_params=_params(("parallel", "arbitrary")),
        name=name,
    )(*args)


def _heads_t(v):
    s = v.shape[0]
    tq = _attn_tile(s)
    return v.reshape(s // tq, tq, N_HEADS, HEAD_DIM).transpose(2, 0, 3, 1)


def _bf16_pair_pack(h):
    bits = lax.bitcast_convert_type(h, U32)
    rnd = bits + jnp.uint32(0x7FFF) + ((bits >> 16) & jnp.uint32(1))
    half = h.shape[1] // 2
    return (rnd[:, half:] & jnp.uint32(0xFFFF0000)) | (rnd[:, :half] >> 16)


def _bf16_pair_unpack(w):
    lo = lax.bitcast_convert_type(w << 16, F32)
    hi = lax.bitcast_convert_type(w & jnp.uint32(0xFFFF0000), F32)
    return jnp.concatenate([lo, hi], axis=1).astype(BF16)


def _mix_kernel(osb_ref, ofx_ref, ga_ref, gb_ref, bg_ref, wpa_ref, wpb_ref, wo_ref, x_ref, g_ref,
                wrh_ref, wrl_ref, br_ref,
                x2_ref, hp_ref, ri_ref, rw_ref, cnt_ref, cnt_scr, *, tm, d, cap):
    t = pl.program_id(0)

    @pl.when(t == 0)
    def _():
        cnt_scr[...] = jnp.zeros_like(cnt_scr)

    bg = bg_ref[...]
    pa = jnp.dot(osb_ref[...], wpa_ref[...], preferred_element_type=F32)
    pb = jnp.dot(ofx_ref[...], wpb_ref[...], preferred_element_type=F32)
    ga = jax.nn.sigmoid(ga_ref[...].astype(F32) + bg[:, :d])
    gb = jax.nn.sigmoid(gb_ref[...].astype(F32) + bg[:, d:])
    mixed = (ga * pa + gb * pb).astype(BF16)
    x2 = x_ref[...] + jnp.dot(mixed, wo_ref[...], preferred_element_type=F32)
    x2_ref[...] = x2
    h2 = _rms(x2, g_ref[...])
    hp_ref[...] = _bf16_pair_pack(h2)

    hh = h2.astype(BF16)
    hl = (h2 - hh.astype(F32)).astype(BF16)
    logits = (jnp.dot(hh, wrh_ref[...], preferred_element_type=F32)
              + jnp.dot(hh, wrl_ref[...], preferred_element_type=F32)
              + jnp.dot(hl, wrh_ref[...], preferred_element_type=F32)
              + br_ref[...])

    lane = lax.broadcasted_iota(I32, (tm, LANES), 1).astype(F32)
    work = logits
    sel = jnp.zeros((tm, LANES), F32)
    vals, idxs = [], []
    for _ in range(TOP_K):
        mx = jnp.max(work, axis=1, keepdims=True)
        idx = jnp.min(jnp.where(work == mx, lane, float(LANES)), axis=1, keepdims=True)
        hit = lane == idx
        vals.append(mx)
        idxs.append(idx)
        work = jnp.where(hit, -jnp.inf, work)
        sel = jnp.where(hit, 1.0, sel)
    exps = [jnp.exp(v - vals[0]) for v in vals]
    denom = exps[0] + exps[1] + exps[2] + exps[3]

    row = lax.broadcasted_iota(I32, (tm, tm), 0)
    col = lax.broadcasted_iota(I32, (tm, tm), 1)
    tri = (col < row).astype(BF16)
    rank = jnp.dot(tri, sel.astype(BF16), preferred_element_type=F32) + cnt_scr[...]
    cnt_new = cnt_scr[...] + jnp.sum(sel, axis=0, keepdims=True)
    cnt_scr[...] = cnt_new
    cnt_ref[...] = cnt_new
    slot = rank + lane * float(cap)
    ri = jnp.zeros((tm, LANES), F32)
    rw = jnp.zeros((tm, LANES), F32)
    for r in range(TOP_K):
        slot_r = jnp.sum(jnp.where(lane == idxs[r], slot, 0.0), axis=1, keepdims=True)
        ri = jnp.where(lane == float(r), slot_r, ri)
        rw = jnp.where(lane == float(r), exps[r] / denom, rw)
    ri_ref[...] = ri.astype(I32)
    rw_ref[...] = rw


def _mix_route(o_sb, o_fx, p, gate_col, b_gate, w_pa, w_pb, w_out, x, g_ffn, wr_hi, wr_lo, br, cap):
    s, d = x.shape
    tm = _pick(s, (256, 128))
    const = lambda shape: pl.BlockSpec(shape, lambda t: (0, 0), pipeline_mode=pl.Buffered(1))
    kern = functools.partial(_mix_kernel, tm=tm, d=d, cap=cap)
    return pl.pallas_call(
        kern,
        grid=(s // tm,),
        in_specs=[
            pl.BlockSpec((tm, D_HEADS), lambda t: (t, 0)),
            pl.BlockSpec((tm, D_HEADS), lambda t: (t, 0)),
            pl.BlockSpec((tm, d), lambda t: (t, gate_col)),
            pl.BlockSpec((tm, d), lambda t: (t, gate_col + 1)),
            const((1, 2 * d)),
            const((D_HEADS, d)),
            const((D_HEADS, d)),
            const((d, d)),
            pl.BlockSpec((tm, d), lambda t: (t, 0)),
            const((1, d)),
            const((d, LANES)),
            const((d, LANES)),
            const((1, LANES)),
        ],
        out_specs=[
            pl.BlockSpec((tm, d), lambda t: (t, 0)),
            pl.BlockSpec((tm, d // 2), lambda t: (t, 0)),
            pl.BlockSpec((tm, LANES), lambda t: (t, 0)),
            pl.BlockSpec((tm, LANES), lambda t: (t, 0)),
            pl.BlockSpec((1, LANES), lambda t: (0, 0)),
        ],
        out_shape=[
            jax.ShapeDtypeStruct((s, d), F32),
            jax.ShapeDtypeStruct((s, d // 2), U32),
            jax.ShapeDtypeStruct((s, LANES), I32),
            jax.ShapeDtypeStruct((s, LANES), F32),
            jax.ShapeDtypeStruct((1, LANES), F32),
        ],
        scratch_shapes=[pltpu.VMEM((1, LANES), F32)],
        compiler_params=_params(("arbitrary",)),
        name="mix_route",
    )(o_sb, o_fx, p, p, b_gate, w_pa, w_pb, w_out, x, g_ffn, wr_hi, wr_lo, br)


def _dispatch_kernel(slot_ref, cnt_ref, hp_ref, xs_ref, zero_scr, sem, zsem, *, tm, ch, cap):
    t = pl.program_id(0)
    base = t * tm

    def row_copy(r, k):
        return pltpu.make_async_copy(
            hp_ref.at[pl.ds(r, 1), :],
            xs_ref.at[pl.ds(slot_ref[(base + r) * TOP_K + k], 1), :],
            sem)

    def start(r, c):
        for k in range(TOP_K):
            row_copy(r, k).start()
        return c

    def wait(r, c):
        for k in range(TOP_K):
            row_copy(r, k).wait()
        return c

    @pl.when(t == 0)
    def _():
        zero_scr[...] = jnp.zeros_like(zero_scr)

        def pad_copy(e):
            first = pl.multiple_of(e * cap + (cnt_ref[e] // SUBLANES) * SUBLANES, SUBLANES)
            return pltpu.make_async_copy(zero_scr, xs_ref.at[pl.ds(first, ch), :], zsem)

        def zstart(e, c):
            pad_copy(e).start()
            return c

        def zwait(e, c):
            pad_copy(e).wait()
            return c

        lax.fori_loop(0, N_EXPERTS, zstart, 0)
        lax.fori_loop(0, N_EXPERTS, zwait, 0)

    lax.fori_loop(0, tm, start, 0)
    lax.fori_loop(0, tm, wait, 0)


def _dispatch(slots, counts, hp, cap, ch):
    s, half = hp.shape
    tm = _pick(s, (512, 256, 128))
    kern = functools.partial(_dispatch_kernel, tm=tm, ch=ch, cap=cap)
    return pl.pallas_call(
        kern,
        grid_spec=pltpu.PrefetchScalarGridSpec(
            num_scalar_prefetch=2,
            grid=(s // tm,),
            in_specs=[pl.BlockSpec((tm, half), lambda t, sl, cn: (t, 0))],
            out_specs=pl.BlockSpec(memory_space=pl.ANY),
            scratch_shapes=[
                pltpu.VMEM((ch, half), U32),
                pltpu.SemaphoreType.DMA(()),
                pltpu.SemaphoreType.DMA(()),
            ],
        ),
        out_shape=jax.ShapeDtypeStruct((N_EXPERTS * cap, half), U32),
        compiler_params=_params(("arbitrary",)),
        name="dispatch",
    )(slots, counts, hp)


def _expert_kernel(sbe_ref, sbblk_ref, sbnch_ref, x_ref, w1_ref, b1_ref, w2_ref, b2_ref, y_ref,
                   xb_scr, w1b_scr, w2b_scr, *, ch, ff):
    s = pl.program_id(0)
    j = pl.program_id(1)
    nch = sbnch_ref[s]
    half = LANES // 2
    even = (lax.broadcasted_iota(I32, (ch, LANES), 1) % 2) == 0

    def half_act(g):
        lin = pltpu.roll(g, LANES - 1, 1)
        gate = jnp.minimum(g, SWIGLU_LIMIT)
        lin = jnp.clip(lin, -SWIGLU_LIMIT, SWIGLU_LIMIT)
        return gate * jax.nn.sigmoid(SWIGLU_ALPHA * gate) * (lin + 1.0)

    @pl.when(nch > 0)
    def _():
        @pl.when(j == 0)
        def _():
            xb_scr[...] = _bf16_pair_unpack(x_ref[...])

        w1b_scr[...] = w1_ref[...].astype(BF16)
        prow = lax.broadcasted_iota(I32, (LANES, LANES), 0)
        pcol = lax.broadcasted_iota(I32, (LANES, LANES), 1)
        perm = (pcol == prow // 2 + half * (prow % 2)).astype(BF16)
        for u in range(ff // LANES):
            grp = w2_ref[pl.ds(u * LANES, LANES), :].astype(BF16)
            w2b_scr[pl.ds(u * LANES, LANES), :] = jnp.dot(
                perm, grp, preferred_element_type=F32).astype(BF16)

        @pl.when(j == 0)
        def _():
            def init(c, carry):
                y_ref[pl.ds(pl.multiple_of(c * ch, ch), ch), :] = jnp.broadcast_to(
                    b2_ref[...], (ch, y_ref.shape[1]))
                return carry

            lax.fori_loop(0, nch, init, 0)

        def up(c):
            xk = xb_scr[pl.ds(pl.multiple_of(c * ch, ch), ch), :]
            gu = jnp.dot(xk, w1b_scr[...], preferred_element_type=F32) + b1_ref[...]
            pieces = []
            for u in range(ff // LANES):
                a = half_act(gu[:, (2 * u) * LANES:(2 * u + 1) * LANES])
                b = half_act(gu[:, (2 * u + 1) * LANES:(2 * u + 2) * LANES])
                pieces.append(jnp.where(even, a, pltpu.roll(b, 1, 1)))
            return jnp.concatenate(pieces, axis=1).astype(BF16)

        def down(c, act):
            r0 = pl.multiple_of(c * ch, ch)
            y_ref[pl.ds(r0, ch), :] += jnp.dot(act, w2b_scr[...], preferred_element_type=F32)

        def trip(c, act):
            down(c - 1, act)
            return up(c)

        down(nch - 1, lax.fori_loop(1, nch, trip, up(0)))


def _experts(sb_e, sb_blk, sb_nch, xs, w1, b1, w2, b2, r_rows, ch, cap):
    ne, d, ff2 = w1.shape
    dff = ff2 // 2
    ff = _pick(dff, (256, 128))
    nj = dff // ff
    ns = sb_e.shape[0]
    half = d // 2
    blocks_per_expert = cap // r_rows

    def jmap(s, j, nch):
        return jnp.where(nch[s] > 0, j, nj - 1)

    kern = functools.partial(_expert_kernel, ch=ch, ff=ff)
    return pl.pallas_call(
        kern,
        grid_spec=pltpu.PrefetchScalarGridSpec(
            num_scalar_prefetch=3,
            grid=(ns, nj),
            in_specs=[
                pl.BlockSpec((r_rows, half),
                             lambda s, j, e, b, n: (e[s] * blocks_per_expert + b[s], 0)),
                pl.BlockSpec((None, d, 2 * ff), lambda s, j, e, b, n: (e[s], 0, jmap(s, j, n))),
                pl.BlockSpec((None, 1, 2 * ff), lambda s, j, e, b, n: (e[s], 0, jmap(s, j, n))),
                pl.BlockSpec((None, ff, d), lambda s, j, e, b, n: (e[s], jmap(s, j, n), 0)),
                pl.BlockSpec((None, 1, d), lambda s, j, e, b, n: (e[s], 0, 0)),
            ],
            out_specs=pl.BlockSpec((r_rows, d),
                                   lambda s, j, e, b, n: (e[s] * blocks_per_expert + b[s], 0)),
            scratch_shapes=[
                pltpu.VMEM((r_rows, d), BF16),
                pltpu.VMEM((d, 2 * ff), BF16),
                pltpu.VMEM((ff, d), BF16),
            ],
        ),
        out_shape=jax.ShapeDtypeStruct((ne * cap, d), F32),
        compiler_params=_params(("arbitrary", "arbitrary")),
        name="experts",
    )(sb_e, sb_blk, sb_nch, xs, w1, b1.reshape(ne, 1, ff2), w2, b2.reshape(ne, 1, d))


def _combine_kernel(slot_ref, x2_ref, rw_ref, g_ref, ys_ref, o_ref, gbuf, sem, *, tm, final_norm):
    t = pl.program_id(0)
    base = t * tm

    def row_copy(r, k):
        return pltpu.make_async_copy(
            ys_ref.at[pl.ds(slot_ref[(base + r) * TOP_K + k], 1), :],
            gbuf.at[pl.ds(k * tm + r, 1), :],
            sem)

    def start(r, c):
        for k in range(TOP_K):
            row_copy(r, k).start()
        return c

    def wait(r, c):
        for k in range(TOP_K):
            row_copy(r, k).wait()
        return c

    lax.fori_loop(0, tm, start, 0)
    lax.fori_loop(0, tm, wait, 0)

    rw = rw_ref[...]
    out = x2_ref[...]
    for k in range(TOP_K):
        out = out + rw[:, k:k + 1] * gbuf[pl.ds(k * tm, tm), :]
    if final_norm:
        out = _rms(out, g_ref[...])
    o_ref[...] = out


def _combine(slots, x2, rw, g, ys, final_norm):
    s, d = x2.shape
    tm = _pick(s, (256, 128))
    kern = functools.partial(_combine_kernel, tm=tm, final_norm=final_norm)
    return pl.pallas_call(
        kern,
        grid_spec=pltpu.PrefetchScalarGridSpec(
            num_scalar_prefetch=1,
            grid=(s // tm,),
            in_specs=[
                pl.BlockSpec((tm, d), lambda t, sl: (t, 0)),
                pl.BlockSpec((tm, LANES), lambda t, sl: (t, 0)),
                pl.BlockSpec((1, d), lambda t, sl: (0, 0)),
                pl.BlockSpec(memory_space=pl.ANY),
            ],
            out_specs=pl.BlockSpec((tm, d), lambda t, sl: (t, 0)),
            scratch_shapes=[pltpu.VMEM((TOP_K * tm, d), F32), pltpu.SemaphoreType.DMA(())],
        ),
        out_shape=jax.ShapeDtypeStruct((s, d), F32),
        compiler_params=_params(("arbitrary",)),
        name="combine",
    )(slots, x2, rw, g, ys)


def _superblocks(counts, r_rows, ch, ns):
    nsb = (counts + r_rows - 1) // r_rows
    ends = jnp.cumsum(nsb)
    total = ends[-1]
    sidx = jnp.arange(ns, dtype=I32)
    last = jnp.maximum(total - 1, 0)
    sclip = jnp.minimum(sidx, last)
    e = jnp.minimum(jnp.searchsorted(ends, sclip, side="right"), N_EXPERTS - 1).astype(I32)
    blk = sclip - (ends[e] - nsb[e])
    rows = jnp.clip(counts[e] - blk * r_rows, 0, r_rows)
    nch = jnp.where(sidx < total, (rows + ch - 1) // ch, 0)
    return e, blk.astype(I32), nch.astype(I32)


def _layer(x, g_mix, w_in, b_forget, b_gate, w_pa, w_pb, w_out, g_ffn,
           w_router, b_router, w1, b1, w2, b2, g_out, final_norm):
    s, d = x.shape
    qkv = 6 * D_HEADS

    w_main = jnp.concatenate([w_in[:, :qkv], w_in[:, qkv + N_HEADS:]], axis=1).astype(BF16)
    w_f = jnp.pad(w_in[:, qkv:qkv + N_HEADS], ((0, 0), (0, LANES - N_HEADS))).astype(BF16)
    p, f_log = _in_proj(x, g_mix.reshape(1, d), w_main, w_f)

    log_f = jax.nn.log_sigmoid(f_log[:, :N_HEADS] + b_forget)
    cum_f = jnp.cumsum(log_f, axis=0).T
    cum_q = cum_f.reshape(N_HEADS, 1, s)
    cum_k = jnp.broadcast_to(cum_f[:, :, None], (N_HEADS, s, LANES))

    nb = D_HEADS // HEAD_DIM
    o_sb = _attention(p, _heads_t(p[:, 2 * D_HEADS:3 * D_HEADS]), 0, nb)
    o_fx = _attention(p, _heads_t(p[:, 5 * D_HEADS:6 * D_HEADS]), 3 * nb, 4 * nb,
                      cum_f=(cum_q, cum_k))

    r_rows = 1024
    ch = 256
    cap = -(-(s + ch) // r_rows) * r_rows
    ns =(s * TOP_K) // r_rows + N_EXPERTS
    wr = jnp.pad(w_router, ((0, 0), (0, LANES - N_EXPERTS)))
    wr_hi = wr.astype(BF16)
    wr_lo = (wr - wr_hi.astype(F32)).astype(BF16)
    br = jnp.pad(b_router, (0, LANES - N_EXPERTS), constant_values=NEG_BIG).reshape(1, LANES)
    x2, hp, ri, rw, cnt = _mix_route(
        o_sb, o_fx, p, qkv // d, b_gate.reshape(1, 2 * d), w_pa.astype(BF16), w_pb.astype(BF16),
        w_out.astype(BF16), x, g_ffn.reshape(1, d), wr_hi, wr_lo, br, cap)

    slots = ri[:, :TOP_K].reshape(-1)
    counts = cnt[0, :N_EXPERTS].astype(I32)
    xs = _dispatch(slots, counts, hp, cap, ch)
    sb_e, sb_blk, sb_nch = _superblocks(counts, r_rows, ch, ns)
    ys = _experts(sb_e, sb_blk, sb_nch, xs, w1, b1, w2, b2, r_rows, ch, cap)
    return _combine(slots, x2, rw, g_out.reshape(1, d), ys, final_norm)


def kernel(x, g_mix, w_in, b_forget, b_gate, w_proj_sb, w_proj_fox, w_out, g_ffn, w_router,
           b_router, w_mlp1, b_mlp1, w_mlp2, b_mlp2, g_final):
    b, s, d = x.shape
    depth = g_mix.shape[0]
    outs = []
    for bi in range(b):
        xb = x[bi]
        for layer in range(depth):
            last = layer == depth - 1
            xb = _layer(xb, g_mix[layer], w_in[layer], b_forget[layer], b_gate[layer],
                        w_proj_sb[layer], w_proj_fox[layer], w_out[layer], g_ffn[layer],
                        w_router[layer], b_router[layer], w_mlp1[layer], b_mlp1[layer],
                        w_mlp2[layer], b_mlp2[layer], g_final, last)
        outs.append(xb)
    return jnp.stack(outs, axis=0)
```

```python
import functools

import jax
import jax.numpy as jnp
from jax import lax
from jax.experimental import pallas as pl
from jax.experimental.pallas import tpu as pltpu

F32 = jnp.float32
BF16 = jnp.bfloat16
U32 = jnp.uint32
I32 = jnp.int32

LANES = 128
SUBLANES = 8
HEAD_DIM = 128
N_HEADS = 8
D_HEADS = N_HEADS * HEAD_DIM
N_EXPERTS = 32
TOP_K = 4
SWIGLU_LIMIT = 7.0
SWIGLU_ALPHA = 1.702
RMS_EPS = 1e-5
NEG_BIG = -1e30
ROW_DMA_UNROLL = 8
LOG2_E = 1.4426950408889634
BF16_ROWS = 16
F32_EXP_UNDERFLOW = 105.0
VMEM_LIMIT = 56 * 1024 * 1024


def _pick(n, candidates):
    for c in candidates:
        if n % c == 0:
            return c
    raise ValueError(f"no tile in {candidates} divides {n}")


def _params(sem, vmem=VMEM_LIMIT):
    return pltpu.CompilerParams(dimension_semantics=sem, vmem_limit_bytes=vmem)


def _rms(x, g):
    ms = jnp.mean(x * x, axis=-1, keepdims=True)
    return x * lax.rsqrt(ms + RMS_EPS) * g


def _inproj_kernel(x_ref, g_ref, w_ref, wf_ref, p_ref, f_ref, h_scr):
    @pl.when(pl.program_id(1) == 0)
    def _():
        hb = _rms(x_ref[...], g_ref[...]).astype(BF16)
        h_scr[...] = hb
        f_ref[...] = jnp.dot(hb, wf_ref[...], preferred_element_type=F32)

    p_ref[...] = jnp.dot(h_scr[...], w_ref[...], preferred_element_type=F32).astype(BF16)


def _in_proj(x, g, w_main, w_f):
    s, d = x.shape
    n = w_main.shape[1]
    tm = _pick(s, (1024, 512, 256, 128))
    tn = _pick(n, (1024, 512, 256, 128))
    return pl.pallas_call(
        _inproj_kernel,
        grid=(s // tm, n // tn),
        in_specs=[
            pl.BlockSpec((tm, d), lambda m, j: (m, 0)),
            pl.BlockSpec((1, d), lambda m, j: (0, 0)),
            pl.BlockSpec((d, tn), lambda m, j: (0, j)),
            pl.BlockSpec((d, LANES), lambda m, j: (0, 0)),
        ],
        out_specs=[
            pl.BlockSpec((tm, tn), lambda m, j: (m, j)),
            pl.BlockSpec((tm, LANES), lambda m, j: (m, 0)),
        ],
        out_shape=[
            jax.ShapeDtypeStruct((s, n), BF16),
            jax.ShapeDtypeStruct((s, LANES), F32),
        ],
        scratch_shapes=[pltpu.VMEM((tm, d), BF16)],
        compiler_params=_params(("parallel", "arbitrary")),
        name="in_proj",
    )(x, g, w_main, w_f)


_NT = (((1,), (1,)), ((), ()))


def _sb_kernel(q_ref, k_ref, vt_ref, o_ref, acc_scr, *, tq, tk, scale):
    i = pl.program_id(1)
    nsub = tq // tk
    q = q_ref[...]
    row = lax.broadcasted_iota(I32, (tk, 2 * tk), 0)
    col = lax.broadcasted_iota(I32, (tk, 2 * tk), 1)
    ut2 = ((col % tk) >= row).astype(BF16)
    diff = lax.broadcasted_iota(I32, (tk, tq), 1) - lax.broadcasted_iota(I32, (tk, tq), 0)

    def sub_block(kb, csum, mask):
        k = k_ref[pl.ds(pl.multiple_of(kb * tk, tk), tk), :]
        z = lax.dot_general(k, q, _NT, preferred_element_type=F32) * scale
        lk = -(jnp.maximum(z, 0.0) + jnp.log(1.0 + jnp.exp(-jnp.abs(z))))
        if mask is not None:
            lk = jnp.where(mask, lk, 0.0)
        hi = lk.astype(BF16)
        lo = (lk - hi.astype(F32)).astype(BF16)
        cs = jnp.dot(ut2, jnp.concatenate([hi, lo], axis=0), preferred_element_type=F32)
        a = jnp.exp(z + cs + csum)
        if mask is not None:
            a = jnp.where(mask, a, 0.0)
        return a.astype(BF16), csum + cs[0:1, :]

    def tile(t, csum, masked):
        parts = [None] * nsub
        for u in reversed(range(nsub)):
            parts[u], csum = sub_block(t * nsub + u, csum, (diff > u * tk) if masked else None)
        acc_scr[...] += jnp.dot(vt_ref[t], jnp.concatenate(parts, axis=0),
                                preferred_element_type=F32)
        return csum

    acc_scr[...] = jnp.zeros_like(acc_scr)
    csum = tile(i, jnp.zeros((1, tq), F32), True)

    def live(state):
        jj, c = state
        return jnp.logical_and(jj < i, jnp.max(c) > -F32_EXP_UNDERFLOW)

    lax.while_loop(live, lambda st: (st[0] + 1, tile(i - 1 - st[0], st[1], False)),
                   (jnp.int32(0), csum))
    o_ref[...] = acc_scr[...].T.astype(o_ref.dtype)


def _fox_kernel(q_ref, qa_ref, k_ref, ka_ref, vt_ref, o_ref, acc_scr, s_scr, *, tq, tk, scale):
    i = pl.program_id(1)
    nsub = tq // tk
    q = jnp.concatenate([q_ref[...], qa_ref[...]], axis=1)
    diff = lax.broadcasted_iota(I32, (tk, tq), 1) - lax.broadcasted_iota(I32, (tk, tq), 0)
    to_log2 = scale * LOG2_E

    def scores(t, slot, masked):
        for u in range(nsub):
            ks = pl.multiple_of((t * nsub + u) * tk, tk)
            k = jnp.concatenate([k_ref[pl.ds(ks, tk), :], ka_ref[pl.ds(ks, tk), :]], axis=1)
            s = lax.dot_general(k, q, _NT, preferred_element_type=F32) * to_log2
            if masked:
                s = jnp.where(diff >= u * tk, s, NEG_BIG)
            s_scr[slot, pl.ds(u * tk, tk), :] = s

    def consume(t, slot, m):
        top = s_scr[slot, pl.ds(0, tk), :]
        for u in range(1, nsub):
            top = jnp.maximum(top, s_scr[slot, pl.ds(u * tk, tk), :])
        m_new = jnp.maximum(m, jnp.max(top, axis=0, keepdims=True))
        alpha = jnp.exp2(m - m_new)
        pcat = jnp.exp2(s_scr[slot] - m_new).astype(BF16)
        acc_scr[...] = alpha * acc_scr[...] + jnp.dot(vt_ref[t], pcat, preferred_element_type=F32)
        return m_new

    def step(t, slot, m):
        scores(t - 1, 1 - slot, False)
        return consume(t, slot, m)

    def trip(k, m):
        t = i - 2 * k
        return step(t - 1, 1, step(t, 0, m))

    acc_scr[...] = jnp.zeros_like(acc_scr)
    scores(i, 0, True)
    m = lax.fori_loop(0, i // 2, trip, jnp.full((1, tq), NEG_BIG, F32))

    @pl.when(i % 2 == 0)
    def _():
        consume(0, 0, m)

    @pl.when(i % 2 == 1)
    def _():
        consume(0, 1, step(1, 0, m))

    acc = acc_scr[...]
    o_ref[...] = (acc[:HEAD_DIM] / acc[HEAD_DIM:HEAD_DIM + 1]).T.astype(o_ref.dtype)


def _attn_tile(s):
    return _pick(s, (512, 256, 128))


def _attention(p, vt, q_col, k_col, bias=None):
    s = p.shape[0]
    tq = _attn_tile(s)
    tk = 128
    scale = HEAD_DIM ** -0.5
    rows = vt.shape[2]
    q_spec = pl.BlockSpec((tq, HEAD_DIM), lambda h, i: (i, q_col + h))
    k_spec = pl.BlockSpec((s, HEAD_DIM), lambda h, i: (0, k_col + h))
    vt_spec = pl.BlockSpec((None, s // tq, rows, tq), lambda h, i: (h, 0, 0, 0))
    scratch = [pltpu.VMEM((rows, tq), F32)]
    if bias is None:
        kern = functools.partial(_sb_kernel, tq=tq, tk=tk, scale=scale)
        name = "sb_attn"
        in_specs = [q_spec, k_spec, vt_spec]
        args = [p, p, vt]
    else:
        scratch.append(pltpu.VMEM((2, tq, tq), F32))
        kern = functools.partial(_fox_kernel, tq=tq, tk=tk, scale=scale)
        name = "fox_attn"
        in_specs = [
            q_spec,
            pl.BlockSpec((None, tq, HEAD_DIM), lambda h, i: (h, i, 0)),
            k_spec,
            pl.BlockSpec((None, s, HEAD_DIM), lambda h, i: (h, 0, 0)),
            vt_spec,
        ]
        args = [p, bias[0], p, bias[1], vt]
    return pl.pallas_call(
        kern,
        grid=(N_HEADS, s // tq),
        in_specs=in_specs,
        out_specs=pl.BlockSpec((tq, HEAD_DIM), lambda h, i: (i, h)),
        out_shape=jax.ShapeDtypeStruct((s, D_HEADS), BF16),
        scratch_shapes=scratch,
        compiler_params=_params(("parallel", "arbitrary")),
        name=name,
    )(*args)


def _heads_t(v, ones_row=False):
    s = v.shape[0]
    tq = _attn_tile(s)
    vt = v.reshape(s // tq, tq, N_HEADS, HEAD_DIM).transpose(2, 0, 3, 1)
    if ones_row:
        extra = jnp.zeros((N_HEADS, s // tq, BF16_ROWS, tq), v.dtype).at[:, :, 0, :].set(1)
        vt = jnp.concatenate([vt, extra], axis=2)
    return vt


def _split3(x):
    a = x.astype(BF16)
    r = x - a.astype(F32)
    b = r.astype(BF16)
    c = (r - b.astype(F32)).astype(BF16)
    return [a, b, c]


def _fox_bias(cum_f):
    parts = _split3(cum_f * (HEAD_DIM ** 0.5))
    ones = jnp.ones_like(parts[0])
    pad = jnp.zeros(cum_f.shape + (HEAD_DIM - 6,), BF16)
    qa = jnp.concatenate([jnp.stack([ones] * 3 + parts, axis=-1), pad], axis=-1)
    ka = jnp.concatenate([jnp.stack([-t for t in parts] + [ones] * 3, axis=-1), pad], axis=-1)
    return qa, ka


def _bf16_pair_pack(h):
    bits = lax.bitcast_convert_type(h, U32)
    rnd = bits + jnp.uint32(0x7FFF) + ((bits >> 16) & jnp.uint32(1))
    half = h.shape[1] // 2
    return (rnd[:, half:] & jnp.uint32(0xFFFF0000)) | (rnd[:, :half] >> 16)


def _bf16_pair_unpack(w):
    lo = lax.bitcast_convert_type(w << 16, F32)
    hi = lax.bitcast_convert_type(w & jnp.uint32(0xFFFF0000), F32)
    return jnp.concatenate([lo, hi], axis=1).astype(BF16)


def _mix_kernel(osb_ref, ofx_ref, ga_ref, gb_ref, bg_ref, wpa_ref, wpb_ref, wo_ref, x_ref, g_ref,
                wrh_ref, wrl_ref, br_ref,
                x2_ref, hp_ref, ri_ref, rw_ref, cnt_ref, cnt_scr, *, tm, d, cap):
    t = pl.program_id(0)

    @pl.when(t == 0)
    def _():
        cnt_scr[...] = jnp.zeros_like(cnt_scr)

    bg = bg_ref[...]
    pa = jnp.dot(osb_ref[...], wpa_ref[...], preferred_element_type=F32)
    pb = jnp.dot(ofx_ref[...], wpb_ref[...], preferred_element_type=F32)
    ga = jax.nn.sigmoid(ga_ref[...].astype(F32) + bg[:, :d])
    gb = jax.nn.sigmoid(gb_ref[...].astype(F32) + bg[:, d:])
    mixed = (ga * pa + gb * pb).astype(BF16)
    x2 = x_ref[...] + jnp.dot(mixed, wo_ref[...], preferred_element_type=F32)
    x2_ref[...] = x2
    h2 = _rms(x2, g_ref[...])
    hp_ref[...] = _bf16_pair_pack(h2)

    hh = h2.astype(BF16)
    hl = (h2 - hh.astype(F32)).astype(BF16)
    logits = (jnp.dot(hh, wrh_ref[...], preferred_element_type=F32)
              + jnp.dot(hh, wrl_ref[...], preferred_element_type=F32)
              + jnp.dot(hl, wrh_ref[...], preferred_element_type=F32)
              + br_ref[...])

    lane = lax.broadcasted_iota(I32, (tm, LANES), 1).astype(F32)
    work = logits
    sel = jnp.zeros((tm, LANES), F32)
    vals, idxs = [], []
    for _ in range(TOP_K):
        mx = jnp.max(work, axis=1, keepdims=True)
        idx = jnp.min(jnp.where(work == mx, lane, float(LANES)), axis=1, keepdims=True)
        hit = lane == idx
        vals.append(mx)
        idxs.append(idx)
        work = jnp.where(hit, -jnp.inf, work)
        sel = jnp.where(hit, 1.0, sel)
    exps = [jnp.exp(v - vals[0]) for v in vals]
    denom = exps[0] + exps[1] + exps[2] + exps[3]

    row = lax.broadcasted_iota(I32, (tm, tm), 0)
    col = lax.broadcasted_iota(I32, (tm, tm), 1)
    tri = (col < row).astype(BF16)
    rank = jnp.dot(tri, sel.astype(BF16), preferred_element_type=F32) + cnt_scr[...]
    cnt_new = cnt_scr[...] + jnp.sum(sel, axis=0, keepdims=True)
    cnt_scr[...] = cnt_new
    cnt_ref[...] = cnt_new
    slot = rank + lane * float(cap)
    ri = jnp.zeros((tm, LANES), F32)
    rw = jnp.zeros((tm, LANES), F32)
    for r in range(TOP_K):
        slot_r = jnp.sum(jnp.where(lane == idxs[r], slot, 0.0), axis=1, keepdims=True)
        ri = jnp.where(lane == float(r), slot_r, ri)
        rw = jnp.where(lane == float(r), exps[r] / denom, rw)
    ri_ref[...] = ri.astype(I32)
    rw_ref[...] = rw


def _mix_route(o_sb, o_fx, p, gate_col, b_gate, w_pa, w_pb, w_out, x, g_ffn, wr_hi, wr_lo, br, cap):
    s, d = x.shape
    tm = _pick(s, (256, 128))
    const = lambda shape: pl.BlockSpec(shape, lambda t: (0, 0), pipeline_mode=pl.Buffered(1))
    kern = functools.partial(_mix_kernel, tm=tm, d=d, cap=cap)
    return pl.pallas_call(
        kern,
        grid=(s // tm,),
        in_specs=[
            pl.BlockSpec((tm, D_HEADS), lambda t: (t, 0)),
            pl.BlockSpec((tm, D_HEADS), lambda t: (t, 0)),
            pl.BlockSpec((tm, d), lambda t: (t, gate_col)),
            pl.BlockSpec((tm, d), lambda t: (t, gate_col + 1)),
            const((1, 2 * d)),
            const((D_HEADS, d)),
            const((D_HEADS, d)),
            const((d, d)),
            pl.BlockSpec((tm, d), lambda t: (t, 0)),
            const((1, d)),
            const((d, LANES)),
            const((d, LANES)),
            const((1, LANES)),
        ],
        out_specs=[
            pl.BlockSpec((tm, d), lambda t: (t, 0)),
            pl.BlockSpec((tm, d // 2), lambda t: (t, 0)),
            pl.BlockSpec((tm, LANES), lambda t: (t, 0)),
            pl.BlockSpec((tm, LANES), lambda t: (t, 0)),
            pl.BlockSpec((1, LANES), lambda t: (0, 0)),
        ],
        out_shape=[
            jax.ShapeDtypeStruct((s, d), F32),
            jax.ShapeDtypeStruct((s, d // 2), U32),
            jax.ShapeDtypeStruct((s, LANES), I32),
            jax.ShapeDtypeStruct((s, LANES), F32),
            jax.ShapeDtypeStruct((1, LANES), F32),
        ],
        scratch_shapes=[pltpu.VMEM((1, LANES), F32)],
        compiler_params=_params(("arbitrary",)),
        name="mix_route",
    )(o_sb, o_fx, p, p, b_gate, w_pa, w_pb, w_out, x, g_ffn, wr_hi, wr_lo, br)


def _dispatch_kernel(slot_ref, cnt_ref, hp_ref, xs_ref, zero_scr, sem, zsem, *, tm, ch, cap):
    t = pl.program_id(0)
    base = t * tm

    def row_copy(r, k):
        return pltpu.make_async_copy(
            hp_ref.at[pl.ds(r, 1), :],
            xs_ref.at[pl.ds(slot_ref[(base + r) * TOP_K + k], 1), :],
            sem)

    def start(r, c):
        for k in range(TOP_K):
            row_copy(r, k).start()
        return c

    def wait(r, c):
        for k in range(TOP_K):
            row_copy(r, k).wait()
        return c

    @pl.when(t == 0)
    def _():
        zero_scr[...] = jnp.zeros_like(zero_scr)

        def pad_copy(e):
            first = pl.multiple_of(e * cap + (cnt_ref[e] // SUBLANES) * SUBLANES, SUBLANES)
            return pltpu.make_async_copy(zero_scr, xs_ref.at[pl.ds(first, ch), :], zsem)

        def zstart(e, c):
            pad_copy(e).start()
            return c

        def zwait(e, c):
            pad_copy(e).wait()
            return c

        lax.fori_loop(0, N_EXPERTS, zstart, 0)
        lax.fori_loop(0, N_EXPERTS, zwait, 0)

    lax.fori_loop(0, tm, start, 0, unroll=ROW_DMA_UNROLL)
    lax.fori_loop(0, tm, wait, 0, unroll=ROW_DMA_UNROLL)


def _dispatch(slots, counts, hp, cap, ch):
    s, half = hp.shape
    tm = _pick(s, (512, 256, 128))
    kern = functools.partial(_dispatch_kernel, tm=tm, ch=ch, cap=cap)
    return pl.pallas_call(
        kern,
        grid_spec=pltpu.PrefetchScalarGridSpec(
            num_scalar_prefetch=2,
            grid=(s // tm,),
            in_specs=[pl.BlockSpec((tm, half), lambda t, sl, cn: (t, 0))],
            out_specs=pl.BlockSpec(memory_space=pl.ANY),
            scratch_shapes=[
                pltpu.VMEM((ch, half), U32),
                pltpu.SemaphoreType.DMA(()),
                pltpu.SemaphoreType.DMA(()),
            ],
        ),
        out_shape=jax.ShapeDtypeStruct((N_EXPERTS * cap, half), U32),
        compiler_params=_params(("arbitrary",)),
        name="dispatch",
    )(slots, counts, hp)


def _expert_kernel(sbe_ref, sbblk_ref, sbnch_ref, x_ref, w1_ref, b1_ref, w2_ref, b2_ref, y_ref,
                   xb_scr, w1b_scr, w2b_scr, *, ch, ff):
    s = pl.program_id(0)
    j = pl.program_id(1)
    nch = sbnch_ref[s]
    half = LANES // 2
    even = (lax.broadcasted_iota(I32, (ch, LANES), 1) % 2) == 0

    def half_act(g):
        lin = pltpu.roll(g, LANES - 1, 1)
        gate = jnp.minimum(g, SWIGLU_LIMIT)
        lin = jnp.clip(lin, -SWIGLU_LIMIT, SWIGLU_LIMIT)
        return gate * jax.nn.sigmoid(SWIGLU_ALPHA * gate) * (lin + 1.0)

    @pl.when(nch > 0)
    def _():
        @pl.when(j == 0)
        def _():
            xb_scr[...] = _bf16_pair_unpack(x_ref[...])

            def init(c, carry):
                y_ref[pl.ds(pl.multiple_of(c * ch, ch), ch), :] = jnp.broadcast_to(
                    b2_ref[...], (ch, y_ref.shape[1]))
                return carry

            lax.fori_loop(0, nch, init, 0)

        w1b_scr[...] = w1_ref[...].astype(BF16)
        prow = lax.broadcasted_iota(I32, (LANES, LANES), 0)
        pcol = lax.broadcasted_iota(I32, (LANES, LANES), 1)
        perm = (pcol == prow // 2 + half * (prow % 2)).astype(BF16)
        for u in range(ff // LANES):
            grp = w2_ref[pl.ds(u * LANES, LANES), :].astype(BF16)
            w2b_scr[pl.ds(u * LANES, LANES), :] = jnp.dot(
                perm, grp, preferred_element_type=F32).astype(BF16)

        def up(c):
            xk = xb_scr[pl.ds(pl.multiple_of(c * ch, ch), ch), :]
            gu = jnp.dot(xk, w1b_scr[...], preferred_element_type=F32) + b1_ref[...]
            pieces = []
            for u in range(ff // LANES):
                a = half_act(gu[:, (2 * u) * LANES:(2 * u + 1) * LANES])
                b = half_act(gu[:, (2 * u + 1) * LANES:(2 * u + 2) * LANES])
                pieces.append(jnp.where(even, a, pltpu.roll(b, 1, 1)))
            return jnp.concatenate(pieces, axis=1).astype(BF16)

        def down(c, act):
            r0 = pl.multiple_of(c * ch, ch)
            y_ref[pl.ds(r0, ch), :] += jnp.dot(act, w2b_scr[...], preferred_element_type=F32)

        def trip(c, act):
            nxt = up(c)
            down(c - 1, act)
            return nxt

        down(nch - 1, lax.fori_loop(1, nch, trip, up(0)))


def _experts(sb_e, sb_blk, sb_nch, xs, w1, b1, w2, b2, r_rows, ch, cap):
    ne, d, ff2 = w1.shape
    dff = ff2 // 2
    ff = _pick(dff, (256, 128))
    nj = dff // ff
    ns = sb_e.shape[0]
    half = d // 2
    blocks_per_expert = cap // r_rows

    def jmap(s, j, nch):
        return jnp.where(nch[s] > 0, j, nj - 1)

    kern = functools.partial(_expert_kernel, ch=ch, ff=ff)
    return pl.pallas_call(
        kern,
        grid_spec=pltpu.PrefetchScalarGridSpec(
            num_scalar_prefetch=3,
            grid=(ns, nj),
            in_specs=[
                pl.BlockSpec((r_rows, half),
                             lambda s, j, e, b, n: (e[s] * blocks_per_expert + b[s], 0)),
                pl.BlockSpec((None, d, 2 * ff), lambda s, j, e, b, n: (e[s], 0, jmap(s, j, n))),
                pl.BlockSpec((None, 1, 2 * ff), lambda s, j, e, b, n: (e[s], 0, jmap(s, j, n))),
                pl.BlockSpec((None, ff, d), lambda s, j, e, b, n: (e[s], jmap(s, j, n), 0)),
                pl.BlockSpec((None, 1, d), lambda s, j, e, b, n: (e[s], 0, 0)),
            ],
            out_specs=pl.BlockSpec((r_rows, d),
                                   lambda s, j, e, b, n: (e[s] * blocks_per_expert + b[s], 0)),
            scratch_shapes=[
                pltpu.VMEM((r_rows, d), BF16),
                pltpu.VMEM((d, 2 * ff), BF16),
                pltpu.VMEM((ff, d), BF16),
            ],
        ),
        out_shape=jax.ShapeDtypeStruct((ne * cap, d), F32),
        compiler_params=_params(("arbitrary", "arbitrary")),
        name="experts",
    )(sb_e, sb_blk, sb_nch, xs, w1, b1.reshape(ne, 1, ff2), w2, b2.reshape(ne, 1, d))


def _combine_kernel(slot_ref, x2_ref, rw_ref, g_ref, ys_ref, o_ref, gbuf, sem, *, tm, final_norm):
    t = pl.program_id(0)
    base = t * tm

    def row_copy(r, k):
        return pltpu.make_async_copy(
            ys_ref.at[pl.ds(slot_ref[(base + r) * TOP_K + k], 1), :],
            gbuf.at[pl.ds(k * tm + r, 1), :],
            sem)

    def start(r, c):
        for k in range(TOP_K):
            row_copy(r, k).start()
        return c

    def wait(r, c):
        for k in range(TOP_K):
            row_copy(r, k).wait()
        return c

    lax.fori_loop(0, tm, start, 0, unroll=ROW_DMA_UNROLL)
    lax.fori_loop(0, tm, wait, 0, unroll=ROW_DMA_UNROLL)

    rw = rw_ref[...]
    out = x2_ref[...]
    for k in range(TOP_K):
        out = out + rw[:, k:k + 1] * gbuf[pl.ds(k * tm, tm), :]
    if final_norm:
        out = _rms(out, g_ref[...])
    o_ref[...] = out


def _combine(slots, x2, rw, g, ys, final_norm):
    s, d = x2.shape
    tm = _pick(s, (256, 128))
    kern = functools.partial(_combine_kernel, tm=tm, final_norm=final_norm)
    return pl.pallas_call(
        kern,
        grid_spec=pltpu.PrefetchScalarGridSpec(
            num_scalar_prefetch=1,
            grid=(s // tm,),
            in_specs=[
                pl.BlockSpec((tm, d), lambda t, sl: (t, 0)),
                pl.BlockSpec((tm, LANES), lambda t, sl: (t, 0)),
                pl.BlockSpec((1, d), lambda t, sl: (0, 0)),
                pl.BlockSpec(memory_space=pl.ANY),
            ],
            out_specs=pl.BlockSpec((tm, d), lambda t, sl: (t, 0)),
            scratch_shapes=[pltpu.VMEM((TOP_K * tm, d), F32), pltpu.SemaphoreType.DMA(())],
        ),
        out_shape=jax.ShapeDtypeStruct((s, d), F32),
        compiler_params=_params(("arbitrary",)),
        name="combine",
    )(slots, x2, rw, g, ys)


def _superblocks(counts, r_rows, ch, ns):
    nsb = (counts + r_rows - 1) // r_rows
    ends = jnp.cumsum(nsb)
    total = ends[-1]
    sidx = jnp.arange(ns, dtype=I32)
    last = jnp.maximum(total - 1, 0)
    sclip = jnp.minimum(sidx, last)
    e = jnp.minimum(jnp.searchsorted(ends, sclip, side="right"), N_EXPERTS - 1).astype(I32)
    blk = sclip - (ends[e] - nsb[e])
    rows = jnp.clip(counts[e] - blk * r_rows, 0, r_rows)
    nch = jnp.where(sidx < total, (rows + ch - 1) // ch, 0)
    return e, blk.astype(I32), nch.astype(I32)


def _layer(x, g_mix, w_in, b_forget, b_gate, w_pa, w_pb, w_out, g_ffn,
           w_router, b_router, w1, b1, w2, b2, g_out, final_norm):
    s, d = x.shape
    qkv = 6 * D_HEADS

    w_main = jnp.concatenate([w_in[:, :qkv], w_in[:, qkv + N_HEADS:]], axis=1).astype(BF16)
    w_f = jnp.pad(w_in[:, qkv:qkv + N_HEADS], ((0, 0), (0, LANES - N_HEADS))).astype(BF16)
    p, f_log = _in_proj(x, g_mix.reshape(1, d), w_main, w_f)

    log_f = jax.nn.log_sigmoid(f_log[:, :N_HEADS] + b_forget)
    cum_f = jnp.cumsum(log_f, axis=0).T

    nb = D_HEADS // HEAD_DIM
    o_sb = _attention(p, _heads_t(p[:, 2 * D_HEADS:3 * D_HEADS]), 0, nb)
    o_fx = _attention(p, _heads_t(p[:, 5 * D_HEADS:6 * D_HEADS], ones_row=True), 3 * nb, 4 * nb,
                      bias=_fox_bias(cum_f))

    r_rows = 1024
    ch = 256
    cap = -(-(s + ch) // r_rows) * r_rows
    ns =(s * TOP_K) // r_rows + N_EXPERTS
    wr = jnp.pad(w_router, ((0, 0), (0, LANES - N_EXPERTS)))
    wr_hi = wr.astype(BF16)
    wr_lo = (wr - wr_hi.astype(F32)).astype(BF16)
    br = jnp.pad(b_router, (0, LANES - N_EXPERTS), constant_values=NEG_BIG).reshape(1, LANES)
    x2, hp, ri, rw, cnt = _mix_route(
        o_sb, o_fx, p, qkv // d, b_gate.reshape(1, 2 * d), w_pa.astype(BF16), w_pb.astype(BF16),
        w_out.astype(BF16), x, g_ffn.reshape(1, d), wr_hi, wr_lo, br, cap)

    slots = ri[:, :TOP_K].reshape(-1)
    counts = cnt[0, :N_EXPERTS].astype(I32)
    xs = _dispatch(slots, counts, hp, cap, ch)
    sb_e, sb_blk, sb_nch = _superblocks(counts, r_rows, ch, ns)
    ys = _experts(sb_e, sb_blk, sb_nch, xs, w1, b1, w2, b2, r_rows, ch, cap)
    return _combine(slots, x2, rw, g_out.reshape(1, d), ys, final_norm)


def kernel(x, g_mix, w_in, b_forget, b_gate, w_proj_sb, w_proj_fox, w_out, g_ffn, w_router,
           b_router, w_mlp1, b_mlp1, w_mlp2, b_mlp2, g_final):
    b, s, d = x.shape
    depth = g_mix.shape[0]
    outs = []
    for bi in range(b):
        xb = x[bi]
        for layer in range(depth):
            last = layer == depth - 1
            xb = _layer(xb, g_mix[layer], w_in[layer], b_forget[layer], b_gate[layer],
                        w_proj_sb[layer], w_proj_fox[layer], w_out[layer], g_ffn[layer],
                        w_router[layer], b_router[layer], w_mlp1[layer], b_mlp1[layer],
                        w_mlp2[layer], b_mlp2[layer], g_final, last)
        outs.append(xb.reshape(1, s, d))
    return outs[0] if b == 1 else jnp.concatenate(outs, axis=0)
```

```python
import functools

import jax
import jax.numpy as jnp
from jax import lax
from jax.experimental import pallas as pl
from jax.experimental.pallas import tpu as pltpu

F32 = jnp.float32
BF16 = jnp.bfloat16
U32 = jnp.uint32
I32 = jnp.int32

LANES = 128
SUBLANES = 8
HEAD_DIM = 128
N_HEADS = 8
D_HEADS = N_HEADS * HEAD_DIM
N_EXPERTS = 32
TOP_K = 4
SWIGLU_LIMIT = 7.0
SWIGLU_ALPHA = 1.702
RMS_EPS = 1e-5
NEG_BIG = -1e30
ROW_DMA_UNROLL = 8
LOG2_E = 1.4426950408889634
BF16_ROWS = 16
F32_EXP_UNDERFLOW = 105.0
VMEM_LIMIT = 56 * 1024 * 1024


def _pick(n, candidates):
    for c in candidates:
        if n % c == 0:
            return c
    raise ValueError(f"no tile in {candidates} divides {n}")


def _params(sem, vmem=VMEM_LIMIT):
    return pltpu.CompilerParams(dimension_semantics=sem, vmem_limit_bytes=vmem)


def _rms(x, g):
    ms = jnp.mean(x * x, axis=-1, keepdims=True)
    return x * lax.rsqrt(ms + RMS_EPS) * g


def _inproj_kernel(x_ref, g_ref, w_ref, wf_ref, p_ref, f_ref, h_scr):
    @pl.when(pl.program_id(1) == 0)
    def _():
        hb = _rms(x_ref[...], g_ref[...]).astype(BF16)
        h_scr[...] = hb
        f_ref[...] = jnp.dot(hb, wf_ref[...], preferred_element_type=F32)

    p_ref[...] = jnp.dot(h_scr[...], w_ref[...], preferred_element_type=F32).astype(BF16)


def _in_proj(x, g, w_main, w_f):
    s, d = x.shape
    n = w_main.shape[1]
    tm = _pick(s, (1024, 512, 256, 128))
    tn = _pick(n, (1024, 512, 256, 128))
    return pl.pallas_call(
        _inproj_kernel,
        grid=(s // tm, n // tn),
        in_specs=[
            pl.BlockSpec((tm, d), lambda m, j: (m, 0)),
            pl.BlockSpec((1, d), lambda m, j: (0, 0)),
            pl.BlockSpec((d, tn), lambda m, j: (0, j)),
            pl.BlockSpec((d, LANES), lambda m, j: (0, 0)),
        ],
        out_specs=[
            pl.BlockSpec((tm, tn), lambda m, j: (m, j)),
            pl.BlockSpec((tm, LANES), lambda m, j: (m, 0)),
        ],
        out_shape=[
            jax.ShapeDtypeStruct((s, n), BF16),
            jax.ShapeDtypeStruct((s, LANES), F32),
        ],
        scratch_shapes=[pltpu.VMEM((tm, d), BF16)],
        compiler_params=_params(("parallel", "arbitrary")),
        name="in_proj",
    )(x, g, w_main, w_f)


_NT = (((1,), (1,)), ((), ()))


def _sb_kernel(q_ref, k_ref, vt_ref, o_ref, acc_scr, *, tq, tk, scale):
    i = pl.program_id(1)
    nsub = tq // tk
    q = q_ref[...]
    row = lax.broadcasted_iota(I32, (tk, 2 * tk), 0)
    col = lax.broadcasted_iota(I32, (tk, 2 * tk), 1)
    ut2 = ((col % tk) >= row).astype(BF16)
    diff = lax.broadcasted_iota(I32, (tk, tq), 1) - lax.broadcasted_iota(I32, (tk, tq), 0)

    def sub_block(kb, csum, mask):
        k = k_ref[pl.ds(pl.multiple_of(kb * tk, tk), tk), :]
        z = lax.dot_general(k, q, _NT, preferred_element_type=F32) * scale
        lk = -(jnp.maximum(z, 0.0) + jnp.log(1.0 + jnp.exp(-jnp.abs(z))))
        if mask is not None:
            lk = jnp.where(mask, lk, 0.0)
        hi = lk.astype(BF16)
        lo = (lk - hi.astype(F32)).astype(BF16)
        cs = jnp.dot(ut2, jnp.concatenate([hi, lo], axis=0), preferred_element_type=F32)
        a = jnp.exp(z + cs + csum)
        if mask is not None:
            a = jnp.where(mask, a, 0.0)
        return a.astype(BF16), csum + cs[0:1, :]

    def tile(t, csum, masked):
        parts = [None] * nsub
        for u in reversed(range(nsub)):
            parts[u], csum = sub_block(t * nsub + u, csum, (diff > u * tk) if masked else None)
        acc_scr[...] += jnp.dot(vt_ref[t], jnp.concatenate(parts, axis=0),
                                preferred_element_type=F32)
        return csum

    acc_scr[...] = jnp.zeros_like(acc_scr)
    csum = tile(i, jnp.zeros((1, tq), F32), True)

    def live(state):
        jj, c = state
        return jnp.logical_and(jj < i, jnp.max(c) > -F32_EXP_UNDERFLOW)

    lax.while_loop(live, lambda st: (st[0] + 1, tile(i - 1 - st[0], st[1], False)),
                   (jnp.int32(0), csum))
    o_ref[...] = acc_scr[...].T.astype(o_ref.dtype)


def _fox_kernel(q_ref, qa_ref, k_ref, ka_ref, vt_ref, o_ref, acc_scr, s_scr, *, tq, tk, scale):
    i = pl.program_id(1)
    nsub = tq // tk
    q = jnp.concatenate([q_ref[...], qa_ref[...]], axis=1)
    diff = lax.broadcasted_iota(I32, (tk, tq), 1) - lax.broadcasted_iota(I32, (tk, tq), 0)
    to_log2 = scale * LOG2_E

    def scores(t, slot, masked):
        for u in range(nsub):
            ks = pl.multiple_of((t * nsub + u) * tk, tk)
            k = jnp.concatenate([k_ref[pl.ds(ks, tk), :], ka_ref[pl.ds(ks, tk), :]], axis=1)
            s = lax.dot_general(k, q, _NT, preferred_element_type=F32) * to_log2
            if masked:
                s = jnp.where(diff >= u * tk, s, NEG_BIG)
            s_scr[slot, pl.ds(u * tk, tk), :] = s

    def consume(t, slot, m):
        top = s_scr[slot, pl.ds(0, tk), :]
        for u in range(1, nsub):
            top = jnp.maximum(top, s_scr[slot, pl.ds(u * tk, tk), :])
        m_new = jnp.maximum(m, jnp.max(top, axis=0, keepdims=True))
        alpha = jnp.exp2(m - m_new)
        pcat = jnp.exp2(s_scr[slot] - m_new).astype(BF16)
        acc_scr[...] = alpha * acc_scr[...] + jnp.dot(vt_ref[t], pcat, preferred_element_type=F32)
        return m_new

    def step(t, slot, m):
        scores(t - 1, 1 - slot, False)
        return consume(t, slot, m)

    def trip(k, m):
        t = i - 2 * k
        return step(t - 1, 1, step(t, 0, m))

    acc_scr[...] = jnp.zeros_like(acc_scr)
    scores(i, 0, True)
    m = lax.fori_loop(0, i // 2, trip, jnp.full((1, tq), NEG_BIG, F32))

    @pl.when(i % 2 == 0)
    def _():
        consume(0, 0, m)

    @pl.when(i % 2 == 1)
    def _():
        consume(0, 1, step(1, 0, m))

    acc = acc_scr[...]
    o_ref[...] = (acc[:HEAD_DIM] / acc[HEAD_DIM:HEAD_DIM + 1]).T.astype(o_ref.dtype)


def _attn_tile(s):
    return _pick(s, (512, 256, 128))


def _attention(p, vt, q_col, k_col, bias=None):
    s = p.shape[0]
    tq = _attn_tile(s)
    tk = 128
    scale = HEAD_DIM ** -0.5
    rows = vt.shape[2]
    q_spec = pl.BlockSpec((tq, HEAD_DIM), lambda h, i: (i, q_col + h))
    k_spec = pl.BlockSpec((s, HEAD_DIM), lambda h, i: (0, k_col + h))
    vt_spec = pl.BlockSpec((None, s // tq, rows, tq), lambda h, i: (h, 0, 0, 0))
    scratch = [pltpu.VMEM((rows, tq), F32)]
    if bias is None:
        kern = functools.partial(_sb_kernel, tq=tq, tk=tk, scale=scale)
        name = "sb_attn"
        in_specs = [q_spec, k_spec, vt_spec]
        args = [p, p, vt]
    else:
        scratch.append(pltpu.VMEM((2, tq, tq), F32))
        kern = functools.partial(_fox_kernel, tq=tq, tk=tk, scale=scale)
        name = "fox_attn"
        in_specs = [
            q_spec,
            pl.BlockSpec((None, tq, HEAD_DIM), lambda h, i: (h, i, 0)),
            k_spec,
            pl.BlockSpec((None, s, HEAD_DIM), lambda h, i: (h, 0, 0)),
            vt_spec,
        ]
        args = [p, bias[0], p, bias[1], vt]
    return pl.pallas_call(
        kern,
        grid=(N_HEADS, s // tq),
        in_specs=in_specs,
        out_specs=pl.BlockSpec((tq, HEAD_DIM), lambda h, i: (i, h)),
        out_shape=jax.ShapeDtypeStruct((s, D_HEADS), BF16),
        scratch_shapes=scratch,
        compiler_params=_params(("parallel", "arbitrary")),
        name=name,
    )(*args)


def _heads_t(v, ones_row=False):
    s = v.shape[0]
    tq = _attn_tile(s)
    vt = v.reshape(s // tq, tq, N_HEADS, HEAD_DIM).transpose(2, 0, 3, 1)
    if ones_row:
        extra = jnp.zeros((N_HEADS, s // tq, BF16_ROWS, tq), v.dtype).at[:, :, 0, :].set(1)
        vt = jnp.concatenate([vt, extra], axis=2)
    return vt


def _bf16_trunc(x):
    bits = lax.bitcast_convert_type(x, U32) & jnp.uint32(0xFFFF0000)
    return lax.bitcast_convert_type(bits, F32)


def _split3(x):
    a = _bf16_trunc(x)
    b = _bf16_trunc(x - a)
    c = _bf16_trunc(x - a - b)
    return [a.astype(BF16), b.astype(BF16), c.astype(BF16)]


def _fox_bias(cum_f):
    parts = [t[:, :, None] for t in _split3(cum_f * (HEAD_DIM ** 0.5))]
    col = lax.broadcasted_iota(I32, cum_f.shape + (HEAD_DIM,), 2)
    zero = jnp.zeros((), BF16)
    one = jnp.ones((), BF16)

    def place(first, vals):
        out = zero
        for n, v in enumerate(vals):
            out = jnp.where(col == first + n, v, out)
        return out

    qa = jnp.where(col < 3, one, place(3, parts))
    ka = jnp.where((col >= 3) & (col < 6), one, place(0, [-t for t in parts]))
    return qa, ka


def _bf16_pair_pack(h):
    bits = lax.bitcast_convert_type(h, U32)
    rnd = bits + jnp.uint32(0x7FFF) + ((bits >> 16) & jnp.uint32(1))
    half = h.shape[1] // 2
    return (rnd[:, half:] & jnp.uint32(0xFFFF0000)) | (rnd[:, :half] >> 16)


def _bf16_pair_unpack(w):
    lo = lax.bitcast_convert_type(w << 16, F32)
    hi = lax.bitcast_convert_type(w & jnp.uint32(0xFFFF0000), F32)
    return jnp.concatenate([lo, hi], axis=1).astype(BF16)


def _mix_kernel(osb_ref, ofx_ref, ga_ref, gb_ref, bg_ref, wpa_ref, wpb_ref, wo_ref, x_ref, g_ref,
                wrh_ref, wrl_ref, br_ref,
                x2_ref, hp_ref, ri_ref, rw_ref, cnt_ref, cnt_scr, *, tm, d, cap):
    t = pl.program_id(0)

    @pl.when(t == 0)
    def _():
        cnt_scr[...] = jnp.zeros_like(cnt_scr)

    bg = bg_ref[...]
    pa = jnp.dot(osb_ref[...], wpa_ref[...], preferred_element_type=F32)
    pb = jnp.dot(ofx_ref[...], wpb_ref[...], preferred_element_type=F32)
    ga = jax.nn.sigmoid(ga_ref[...].astype(F32) + bg[:, :d])
    gb = jax.nn.sigmoid(gb_ref[...].astype(F32) + bg[:, d:])
    mixed = (ga * pa + gb * pb).astype(BF16)
    x2 = x_ref[...] + jnp.dot(mixed, wo_ref[...], preferred_element_type=F32)
    x2_ref[...] = x2
    h2 = _rms(x2, g_ref[...])
    hp_ref[...] = _bf16_pair_pack(h2)

    hh = h2.astype(BF16)
    hl = (h2 - hh.astype(F32)).astype(BF16)
    logits = (jnp.dot(hh, wrh_ref[...], preferred_element_type=F32)
              + jnp.dot(hh, wrl_ref[...], preferred_element_type=F32)
              + jnp.dot(hl, wrh_ref[...], preferred_element_type=F32)
              + br_ref[...])

    lane = lax.broadcasted_iota(I32, (tm, LANES), 1).astype(F32)
    work = logits
    sel = jnp.zeros((tm, LANES), F32)
    vals, idxs = [], []
    for _ in range(TOP_K):
        mx = jnp.max(work, axis=1, keepdims=True)
        idx = jnp.min(jnp.where(work == mx, lane, float(LANES)), axis=1, keepdims=True)
        hit = lane == idx
        vals.append(mx)
        idxs.append(idx)
        work = jnp.where(hit, -jnp.inf, work)
        sel = jnp.where(hit, 1.0, sel)
    exps = [jnp.exp(v - vals[0]) for v in vals]
    denom = exps[0] + exps[1] + exps[2] + exps[3]

    row = lax.broadcasted_iota(I32, (tm, tm), 0)
    col = lax.broadcasted_iota(I32, (tm, tm), 1)
    tri = (col < row).astype(BF16)
    rank = jnp.dot(tri, sel.astype(BF16), preferred_element_type=F32) + cnt_scr[...]
    cnt_new = cnt_scr[...] + jnp.sum(sel, axis=0, keepdims=True)
    cnt_scr[...] = cnt_new
    cnt_ref[...] = cnt_new
    slot = rank + lane * float(cap)
    ri = jnp.zeros((tm, LANES), F32)
    rw = jnp.zeros((tm, LANES), F32)
    for r in range(TOP_K):
        slot_r = jnp.sum(jnp.where(lane == idxs[r], slot, 0.0), axis=1, keepdims=True)
        ri = jnp.where(lane == float(r), slot_r, ri)
        rw = jnp.where(lane == float(r), exps[r] / denom, rw)
    ri_ref[...] = ri.astype(I32)
    rw_ref[...] = rw


def _mix_route(o_sb, o_fx, p, gate_col, b_gate, w_pa, w_pb, w_out, x, g_ffn, wr_hi, wr_lo, br, cap):
    s, d = x.shape
    tm = _pick(s, (256, 128))
    const = lambda shape: pl.BlockSpec(shape, lambda t: (0, 0), pipeline_mode=pl.Buffered(1))
    kern = functools.partial(_mix_kernel, tm=tm, d=d, cap=cap)
    return pl.pallas_call(
        kern,
        grid=(s // tm,),
        in_specs=[
            pl.BlockSpec((tm, D_HEADS), lambda t: (t, 0)),
            pl.BlockSpec((tm, D_HEADS), lambda t: (t, 0)),
            pl.BlockSpec((tm, d), lambda t: (t, gate_col)),
            pl.BlockSpec((tm, d), lambda t: (t, gate_col + 1)),
            const((1, 2 * d)),
            const((D_HEADS, d)),
            const((D_HEADS, d)),
            const((d, d)),
            pl.BlockSpec((tm, d), lambda t: (t, 0)),
            const((1, d)),
            const((d, LANES)),
            const((d, LANES)),
            const((1, LANES)),
        ],
        out_specs=[
            pl.BlockSpec((tm, d), lambda t: (t, 0)),
            pl.BlockSpec((tm, d // 2), lambda t: (t, 0)),
            pl.BlockSpec((tm, LANES), lambda t: (t, 0)),
            pl.BlockSpec((tm, LANES), lambda t: (t, 0)),
            pl.BlockSpec((1, LANES), lambda t: (0, 0)),
        ],
        out_shape=[
            jax.ShapeDtypeStruct((s, d), F32),
            jax.ShapeDtypeStruct((s, d // 2), U32),
            jax.ShapeDtypeStruct((s, LANES), I32),
            jax.ShapeDtypeStruct((s, LANES), F32),
            jax.ShapeDtypeStruct((1, LANES), F32),
        ],
        scratch_shapes=[pltpu.VMEM((1, LANES), F32)],
        compiler_params=_params(("arbitrary",)),
        name="mix_route",
    )(o_sb, o_fx, p, p, b_gate, w_pa, w_pb, w_out, x, g_ffn, wr_hi, wr_lo, br)


def _dispatch_kernel(slot_ref, cnt_ref, hp_ref, xs_ref, zero_scr, sem, zsem, *, tm, ch, cap):
    t = pl.program_id(0)
    base = t * tm

    def row_copy(r, k):
        return pltpu.make_async_copy(
            hp_ref.at[pl.ds(r, 1), :],
            xs_ref.at[pl.ds(slot_ref[(base + r) * TOP_K + k], 1), :],
            sem)

    def start(r, c):
        for k in range(TOP_K):
            row_copy(r, k).start()
        return c

    def wait(r, c):
        for k in range(TOP_K):
            row_copy(r, k).wait()
        return c

    @pl.when(t == 0)
    def _():
        zero_scr[...] = jnp.zeros_like(zero_scr)

        def pad_copy(e):
            first = pl.multiple_of(e * cap + (cnt_ref[e] // SUBLANES) * SUBLANES, SUBLANES)
            return pltpu.make_async_copy(zero_scr, xs_ref.at[pl.ds(first, ch), :], zsem)

        def zstart(e, c):
            pad_copy(e).start()
            return c

        def zwait(e, c):
            pad_copy(e).wait()
            return c

        lax.fori_loop(0, N_EXPERTS, zstart, 0)
        lax.fori_loop(0, N_EXPERTS, zwait, 0)

    lax.fori_loop(0, tm, start, 0, unroll=ROW_DMA_UNROLL)
    lax.fori_loop(0, tm, wait, 0, unroll=ROW_DMA_UNROLL)


def _dispatch(slots, counts, hp, cap, ch):
    s, half = hp.shape
    tm = _pick(s, (512, 256, 128))
    kern = functools.partial(_dispatch_kernel, tm=tm, ch=ch, cap=cap)
    return pl.pallas_call(
        kern,
        grid_spec=pltpu.PrefetchScalarGridSpec(
            num_scalar_prefetch=2,
            grid=(s // tm,),
            in_specs=[pl.BlockSpec((tm, half), lambda t, sl, cn: (t, 0))],
            out_specs=pl.BlockSpec(memory_space=pl.ANY),
            scratch_shapes=[
                pltpu.VMEM((ch, half), U32),
                pltpu.SemaphoreType.DMA(()),
                pltpu.SemaphoreType.DMA(()),
            ],
        ),
        out_shape=jax.ShapeDtypeStruct((N_EXPERTS * cap, half), U32),
        compiler_params=_params(("arbitrary",)),
        name="dispatch",
    )(slots, counts, hp)


def _expert_kernel(sbe_ref, sbblk_ref, sbnch_ref, x_ref, w1a_ref, w1c_ref, b1_ref, w2a_ref, w2c_ref,
                   b2_ref, y_ref, xb_scr, w1b_scr, w2b_scr, *, ch, ff):
    s = pl.program_id(0)
    j = pl.program_id(1)
    nch = sbnch_ref[s]
    half = LANES // 2
    even = (lax.broadcasted_iota(I32, (ch, LANES), 1) % 2) == 0

    def half_act(g):
        lin = pltpu.roll(g, LANES - 1, 1)
        gate = jnp.minimum(g, SWIGLU_LIMIT)
        lin = jnp.clip(lin, -SWIGLU_LIMIT, SWIGLU_LIMIT)
        return gate * jax.nn.sigmoid(SWIGLU_ALPHA * gate) * (lin + 1.0)

    @pl.when(nch > 0)
    def _():
        @pl.when(j == 0)
        def _():
            xb_scr[...] = _bf16_pair_unpack(x_ref[...])

            def init(c, carry):
                y_ref[pl.ds(pl.multiple_of(c * ch, ch), ch), :] = jnp.broadcast_to(
                    b2_ref[...], (ch, y_ref.shape[1]))
                return carry

            lax.fori_loop(0, nch, init, 0)

        w1b_scr[:, :ff] = w1a_ref[...].astype(BF16)
        w1b_scr[:, ff:] = w1c_ref[...].astype(BF16)
        prow = lax.broadcasted_iota(I32, (LANES, LANES), 0)
        pcol = lax.broadcasted_iota(I32, (LANES, LANES), 1)
        perm = (pcol == prow // 2 + half * (prow % 2)).astype(BF16)
        groups = ff // LANES
        for u in range(groups):
            src_ref, v = (w2a_ref, u) if u < groups // 2 else (w2c_ref, u - groups // 2)
            grp = src_ref[pl.ds(v * LANES, LANES), :].astype(BF16)
            w2b_scr[pl.ds(u * LANES, LANES), :] = jnp.dot(
                perm, grp, preferred_element_type=F32).astype(BF16)

        def up(c):
            xk = xb_scr[pl.ds(pl.multiple_of(c * ch, ch), ch), :]
            gu = jnp.dot(xk, w1b_scr[...], preferred_element_type=F32) + b1_ref[...]
            pieces = []
            for u in range(ff // LANES):
                a = half_act(gu[:, (2 * u) * LANES:(2 * u + 1) * LANES])
                b = half_act(gu[:, (2 * u + 1) * LANES:(2 * u + 2) * LANES])
                pieces.append(jnp.where(even, a, pltpu.roll(b, 1, 1)))
            return jnp.concatenate(pieces, axis=1).astype(BF16)

        def down(c, act):
            r0 = pl.multiple_of(c * ch, ch)
            y_ref[pl.ds(r0, ch), :] += jnp.dot(act, w2b_scr[...], preferred_element_type=F32)

        def trip(c, act):
            nxt = up(c)
            down(c - 1, act)
            return nxt

        down(nch - 1, lax.fori_loop(1, nch, trip, up(0)))


def _experts(sb_e, sb_blk, sb_nch, xs, w1, b1, w2, b2, r_rows, ch, cap):
    ne, d, ff2 = w1.shape
    dff = ff2 // 2
    ff = _pick(dff, (256, 128))
    nj = dff // ff
    ns = sb_e.shape[0]
    half = d // 2
    blocks_per_expert = cap // r_rows

    def jmap(s, j, nch):
        return jnp.where(nch[s] > 0, j, nj - 1)

    kern = functools.partial(_expert_kernel, ch=ch, ff=ff)
    return pl.pallas_call(
        kern,
        grid_spec=pltpu.PrefetchScalarGridSpec(
            num_scalar_prefetch=3,
            grid=(ns, nj),
            in_specs=[
                pl.BlockSpec((r_rows, half),
                             lambda s, j, e, b, n: (e[s] * blocks_per_expert + b[s], 0)),
                pl.BlockSpec((None, d, ff), lambda s, j, e, b, n: (e[s], 0, 2 * jmap(s, j, n))),
                pl.BlockSpec((None, d, ff), lambda s, j, e, b, n: (e[s], 0, 2 * jmap(s, j, n) + 1)),
                pl.BlockSpec((None, 1, 2 * ff), lambda s, j, e, b, n: (e[s], 0, jmap(s, j, n))),
                pl.BlockSpec((None, ff // 2, d), lambda s, j, e, b, n: (e[s], 2 * jmap(s, j, n), 0)),
                pl.BlockSpec((None, ff // 2, d),
                             lambda s, j, e, b, n: (e[s], 2 * jmap(s, j, n) + 1, 0)),
                pl.BlockSpec((None, 1, d), lambda s, j, e, b, n: (e[s], 0, 0)),
            ],
            out_specs=pl.BlockSpec((r_rows, d),
                                   lambda s, j, e, b, n: (e[s] * blocks_per_expert + b[s], 0)),
            scratch_shapes=[
                pltpu.VMEM((r_rows, d), BF16),
                pltpu.VMEM((d, 2 * ff), BF16),
                pltpu.VMEM((ff, d), BF16),
            ],
        ),
        out_shape=jax.ShapeDtypeStruct((ne * cap, d), F32),
        compiler_params=_params(("arbitrary", "arbitrary")),
        name="experts",
    )(sb_e, sb_blk, sb_nch, xs, w1, w1, b1.reshape(ne, 1, ff2), w2, w2, b2.reshape(ne, 1, d))


def _combine_kernel(slot_ref, x2_ref, rw_ref, g_ref, ys_ref, o_ref, gbuf, sem, *, tm, final_norm):
    t = pl.program_id(0)
    base = t * tm

    def row_copy(r, k):
        return pltpu.make_async_copy(
            ys_ref.at[pl.ds(slot_ref[(base + r) * TOP_K + k], 1), :],
            gbuf.at[pl.ds(k * tm + r, 1), :],
            sem)

    def start(r, c):
        for k in range(TOP_K):
            row_copy(r, k).start()
        return c

    def wait(r, c):
        for k in range(TOP_K):
            row_copy(r, k).wait()
        return c

    lax.fori_loop(0, tm, start, 0, unroll=ROW_DMA_UNROLL)
    lax.fori_loop(0, tm, wait, 0, unroll=ROW_DMA_UNROLL)

    rw = rw_ref[...]
    out = x2_ref[...]
    for k in range(TOP_K):
        out = out + rw[:, k:k + 1] * gbuf[pl.ds(k * tm, tm), :]
    if final_norm:
        out = _rms(out, g_ref[...])
    o_ref[...] = out


def _combine(slots, x2, rw, g, ys, final_norm):
    s, d = x2.shape
    tm = _pick(s, (256, 128))
    kern = functools.partial(_combine_kernel, tm=tm, final_norm=final_norm)
    return pl.pallas_call(
        kern,
        grid_spec=pltpu.PrefetchScalarGridSpec(
            num_scalar_prefetch=1,
            grid=(s // tm,),
            in_specs=[
                pl.BlockSpec((tm, d), lambda t, sl: (t, 0)),
                pl.BlockSpec((tm, LANES), lambda t, sl: (t, 0)),
                pl.BlockSpec((1, d), lambda t, sl: (0, 0)),
                pl.BlockSpec(memory_space=pl.ANY),
            ],
            out_specs=pl.BlockSpec((tm, d), lambda t, sl: (t, 0)),
            scratch_shapes=[pltpu.VMEM((TOP_K * tm, d), F32), pltpu.SemaphoreType.DMA(())],
        ),
        out_shape=jax.ShapeDtypeStruct((s, d), F32),
        compiler_params=_params(("arbitrary",)),
        name="combine",
    )(slots, x2, rw, g, ys)


def _superblocks(counts, r_rows, ch, ns):
    nsb = (counts + r_rows - 1) // r_rows
    ends = jnp.cumsum(nsb)
    total = ends[-1]
    sidx = jnp.arange(ns, dtype=I32)
    last = jnp.maximum(total - 1, 0)
    sclip = jnp.minimum(sidx, last)
    e = jnp.minimum(jnp.searchsorted(ends, sclip, side="right"), N_EXPERTS - 1).astype(I32)
    blk = sclip - (ends[e] - nsb[e])
    rows = jnp.clip(counts[e] - blk * r_rows, 0, r_rows)
    nch = jnp.where(sidx < total, (rows + ch - 1) // ch, 0)
    return e, blk.astype(I32), nch.astype(I32)


def _layer(x, g_mix, w_in, b_forget, b_gate, w_pa, w_pb, w_out, g_ffn,
           w_router, b_router, w1, b1, w2, b2, g_out, final_norm):
    s, d = x.shape
    qkv = 6 * D_HEADS

    w_main = jnp.concatenate([w_in[:, :qkv], w_in[:, qkv + N_HEADS:]], axis=1).astype(BF16)
    w_f = jnp.pad(w_in[:, qkv:qkv + N_HEADS], ((0, 0), (0, LANES - N_HEADS))).astype(BF16)
    p, f_log = _in_proj(x, g_mix.reshape(1, d), w_main, w_f)

    log_f = jax.nn.log_sigmoid(f_log[:, :N_HEADS] + b_forget)
    cum_f = jnp.cumsum(log_f, axis=0).T

    nb = D_HEADS // HEAD_DIM
    o_sb = _attention(p, _heads_t(p[:, 2 * D_HEADS:3 * D_HEADS]), 0, nb)
    o_fx = _attention(p, _heads_t(p[:, 5 * D_HEADS:6 * D_HEADS], ones_row=True), 3 * nb, 4 * nb,
                      bias=_fox_bias(cum_f))

    r_rows = 1024
    ch = 256
    cap = -(-(s + ch) // r_rows) * r_rows
    ns =(s * TOP_K) // r_rows + N_EXPERTS
    wr = jnp.pad(w_router, ((0, 0), (0, LANES - N_EXPERTS)))
    wr_top = _bf16_trunc(wr)
    wr_hi = wr_top.astype(BF16)
    wr_lo = (wr - wr_top).astype(BF16)
    br = jnp.pad(b_router, (0, LANES - N_EXPERTS), constant_values=NEG_BIG).reshape(1, LANES)
    x2, hp, ri, rw, cnt = _mix_route(
        o_sb, o_fx, p, qkv // d, b_gate.reshape(1, 2 * d), w_pa.astype(BF16), w_pb.astype(BF16),
        w_out.astype(BF16), x, g_ffn.reshape(1, d), wr_hi, wr_lo, br, cap)

    slots = ri[:, :TOP_K].reshape(-1)
    counts = cnt[0, :N_EXPERTS].astype(I32)
    xs = _dispatch(slots, counts, hp, cap, ch)
    sb_e, sb_blk, sb_nch = _superblocks(counts, r_rows, ch, ns)
    ys = _experts(sb_e, sb_blk, sb_nch, xs, w1, b1, w2, b2, r_rows, ch, cap)
    return _combine(slots, x2, rw, g_out.reshape(1, d), ys, final_norm)


def kernel(x, g_mix, w_in, b_forget, b_gate, w_proj_sb, w_proj_fox, w_out, g_ffn, w_router,
           b_router, w_mlp1, b_mlp1, w_mlp2, b_mlp2, g_final):
    b, s, d = x.shape
    depth = g_mix.shape[0]
    outs = []
    for bi in range(b):
        xb = x[bi]
        for layer in range(depth):
            last = layer == depth - 1
            xb = _layer(xb, g_mix[layer], w_in[layer], b_forget[layer], b_gate[layer],
                        w_proj_sb[layer], w_proj_fox[layer], w_out[layer], g_ffn[layer],
                        w_router[layer], b_router[layer], w_mlp1[layer], b_mlp1[layer],
                        w_mlp2[layer], b_mlp2[layer], g_final, last)
        outs.append(xb.reshape(1, s, d))
    return outs[0] if b == 1 else jnp.concatenate(outs, axis=0)
```

```python
import functools

import jax
import jax.numpy as jnp
from jax import lax
from jax.experimental import pallas as pl
from jax.experimental.pallas import tpu as pltpu

F32 = jnp.float32
BF16 = jnp.bfloat16
U32 = jnp.uint32
I32 = jnp.int32

LANES = 128
SUBLANES = 8
HEAD_DIM = 128
N_HEADS = 8
D_HEADS = N_HEADS * HEAD_DIM
N_EXPERTS = 32
TOP_K = 4
SWIGLU_LIMIT = 7.0
SWIGLU_ALPHA = 1.702
RMS_EPS = 1e-5
NEG_BIG = -1e30
ROW_DMA_UNROLL = 8
LOG2_E = 1.4426950408889634
BF16_ROWS = 16
F32_EXP2_UNDERFLOW = 152.0
NORM_SLACK = 1.001
F32_EXP_UNDERFLOW = 105.0
VMEM_LIMIT = 56 * 1024 * 1024


def _pick(n, candidates):
    for c in candidates:
        if n % c == 0:
            return c
    raise ValueError(f"no tile in {candidates} divides {n}")


def _params(sem, vmem=VMEM_LIMIT):
    return pltpu.CompilerParams(dimension_semantics=sem, vmem_limit_bytes=vmem)


def _rms(x, g):
    ms = jnp.mean(x * x, axis=-1, keepdims=True)
    return x * lax.rsqrt(ms + RMS_EPS) * g


def _inproj_kernel(x_ref, g_ref, w_ref, wf_ref, p_ref, f_ref, h_scr):
    @pl.when(pl.program_id(1) == 0)
    def _():
        hb = _rms(x_ref[...], g_ref[...]).astype(BF16)
        h_scr[...] = hb
        f_ref[...] = jnp.dot(hb, wf_ref[...], preferred_element_type=F32)

    p_ref[...] = jnp.dot(h_scr[...], w_ref[...], preferred_element_type=F32).astype(BF16)


def _in_proj(x, g, w_main, w_f):
    s, d = x.shape
    n = w_main.shape[1]
    tm = _pick(s, (1024, 512, 256, 128))
    tn = _pick(n, (1024, 512, 256, 128))
    return pl.pallas_call(
        _inproj_kernel,
        grid=(s // tm, n // tn),
        in_specs=[
            pl.BlockSpec((tm, d), lambda m, j: (m, 0)),
            pl.BlockSpec((1, d), lambda m, j: (0, 0)),
            pl.BlockSpec((d, tn), lambda m, j: (0, j)),
            pl.BlockSpec((d, LANES), lambda m, j: (0, 0)),
        ],
        out_specs=[
            pl.BlockSpec((tm, tn), lambda m, j: (m, j)),
            pl.BlockSpec((tm, LANES), lambda m, j: (m, 0)),
        ],
        out_shape=[
            jax.ShapeDtypeStruct((s, n), BF16),
            jax.ShapeDtypeStruct((s, LANES), F32),
        ],
        scratch_shapes=[pltpu.VMEM((tm, d), BF16)],
        compiler_params=_params(("parallel", "arbitrary")),
        name="in_proj",
    )(x, g, w_main, w_f)


_NT = (((1,), (1,)), ((), ()))


def _sb_kernel(q_ref, k_ref, vt_ref, o_ref, acc_scr, *, tq, tk, scale):
    i = pl.program_id(1)
    nsub = tq // tk
    q = q_ref[...]
    row = lax.broadcasted_iota(I32, (tk, 2 * tk), 0)
    col = lax.broadcasted_iota(I32, (tk, 2 * tk), 1)
    ut2 = ((col % tk) >= row).astype(BF16)
    diff = lax.broadcasted_iota(I32, (tk, tq), 1) - lax.broadcasted_iota(I32, (tk, tq), 0)

    def sub_block(kb, csum, mask):
        k = k_ref[pl.ds(pl.multiple_of(kb * tk, tk), tk), :]
        z = lax.dot_general(k, q, _NT, preferred_element_type=F32) * scale
        lk = -(jnp.maximum(z, 0.0) + jnp.log(1.0 + jnp.exp(-jnp.abs(z))))
        if mask is not None:
            lk = jnp.where(mask, lk, 0.0)
        hi = lk.astype(BF16)
        lo = (lk - hi.astype(F32)).astype(BF16)
        cs = jnp.dot(ut2, jnp.concatenate([hi, lo], axis=0), preferred_element_type=F32)
        a = jnp.exp(z + cs + csum)
        if mask is not None:
            a = jnp.where(mask, a, 0.0)
        return a.astype(BF16), csum + cs[0:1, :]

    def tile(t, csum, masked):
        parts = [None] * nsub
        for u in reversed(range(nsub)):
            parts[u], csum = sub_block(t * nsub + u, csum, (diff > u * tk) if masked else None)
        acc_scr[...] += jnp.dot(vt_ref[t], jnp.concatenate(parts, axis=0),
                                preferred_element_type=F32)
        return csum

    acc_scr[...] = jnp.zeros_like(acc_scr)
    csum = tile(i, jnp.zeros((1, tq), F32), True)

    def live(state):
        jj, c = state
        return jnp.logical_and(jj < i, jnp.max(c) > -F32_EXP_UNDERFLOW)

    lax.while_loop(live, lambda st: (st[0] + 1, tile(i - 1 - st[0], st[1], False)),
                   (jnp.int32(0), csum))
    o_ref[...] = acc_scr[...].T.astype(o_ref.dtype)


def _fox_kernel(qn_ref, kn_ref, ff_ref, fl_ref, q_ref, qa_ref, k_ref, ka_ref, vt_ref, o_ref,
                acc_scr, s_scr, *, tq, tk, scale):
    i = pl.program_id(1)
    nsub = tq // tk
    q = jnp.concatenate([q_ref[...], qa_ref[...]], axis=1)
    diff = lax.broadcasted_iota(I32, (tk, tq), 1) - lax.broadcasted_iota(I32, (tk, tq), 0)
    to_log2 = scale * LOG2_E

    def scores(t, slot, masked):
        for u in range(nsub):
            ks = pl.multiple_of((t * nsub + u) * tk, tk)
            k = jnp.concatenate([k_ref[pl.ds(ks, tk), :], ka_ref[pl.ds(ks, tk), :]], axis=1)
            s = lax.dot_general(k, q, _NT, preferred_element_type=F32) * to_log2
            if masked:
                s = jnp.where(diff >= u * tk, s, NEG_BIG)
            s_scr[slot, pl.ds(u * tk, tk), :] = s

    def consume(t, slot, m):
        top = s_scr[slot, pl.ds(0, tk), :]
        for u in range(1, nsub):
            top = jnp.maximum(top, s_scr[slot, pl.ds(u * tk, tk), :])
        m_new = jnp.maximum(m, jnp.max(top, axis=0, keepdims=True))
        alpha = jnp.exp2(m - m_new)
        pcat = jnp.exp2(s_scr[slot] - m_new).astype(BF16)
        acc_scr[...] = alpha * acc_scr[...] + jnp.dot(vt_ref[t], pcat, preferred_element_type=F32)
        return m_new

    def step(t, slot, m):
        scores(t - 1, 1 - slot, False)
        return consume(t, slot, m)

    acc_scr[...] = jnp.zeros_like(acc_scr)
    scores(i, 0, True)

    @pl.when(i > 0)
    def _():
        scores(i - 1, 1, False)

    m = consume(i, 0, jnp.full((1, tq), NEG_BIG, F32))

    @pl.when(i > 0)
    def _():
        nt = pl.num_programs(1)
        row = pl.program_id(0) * nt
        floor = jnp.min(m) - F32_EXP2_UNDERFLOW
        qk = to_log2 * NORM_SLACK * qn_ref[row + i]
        f_first = LOG2_E * ff_ref[row + i]

        def live(extra):
            t = i - 2 - extra
            bound = qk * kn_ref[row + jnp.maximum(t, 0)] + (
                f_first - LOG2_E * fl_ref[row + jnp.maximum(t, 0)])
            return jnp.logical_and(t >= 0, bound >= floor)

        extra = lax.while_loop(live, lambda e: e + 1, jnp.int32(0))
        last = i - 1 - extra

        def trip(k, m):
            t = i - 1 - 2 * k
            return step(t - 1, 0, step(t, 1, m))

        m2 = lax.fori_loop(0, extra // 2, trip, m)

        @pl.when(extra % 2 == 0)
        def _():
            consume(last, 1, m2)

        @pl.when(extra % 2 == 1)
        def _():
            consume(last, 0, step(last + 1, 1, m2))

    acc = acc_scr[...]
    o_ref[...] = (acc[:HEAD_DIM] / acc[HEAD_DIM:HEAD_DIM + 1]).T.astype(o_ref.dtype)


def _attn_tile(s):
    return _pick(s, (512, 256, 128))


def _attention(p, vt, q_col, k_col, fox=None):
    s = p.shape[0]
    tq = _attn_tile(s)
    tk = 128
    scale = HEAD_DIM ** -0.5
    rows = vt.shape[2]
    q_spec = pl.BlockSpec((tq, HEAD_DIM), lambda h, i, *_: (i, q_col + h))
    k_spec = pl.BlockSpec((s, HEAD_DIM), lambda h, i, *_: (0, k_col + h))
    vt_spec = pl.BlockSpec((None, s // tq, rows, tq), lambda h, i, *_: (h, 0, 0, 0))
    scratch = [pltpu.VMEM((rows, tq), F32)]
    if fox is None:
        kern = functools.partial(_sb_kernel, tq=tq, tk=tk, scale=scale)
        name = "sb_attn"
        tables = []
        in_specs = [q_spec, k_spec, vt_spec]
        args = [p, p, vt]
    else:
        tables, qa, ka = fox
        scratch.append(pltpu.VMEM((2, tq, tq), F32))
        kern = functools.partial(_fox_kernel, tq=tq, tk=tk, scale=scale)
        name = "fox_attn"
        in_specs = [
            q_spec,
            pl.BlockSpec((None, tq, HEAD_DIM), lambda h, i, *_: (h, i, 0)),
            k_spec,
            pl.BlockSpec((None, s, HEAD_DIM), lambda h, i, *_: (h, 0, 0)),
            vt_spec,
        ]
        args = [p, qa, p, ka, vt]
    return pl.pallas_call(
        kern,
        grid_spec=pltpu.PrefetchScalarGridSpec(
            num_scalar_prefetch=len(tables),
            grid=(N_HEADS, s // tq),
            in_specs=in_specs,
            out_specs=pl.BlockSpec((tq, HEAD_DIM), lambda h, i, *_: (i, h)),
            scratch_shapes=scratch,
        ),
        out_shape=jax.ShapeDtypeStruct((s, D_HEADS), BF16),
        compiler_params=_params(("parallel", "arbitrary")),
        name=name,
    )(*tables, *args)


def _fox_tables(q, k, cum_f):
    s = q.shape[0]
    tq = _attn_tile(s)
    nt = s // tq

    def tile_max_norm(x):
        sq = jnp.square(x.astype(F32)).reshape(nt, tq, N_HEADS, HEAD_DIM).sum(axis=-1)
        return jnp.sqrt(sq.max(axis=1)).T

    f_tiles = cum_f.reshape(N_HEADS, nt, tq)
    tables = [tile_max_norm(q), lax.cummax(tile_max_norm(k), axis=1),
              f_tiles[:, :, 0], f_tiles[:, :, tq - 1]]
    return [t.reshape(-1) for t in tables]


def _heads_t(v, ones_row=False):
    s = v.shape[0]
    tq = _attn_tile(s)
    vt = v.reshape(s // tq, tq, N_HEADS, HEAD_DIM).transpose(2, 0, 3, 1)
    if ones_row:
        extra = jnp.zeros((N_HEADS, s // tq, BF16_ROWS, tq), v.dtype).at[:, :, 0, :].set(1)
        vt = jnp.concatenate([vt, extra], axis=2)
    return vt


def _bf16_trunc(x):
    bits = lax.bitcast_convert_type(x, U32) & jnp.uint32(0xFFFF0000)
    return lax.bitcast_convert_type(bits, F32)


def _split3(x):
    a = _bf16_trunc(x)
    b = _bf16_trunc(x - a)
    c = _bf16_trunc(x - a - b)
    return [a.astype(BF16), b.astype(BF16), c.astype(BF16)]


def _fox_bias(cum_f):
    parts = [t[:, :, None] for t in _split3(cum_f * (HEAD_DIM ** 0.5))]
    col = lax.broadcasted_iota(I32, cum_f.shape + (HEAD_DIM,), 2)
    zero = jnp.zeros((), BF16)
    one = jnp.ones((), BF16)

    def place(first, vals):
        out = zero
        for n, v in enumerate(vals):
            out = jnp.where(col == first + n, v, out)
        return out

    qa = jnp.where(col < 3, one, place(3, parts))
    ka = jnp.where((col >= 3) & (col < 6), one, place(0, [-t for t in parts]))
    return qa, ka


def _bf16_pair_pack(h):
    bits = lax.bitcast_convert_type(h, U32)
    rnd = bits + jnp.uint32(0x7FFF) + ((bits >> 16) & jnp.uint32(1))
    half = h.shape[1] // 2
    return (rnd[:, half:] & jnp.uint32(0xFFFF0000)) | (rnd[:, :half] >> 16)


def _bf16_pair_unpack(w):
    lo = lax.bitcast_convert_type(w << 16, F32)
    hi = lax.bitcast_convert_type(w & jnp.uint32(0xFFFF0000), F32)
    return jnp.concatenate([lo, hi], axis=1).astype(BF16)


def _mix_kernel(osb_ref, ofx_ref, ga_ref, gb_ref, bg_ref, wpa_ref, wpb_ref, wo_ref, x_ref, g_ref,
                wrh_ref, wrl_ref, br_ref,
                x2_ref, hp_ref, ri_ref, rw_ref, cnt_ref, cnt_scr, *, tm, d, cap):
    t = pl.program_id(0)

    @pl.when(t == 0)
    def _():
        cnt_scr[...] = jnp.zeros_like(cnt_scr)

    bg = bg_ref[...]
    pa = jnp.dot(osb_ref[...], wpa_ref[...], preferred_element_type=F32)
    pb = jnp.dot(ofx_ref[...], wpb_ref[...], preferred_element_type=F32)
    ga = jax.nn.sigmoid(ga_ref[...].astype(F32) + bg[:, :d])
    gb = jax.nn.sigmoid(gb_ref[...].astype(F32) + bg[:, d:])
    mixed = (ga * pa + gb * pb).astype(BF16)
    x2 = x_ref[...] + jnp.dot(mixed, wo_ref[...], preferred_element_type=F32)
    x2_ref[...] = x2
    h2 = _rms(x2, g_ref[...])
    hp_ref[...] = _bf16_pair_pack(h2)

    hh = h2.astype(BF16)
    hl = (h2 - hh.astype(F32)).astype(BF16)
    logits = (jnp.dot(hh, wrh_ref[...], preferred_element_type=F32)
              + jnp.dot(hh, wrl_ref[...], preferred_element_type=F32)
              + jnp.dot(hl, wrh_ref[...], preferred_element_type=F32)
              + br_ref[...])

    lane = lax.broadcasted_iota(I32, (tm, LANES), 1).astype(F32)
    work = logits
    sel = jnp.zeros((tm, LANES), F32)
    vals, idxs = [], []
    for _ in range(TOP_K):
        mx = jnp.max(work, axis=1, keepdims=True)
        idx = jnp.min(jnp.where(work == mx, lane, float(LANES)), axis=1, keepdims=True)
        hit = lane == idx
        vals.append(mx)
        idxs.append(idx)
        work = jnp.where(hit, -jnp.inf, work)
        sel = jnp.where(hit, 1.0, sel)
    exps = [jnp.exp(v - vals[0]) for v in vals]
    denom = exps[0] + exps[1] + exps[2] + exps[3]

    row = lax.broadcasted_iota(I32, (tm, tm), 0)
    col = lax.broadcasted_iota(I32, (tm, tm), 1)
    tri = (col < row).astype(BF16)
    rank = jnp.dot(tri, sel.astype(BF16), preferred_element_type=F32) + cnt_scr[...]
    cnt_new = cnt_scr[...] + jnp.sum(sel, axis=0, keepdims=True)
    cnt_scr[...] = cnt_new
    cnt_ref[...] = cnt_new
    slot = rank + lane * float(cap)
    ri = jnp.zeros((tm, LANES), F32)
    rw = jnp.zeros((tm, LANES), F32)
    for r in range(TOP_K):
        slot_r = jnp.sum(jnp.where(lane == idxs[r], slot, 0.0), axis=1, keepdims=True)
        ri = jnp.where(lane == float(r), slot_r, ri)
        rw = jnp.where(lane == float(r), exps[r] / denom, rw)
    ri_ref[...] = ri.astype(I32)
    rw_ref[...] = rw


def _mix_route(o_sb, o_fx, p, gate_col, b_gate, w_pa, w_pb, w_out, x, g_ffn, wr_hi, wr_lo, br, cap):
    s, d = x.shape
    tm = _pick(s, (256, 128))
    const = lambda shape: pl.BlockSpec(shape, lambda t: (0, 0), pipeline_mode=pl.Buffered(1))
    kern = functools.partial(_mix_kernel, tm=tm, d=d, cap=cap)
    return pl.pallas_call(
        kern,
        grid=(s // tm,),
        in_specs=[
            pl.BlockSpec((tm, D_HEADS), lambda t: (t, 0)),
            pl.BlockSpec((tm, D_HEADS), lambda t: (t, 0)),
            pl.BlockSpec((tm, d), lambda t: (t, gate_col)),
            pl.BlockSpec((tm, d), lambda t: (t, gate_col + 1)),
            const((1, 2 * d)),
            const((D_HEADS, d)),
            const((D_HEADS, d)),
            const((d, d)),
            pl.BlockSpec((tm, d), lambda t: (t, 0)),
            const((1, d)),
            const((d, LANES)),
            const((d, LANES)),
            const((1, LANES)),
        ],
        out_specs=[
            pl.BlockSpec((tm, d), lambda t: (t, 0)),
            pl.BlockSpec((tm, d // 2), lambda t: (t, 0)),
            pl.BlockSpec((tm, LANES), lambda t: (t, 0)),
            pl.BlockSpec((tm, LANES), lambda t: (t, 0)),
            pl.BlockSpec((1, LANES), lambda t: (0, 0)),
        ],
        out_shape=[
            jax.ShapeDtypeStruct((s, d), F32),
            jax.ShapeDtypeStruct((s, d // 2), U32),
            jax.ShapeDtypeStruct((s, LANES), I32),
            jax.ShapeDtypeStruct((s, LANES), F32),
            jax.ShapeDtypeStruct((1, LANES), F32),
        ],
        scratch_shapes=[pltpu.VMEM((1, LANES), F32)],
        compiler_params=_params(("arbitrary",)),
        name="mix_route",
    )(o_sb, o_fx, p, p, b_gate, w_pa, w_pb, w_out, x, g_ffn, wr_hi, wr_lo, br)


def _dispatch_kernel(slot_ref, cnt_ref, hp_ref, xs_ref, zero_scr, sem, zsem, *, tm, ch, cap):
    t = pl.program_id(0)
    base = t * tm

    def row_copy(r, k):
        return pltpu.make_async_copy(
            hp_ref.at[pl.ds(r, 1), :],
            xs_ref.at[pl.ds(slot_ref[(base + r) * TOP_K + k], 1), :],
            sem)

    def start(r, c):
        for k in range(TOP_K):
            row_copy(r, k).start()
        return c

    def wait(r, c):
        for k in range(TOP_K):
            row_copy(r, k).wait()
        return c

    @pl.when(t == 0)
    def _():
        zero_scr[...] = jnp.zeros_like(zero_scr)

        def pad_copy(e):
            first = pl.multiple_of(e * cap + (cnt_ref[e] // SUBLANES) * SUBLANES, SUBLANES)
            return pltpu.make_async_copy(zero_scr, xs_ref.at[pl.ds(first, ch), :], zsem)

        def zstart(e, c):
            pad_copy(e).start()
            return c

        def zwait(e, c):
            pad_copy(e).wait()
            return c

        lax.fori_loop(0, N_EXPERTS, zstart, 0)
        lax.fori_loop(0, N_EXPERTS, zwait, 0)

    lax.fori_loop(0, tm, start, 0, unroll=ROW_DMA_UNROLL)
    lax.fori_loop(0, tm, wait, 0, unroll=ROW_DMA_UNROLL)


def _dispatch(slots, counts, hp, cap, ch):
    s, half = hp.shape
    tm = _pick(s, (512, 256, 128))
    kern = functools.partial(_dispatch_kernel, tm=tm, ch=ch, cap=cap)
    return pl.pallas_call(
        kern,
        grid_spec=pltpu.PrefetchScalarGridSpec(
            num_scalar_prefetch=2,
            grid=(s // tm,),
            in_specs=[pl.BlockSpec((tm, half), lambda t, sl, cn: (t, 0))],
            out_specs=pl.BlockSpec(memory_space=pl.ANY),
            scratch_shapes=[
                pltpu.VMEM((ch, half), U32),
                pltpu.SemaphoreType.DMA(()),
                pltpu.SemaphoreType.DMA(()),
            ],
        ),
        out_shape=jax.ShapeDtypeStruct((N_EXPERTS * cap, half), U32),
        compiler_params=_params(("arbitrary",)),
        name="dispatch",
    )(slots, counts, hp)


def _expert_kernel(sbe_ref, sbblk_ref, sbnch_ref, x_ref, w1a_ref, w1c_ref, b1_ref, w2a_ref, w2c_ref,
                   b2_ref, y_ref, xb_scr, w1b_scr, w2b_scr, *, ch, ff):
    s = pl.program_id(0)
    j = pl.program_id(1)
    nch = sbnch_ref[s]
    half = LANES // 2

    def half_act(g):
        lin = pltpu.roll(g, LANES - 1, 1)
        gate = jnp.minimum(g, SWIGLU_LIMIT)
        lin = jnp.clip(lin, -SWIGLU_LIMIT, SWIGLU_LIMIT)
        return gate * jax.nn.sigmoid(SWIGLU_ALPHA * gate) * (lin + 1.0)

    @pl.when(nch > 0)
    def _():
        @pl.when(j == 0)
        def _():
            xb_scr[...] = _bf16_pair_unpack(x_ref[...])

            def init(c, carry):
                y_ref[pl.ds(pl.multiple_of(c * ch, ch), ch), :] = jnp.broadcast_to(
                    b2_ref[...], (ch, y_ref.shape[1]))
                return carry

            lax.fori_loop(0, nch, init, 0)

        w1b_scr[:, :ff] = w1a_ref[...].astype(BF16)
        w1b_scr[:, ff:] = w1c_ref[...].astype(BF16)
        prow = lax.broadcasted_iota(I32, (LANES, LANES), 0)
        pcol = lax.broadcasted_iota(I32, (LANES, LANES), 1)
        perm = (pcol == prow // 2 + half * (prow % 2)).astype(BF16)
        groups = ff // LANES
        for u in range(groups):
            src_ref, v = (w2a_ref, u) if u < groups // 2 else (w2c_ref, u - groups // 2)
            grp = src_ref[pl.ds(v * LANES, LANES), :].astype(BF16)
            w2b_scr[pl.ds(u * LANES, LANES), :] = jnp.dot(
                perm, grp, preferred_element_type=F32).astype(BF16)

        def up(r0, rows):
            gu = jnp.dot(xb_scr[pl.ds(r0, rows), :], w1b_scr[...],
                         preferred_element_type=F32) + b1_ref[...]
            even = (lax.broadcasted_iota(I32, (rows, LANES), 1) % 2) == 0
            pieces = []
            for u in range(ff // LANES):
                a = half_act(gu[:, (2 * u) * LANES:(2 * u + 1) * LANES])
                b = half_act(gu[:, (2 * u + 1) * LANES:(2 * u + 2) * LANES])
                pieces.append(jnp.where(even, a, pltpu.roll(b, 1, 1)))
            return jnp.concatenate(pieces, axis=1).astype(BF16)

        def down(r0, rows, act):
            y_ref[pl.ds(r0, rows), :] += jnp.dot(act, w2b_scr[...], preferred_element_type=F32)

        big = 2 * ch
        nbig = nch // 2

        @pl.when(nbig > 0)
        def _():
            def trip(c, act):
                nxt = up(pl.multiple_of(c * big, big), big)
                down(pl.multiple_of((c - 1) * big, big), big, act)
                return nxt

            act = lax.fori_loop(1, nbig, trip, up(0, big))
            down(pl.multiple_of((nbig - 1) * big, big), big, act)

        @pl.when(nch % 2 == 1)
        def _():
            r0 = pl.multiple_of(nbig * big, ch)
            down(r0, ch, up(r0, ch))


def _experts(sb_e, sb_blk, sb_nch, xs, w1, b1, w2, b2, r_rows, ch, cap):
    ne, d, ff2 = w1.shape
    dff = ff2 // 2
    ff = _pick(dff, (256, 128))
    nj = dff // ff
    ns = sb_e.shape[0]
    half = d // 2
    blocks_per_expert = cap // r_rows

    def jmap(s, j, nch):
        return jnp.where(nch[s] > 0, j, nj - 1)

    kern = functools.partial(_expert_kernel, ch=ch, ff=ff)
    return pl.pallas_call(
        kern,
        grid_spec=pltpu.PrefetchScalarGridSpec(
            num_scalar_prefetch=3,
            grid=(ns, nj),
            in_specs=[
                pl.BlockSpec((r_rows, half),
                             lambda s, j, e, b, n: (e[s] * blocks_per_expert + b[s], 0)),
                pl.BlockSpec((None, d, ff), lambda s, j, e, b, n: (e[s], 0, 2 * jmap(s, j, n))),
                pl.BlockSpec((None, d, ff), lambda s, j, e, b, n: (e[s], 0, 2 * jmap(s, j, n) + 1)),
                pl.BlockSpec((None, 1, 2 * ff), lambda s, j, e, b, n: (e[s], 0, jmap(s, j, n))),
                pl.BlockSpec((None, ff // 2, d), lambda s, j, e, b, n: (e[s], 2 * jmap(s, j, n), 0)),
                pl.BlockSpec((None, ff // 2, d),
                             lambda s, j, e, b, n: (e[s], 2 * jmap(s, j, n) + 1, 0)),
                pl.BlockSpec((None, 1, d), lambda s, j, e, b, n: (e[s], 0, 0)),
            ],
            out_specs=pl.BlockSpec((r_rows, d),
                                   lambda s, j, e, b, n: (e[s] * blocks_per_expert + b[s], 0)),
            scratch_shapes=[
                pltpu.VMEM((r_rows, d), BF16),
                pltpu.VMEM((d, 2 * ff), BF16),
                pltpu.VMEM((ff, d), BF16),
            ],
        ),
        out_shape=jax.ShapeDtypeStruct((ne * cap, d), F32),
        compiler_params=_params(("arbitrary", "arbitrary")),
        name="experts",
    )(sb_e, sb_blk, sb_nch, xs, w1, w1, b1.reshape(ne, 1, ff2), w2, w2, b2.reshape(ne, 1, d))


def _combine_kernel(slot_ref, x2_ref, rw_ref, g_ref, ys_ref, o_ref, gbuf, sem, *, tm, final_norm):
    t = pl.program_id(0)
    base = t * tm

    def row_copy(r, k):
        return pltpu.make_async_copy(
            ys_ref.at[pl.ds(slot_ref[(base + r) * TOP_K + k], 1), :],
            gbuf.at[pl.ds(k * tm + r, 1), :],
            sem)

    def start(r, c):
        for k in range(TOP_K):
            row_copy(r, k).start()
        return c

    def wait(r, c):
        for k in range(TOP_K):
            row_copy(r, k).wait()
        return c

    lax.fori_loop(0, tm, start, 0, unroll=ROW_DMA_UNROLL)
    lax.fori_loop(0, tm, wait, 0, unroll=ROW_DMA_UNROLL)

    rw = rw_ref[...]
    out = x2_ref[...]
    for k in range(TOP_K):
        out = out + rw[:, k:k + 1] * gbuf[pl.ds(k * tm, tm), :]
    if final_norm:
        out = _rms(out, g_ref[...])
    o_ref[...] = out


def _combine(slots, x2, rw, g, ys, final_norm):
    s, d = x2.shape
    tm = _pick(s, (256, 128))
    kern = functools.partial(_combine_kernel, tm=tm, final_norm=final_norm)
    return pl.pallas_call(
        kern,
        grid_spec=pltpu.PrefetchScalarGridSpec(
            num_scalar_prefetch=1,
            grid=(s // tm,),
            in_specs=[
                pl.BlockSpec((tm, d), lambda t, sl: (t, 0)),
                pl.BlockSpec((tm, LANES), lambda t, sl: (t, 0)),
                pl.BlockSpec((1, d), lambda t, sl: (0, 0)),
                pl.BlockSpec(memory_space=pl.ANY),
            ],
            out_specs=pl.BlockSpec((tm, d), lambda t, sl: (t, 0)),
            scratch_shapes=[pltpu.VMEM((TOP_K * tm, d), F32), pltpu.SemaphoreType.DMA(())],
        ),
        out_shape=jax.ShapeDtypeStruct((s, d), F32),
        compiler_params=_params(("arbitrary",)),
        name="combine",
    )(slots, x2, rw, g, ys)


def _superblocks(counts, r_rows, ch, ns):
    nsb = (counts + r_rows - 1) // r_rows
    ends = jnp.cumsum(nsb)
    total = ends[-1]
    sidx = jnp.arange(ns, dtype=I32)
    last = jnp.maximum(total - 1, 0)
    sclip = jnp.minimum(sidx, last)
    e = jnp.minimum(jnp.searchsorted(ends, sclip, side="right"), N_EXPERTS - 1).astype(I32)
    blk = sclip - (ends[e] - nsb[e])
    rows = jnp.clip(counts[e] - blk * r_rows, 0, r_rows)
    nch = jnp.where(sidx < total, (rows + ch - 1) // ch, 0)
    return e, blk.astype(I32), nch.astype(I32)


def _layer(x, g_mix, w_in, b_forget, b_gate, w_pa, w_pb, w_out, g_ffn,
           w_router, b_router, w1, b1, w2, b2, g_out, final_norm):
    s, d = x.shape
    qkv = 6 * D_HEADS

    w_main = jnp.concatenate([w_in[:, :qkv], w_in[:, qkv + N_HEADS:]], axis=1).astype(BF16)
    w_f = jnp.pad(w_in[:, qkv:qkv + N_HEADS], ((0, 0), (0, LANES - N_HEADS))).astype(BF16)
    p, f_log = _in_proj(x, g_mix.reshape(1, d), w_main, w_f)

    log_f = jax.nn.log_sigmoid(f_log[:, :N_HEADS] + b_forget)
    cum_f = jnp.cumsum(log_f, axis=0).T

    nb = D_HEADS // HEAD_DIM
    o_sb = _attention(p, _heads_t(p[:, 2 * D_HEADS:3 * D_HEADS]), 0, nb)
    tables = _fox_tables(p[:, 3 * D_HEADS:4 * D_HEADS], p[:, 4 * D_HEADS:5 * D_HEADS], cum_f)
    o_fx = _attention(p, _heads_t(p[:, 5 * D_HEADS:6 * D_HEADS], ones_row=True), 3 * nb, 4 * nb,
                      fox=(tables, *_fox_bias(cum_f)))

    r_rows = 1280
    ch = 256
    cap = -(-(s + ch) // r_rows) * r_rows
    ns =(s * TOP_K) // r_rows + N_EXPERTS
    wr = jnp.pad(w_router, ((0, 0), (0, LANES - N_EXPERTS)))
    wr_top = _bf16_trunc(wr)
    wr_hi = wr_top.astype(BF16)
    wr_lo = (wr - wr_top).astype(BF16)
    br = jnp.pad(b_router, (0, LANES - N_EXPERTS), constant_values=NEG_BIG).reshape(1, LANES)
    x2, hp, ri, rw, cnt = _mix_route(
        o_sb, o_fx, p, qkv // d, b_gate.reshape(1, 2 * d), w_pa.astype(BF16), w_pb.astype(BF16),
        w_out.astype(BF16), x, g_ffn.reshape(1, d), wr_hi, wr_lo, br, cap)

    slots = ri[:, :TOP_K].reshape(-1)
    counts = cnt[0, :N_EXPERTS].astype(I32)
    xs = _dispatch(slots, counts, hp, cap, ch)
    sb_e, sb_blk, sb_nch = _superblocks(counts, r_rows, ch, ns)
    ys = _experts(sb_e, sb_blk, sb_nch, xs, w1, b1, w2, b2, r_rows, ch, cap)
    return _combine(slots, x2, rw, g_out.reshape(1, d), ys, final_norm)


def kernel(x, g_mix, w_in, b_forget, b_gate, w_proj_sb, w_proj_fox, w_out, g_ffn, w_router,
           b_router, w_mlp1, b_mlp1, w_mlp2, b_mlp2, g_final):
    b, s, d = x.shape
    depth = g_mix.shape[0]
    outs = []
    for bi in range(b):
        xb = x[bi]
        for layer in range(depth):
            last = layer == depth - 1
            xb = _layer(xb, g_mix[layer], w_in[layer], b_forget[layer], b_gate[layer],
                        w_proj_sb[layer], w_proj_fox[layer], w_out[layer], g_ffn[layer],
                        w_router[layer], b_router[layer], w_mlp1[layer], b_mlp1[layer],
                        w_mlp2[layer], b_mlp2[layer], g_final, last)
        outs.append(xb.reshape(1, s, d))
    return outs[0] if b == 1 else jnp.concatenate(outs, axis=0)
```

```python
import functools

import jax
import jax.numpy as jnp
from jax import lax
from jax.experimental import pallas as pl
from jax.experimental.pallas import tpu as pltpu

F32 = jnp.float32
BF16 = jnp.bfloat16
U32 = jnp.uint32
I32 = jnp.int32

LANES = 128
SUBLANES = 8
HEAD_DIM = 128
N_HEADS = 8
D_HEADS = N_HEADS * HEAD_DIM
N_EXPERTS = 32
TOP_K = 4
SWIGLU_LIMIT = 7.0
SWIGLU_ALPHA = 1.702
RMS_EPS = 1e-5
NEG_BIG = -1e30
ROW_DMA_UNROLL = 8
LOG2_E = 1.4426950408889634
BF16_ROWS = 16
F32_EXP2_UNDERFLOW = 152.0
NORM_SLACK = 1.001
F32_EXP_UNDERFLOW = 105.0
VMEM_LIMIT = 56 * 1024 * 1024


def _pick(n, candidates):
    for c in candidates:
        if n % c == 0:
            return c
    raise ValueError(f"no tile in {candidates} divides {n}")


def _params(sem, vmem=VMEM_LIMIT):
    return pltpu.CompilerParams(dimension_semantics=sem, vmem_limit_bytes=vmem)


def _rms(x, g):
    ms = jnp.mean(x * x, axis=-1, keepdims=True)
    return x * lax.rsqrt(ms + RMS_EPS) * g


def _inproj_kernel(x_ref, g_ref, w_ref, wf_ref, p_ref, f_ref, vta_ref, vtb_ref, h_scr, *, tn, tq):
    n = pl.program_id(1)

    @pl.when(n == 0)
    def _():
        hb = _rms(x_ref[...], g_ref[...]).astype(BF16)
        h_scr[...] = hb
        f_ref[...] = jnp.dot(hb, wf_ref[...], preferred_element_type=F32)

    res = jnp.dot(h_scr[...], w_ref[...], preferred_element_type=F32)
    p_ref[...] = res.astype(BF16)

    heads_per_tile = tn // HEAD_DIM
    for section, vt_ref in ((2, vta_ref), (5, vtb_ref)):
        for tile in range(section * D_HEADS // tn, (section + 1) * D_HEADS // tn):
            @pl.when(n == tile)
            def _(vt_ref=vt_ref, head0=(tile * tn - section * D_HEADS) // HEAD_DIM):
                for hh in range(heads_per_tile):
                    vt = res[:, hh * HEAD_DIM:(hh + 1) * HEAD_DIM].T.astype(BF16)
                    for c in range(vt.shape[1] // tq):
                        vt_ref[head0 + hh, c, pl.ds(0, HEAD_DIM), :] = vt[:, c * tq:(c + 1) * tq]
                        if vt_ref.shape[2] > HEAD_DIM:
                            extra = lax.broadcasted_iota(I32, (BF16_ROWS, tq), 0) == 0
                            vt_ref[head0 + hh, c, pl.ds(HEAD_DIM, BF16_ROWS), :] = extra.astype(BF16)


def _wprep_kernel(a_ref, b_ref, o_ref, *, first_tail, skip):
    n = pl.program_id(1)

    @pl.when(n < first_tail)
    def _():
        o_ref[...] = a_ref[...].astype(BF16)

    @pl.when(n >= first_tail)
    def _():
        ext = jnp.concatenate([a_ref[...], b_ref[...]], axis=1)
        o_ref[...] = pltpu.roll(ext, ext.shape[1] - skip, 1)[:, :o_ref.shape[1]].astype(BF16)


def _w_prep(w_in, qkv, skip):
    d, n_in = w_in.shape
    n_out = n_in - skip
    tn = _pick(n_out, (1024, 512, 256, 128))
    tr = _pick(d, (1024, 512, 256, 128))
    per = tn // LANES
    last = (n_in - 1) // LANES
    kern = functools.partial(_wprep_kernel, first_tail=qkv // tn, skip=skip)
    return pl.pallas_call(
        kern,
        grid=(d // tr, n_out // tn),
        in_specs=[
            pl.BlockSpec((tr, tn), lambda r, n: (r, n)),
            pl.BlockSpec((tr, LANES), lambda r, n: (r, jnp.minimum((n + 1) * per, last))),
        ],
        out_specs=pl.BlockSpec((tr, tn), lambda r, n: (r, n)),
        out_shape=jax.ShapeDtypeStruct((d, n_out), BF16),
        compiler_params=_params(("parallel", "parallel")),
        name="w_prep",
    )(w_in, w_in)


def _in_proj(x, g, w_main, w_f):
    s, d = x.shape
    n = w_main.shape[1]
    tm = _pick(s, (1024, 512, 256, 128))
    tn = _pick(n, (1024, 512, 256, 128))
    tq = _attn_tile(s)
    rows_b = HEAD_DIM + BF16_ROWS
    vt_spec = lambda rows: pl.BlockSpec((N_HEADS, tm // tq, rows, tq), lambda m, j: (0, m, 0, 0))
    return pl.pallas_call(
        functools.partial(_inproj_kernel, tn=tn, tq=tq),
        grid=(s // tm, n // tn),
        in_specs=[
            pl.BlockSpec((tm, d), lambda m, j: (m, 0)),
            pl.BlockSpec((1, d), lambda m, j: (0, 0)),
            pl.BlockSpec((d, tn), lambda m, j: (0, j)),
            pl.BlockSpec((d, LANES), lambda m, j: (0, 0)),
        ],
        out_specs=[
            pl.BlockSpec((tm, tn), lambda m, j: (m, j)),
            pl.BlockSpec((tm, LANES), lambda m, j: (m, 0)),
            vt_spec(HEAD_DIM),
            vt_spec(rows_b),
        ],
        out_shape=[
            jax.ShapeDtypeStruct((s, n), BF16),
            jax.ShapeDtypeStruct((s, LANES), F32),
            jax.ShapeDtypeStruct((N_HEADS, s // tq, HEAD_DIM, tq), BF16),
            jax.ShapeDtypeStruct((N_HEADS, s // tq, rows_b, tq), BF16),
        ],
        scratch_shapes=[pltpu.VMEM((tm, d), BF16)],
        compiler_params=_params(("parallel", "arbitrary")),
        name="in_proj",
    )(x, g, w_main, w_f)


_NT = (((1,), (1,)), ((), ()))


def _sb_kernel(q_ref, k_ref, vt_ref, o_ref, acc_scr, *, tq, tk, scale):
    i = pl.program_id(1)
    nsub = tq // tk
    q = q_ref[...]
    row = lax.broadcasted_iota(I32, (tk, 2 * tk), 0)
    col = lax.broadcasted_iota(I32, (tk, 2 * tk), 1)
    ut2 = ((col % tk) >= row).astype(BF16)
    diff = lax.broadcasted_iota(I32, (tk, tq), 1) - lax.broadcasted_iota(I32, (tk, tq), 0)

    def sub_block(kb, csum, mask):
        k = k_ref[pl.ds(pl.multiple_of(kb * tk, tk), tk), :]
        z = lax.dot_general(k, q, _NT, preferred_element_type=F32) * scale
        lk = -(jnp.maximum(z, 0.0) + jnp.log(1.0 + jnp.exp(-jnp.abs(z))))
        if mask is not None:
            lk = jnp.where(mask, lk, 0.0)
        hi = lk.astype(BF16)
        lo = (lk - hi.astype(F32)).astype(BF16)
        cs = jnp.dot(ut2, jnp.concatenate([hi, lo], axis=0), preferred_element_type=F32)
        a = jnp.exp(z + cs + csum)
        if mask is not None:
            a = jnp.where(mask, a, 0.0)
        return a.astype(BF16), csum + cs[0:1, :]

    def tile(t, csum, masked):
        parts = [None] * nsub
        for u in reversed(range(nsub)):
            parts[u], csum = sub_block(t * nsub + u, csum, (diff > u * tk) if masked else None)
        acc_scr[...] += jnp.dot(vt_ref[t], jnp.concatenate(parts, axis=0),
                                preferred_element_type=F32)
        return csum

    acc_scr[...] = jnp.zeros_like(acc_scr)
    csum = tile(i, jnp.zeros((1, tq), F32), True)

    def live(state):
        jj, c = state
        return jnp.logical_and(jj < i, jnp.max(c) > -F32_EXP_UNDERFLOW)

    lax.while_loop(live, lambda st: (st[0] + 1, tile(i - 1 - st[0], st[1], False)),
                   (jnp.int32(0), csum))
    o_ref[...] = acc_scr[...].T.astype(o_ref.dtype)


def _fox_kernel(ff_ref, fl_ref, q_ref, qa_ref, k_ref, ka_ref, vt_ref, o_ref,
                acc_scr, s_scr, kn_smem, *, tq, tk, scale):
    i = pl.program_id(1)
    nsub = tq // tk
    q = jnp.concatenate([q_ref[...], qa_ref[...]], axis=1)
    diff = lax.broadcasted_iota(I32, (tk, tq), 1) - lax.broadcasted_iota(I32, (tk, tq), 0)
    to_log2 = scale * LOG2_E

    def scores(t, slot, masked):
        for u in range(nsub):
            ks = pl.multiple_of((t * nsub + u) * tk, tk)
            k = jnp.concatenate([k_ref[pl.ds(ks, tk), :], ka_ref[pl.ds(ks, tk), :]], axis=1)
            s = lax.dot_general(k, q, _NT, preferred_element_type=F32) * to_log2
            if masked:
                s = jnp.where(diff >= u * tk, s, NEG_BIG)
            s_scr[slot, pl.ds(u * tk, tk), :] = s

    def consume(t, slot, m):
        top = s_scr[slot, pl.ds(0, tk), :]
        for u in range(1, nsub):
            top = jnp.maximum(top, s_scr[slot, pl.ds(u * tk, tk), :])
        m_new = jnp.maximum(m, jnp.max(top, axis=0, keepdims=True))
        alpha = jnp.exp2(m - m_new)
        pcat = jnp.exp2(s_scr[slot] - m_new).astype(BF16)
        acc_scr[...] = alpha * acc_scr[...] + jnp.dot(vt_ref[t], pcat, preferred_element_type=F32)
        return m_new

    def step(t, slot, m):
        scores(t - 1, 1 - slot, False)
        return consume(t, slot, m)

    def max_norm(x):
        xf = x.astype(F32)
        return jnp.max(jnp.sqrt(jnp.sum(xf * xf, axis=1, keepdims=True)))

    kn_prev = jnp.where(i > 0, kn_smem[jnp.maximum(i - 1, 0)], 0.0)
    kn_smem[i] = jnp.maximum(kn_prev, max_norm(k_ref[pl.ds(pl.multiple_of(i * tq, tq), tq), :]))

    acc_scr[...] = jnp.zeros_like(acc_scr)
    scores(i, 0, True)

    @pl.when(i > 0)
    def _():
        scores(i - 1, 1, False)

    m = consume(i, 0, jnp.full((1, tq), NEG_BIG, F32))

    @pl.when(i > 0)
    def _():
        nt = pl.num_programs(1)
        row = pl.program_id(0) * nt
        floor = jnp.min(m) - F32_EXP2_UNDERFLOW
        qk = to_log2 * NORM_SLACK * max_norm(q_ref[...])
        f_first = LOG2_E * ff_ref[row + i]

        def live(extra):
            t = i - 2 - extra
            bound = qk * kn_smem[jnp.maximum(t, 0)] + (
                f_first - LOG2_E * fl_ref[row + jnp.maximum(t, 0)])
            return jnp.logical_and(t >= 0, bound >= floor)

        extra = lax.while_loop(live, lambda e: e + 1, jnp.int32(0))
        last = i - 1 - extra

        def trip(k, m):
            t = i - 1 - 2 * k
            return step(t - 1, 0, step(t, 1, m))

        m2 = lax.fori_loop(0, extra // 2, trip, m)

        @pl.when(extra % 2 == 0)
        def _():
            consume(last, 1, m2)

        @pl.when(extra % 2 == 1)
        def _():
            consume(last, 0, step(last + 1, 1, m2))

    acc = acc_scr[...]
    o_ref[...] = (acc[:HEAD_DIM] / acc[HEAD_DIM:HEAD_DIM + 1]).T.astype(o_ref.dtype)


def _attn_tile(s):
    return _pick(s, (512, 256, 128))


def _attention(p, vt, q_col, k_col, fox=None):
    s = p.shape[0]
    tq = _attn_tile(s)
    tk = 128
    scale = HEAD_DIM ** -0.5
    rows = vt.shape[2]
    q_spec = pl.BlockSpec((tq, HEAD_DIM), lambda h, i, *_: (i, q_col + h))
    k_spec = pl.BlockSpec((s, HEAD_DIM), lambda h, i, *_: (0, k_col + h))
    vt_spec = pl.BlockSpec((None, s // tq, rows, tq), lambda h, i, *_: (h, 0, 0, 0))
    scratch = [pltpu.VMEM((rows, tq), F32)]
    order = ("parallel", "arbitrary")
    if fox is None:
        kern = functools.partial(_sb_kernel, tq=tq, tk=tk, scale=scale)
        name = "sb_attn"
        tables = []
        in_specs = [q_spec, k_spec, vt_spec]
        args = [p, p, vt]
    else:
        tables, qa, ka = fox
        scratch.append(pltpu.VMEM((2, tq, tq), F32))
        scratch.append(pltpu.SMEM((s // tq,), F32))
        order = ("arbitrary", "arbitrary")
        kern = functools.partial(_fox_kernel, tq=tq, tk=tk, scale=scale)
        name = "fox_attn"
        in_specs = [
            q_spec,
            pl.BlockSpec((None, tq, HEAD_DIM), lambda h, i, *_: (h, i, 0)),
            k_spec,
            pl.BlockSpec((None, s, HEAD_DIM), lambda h, i, *_: (h, 0, 0)),
            vt_spec,
        ]
        args = [p, qa, p, ka, vt]
    return pl.pallas_call(
        kern,
        grid_spec=pltpu.PrefetchScalarGridSpec(
            num_scalar_prefetch=len(tables),
            grid=(N_HEADS, s // tq),
            in_specs=in_specs,
            out_specs=pl.BlockSpec((tq, HEAD_DIM), lambda h, i, *_: (i, h)),
            scratch_shapes=scratch,
        ),
        out_shape=jax.ShapeDtypeStruct((s, D_HEADS), BF16),
        compiler_params=_params(order),
        name=name,
    )(*tables, *args)


def _fox_tables(cum_f):
    nt = cum_f.shape[1] // _attn_tile(cum_f.shape[1])
    f_tiles = cum_f.reshape(N_HEADS, nt, -1)
    return [f_tiles[:, :, 0].reshape(-1), f_tiles[:, :, -1].reshape(-1)]


def _bf16_trunc(x):
    bits = lax.bitcast_convert_type(x, U32) & jnp.uint32(0xFFFF0000)
    return lax.bitcast_convert_type(bits, F32)


def _split3(x):
    a = _bf16_trunc(x)
    b = _bf16_trunc(x - a)
    c = _bf16_trunc(x - a - b)
    return [a.astype(BF16), b.astype(BF16), c.astype(BF16)]


def _fox_bias(cum_f):
    parts = _split3(cum_f * (HEAD_DIM ** 0.5))
    ones = jnp.ones_like(parts[0])
    zeros = jnp.zeros_like(parts[0])
    widen = jnp.eye(SUBLANES, HEAD_DIM, dtype=BF16)

    def spread(cols):
        packed = jnp.stack(cols + [zeros, zeros], axis=-1)
        return jnp.einsum("hsc,cl->hsl", packed, widen,
                          preferred_element_type=F32).astype(BF16)

    return spread([ones] * 3 + parts), spread([-t for t in parts] + [ones] * 3)


def _bf16_pair_pack(h):
    bits = lax.bitcast_convert_type(h, U32)
    rnd = bits + jnp.uint32(0x7FFF) + ((bits >> 16) & jnp.uint32(1))
    half = h.shape[1] // 2
    return (rnd[:, half:] & jnp.uint32(0xFFFF0000)) | (rnd[:, :half] >> 16)


def _bf16_pair_unpack(w):
    lo = lax.bitcast_convert_type(w << 16, F32)
    hi = lax.bitcast_convert_type(w & jnp.uint32(0xFFFF0000), F32)
    return jnp.concatenate([lo, hi], axis=1).astype(BF16)


def _mix_kernel(osb_ref, ofx_ref, ga_ref, gb_ref, bg_ref, wpa_ref, wpb_ref, wo_ref, x_ref, g_ref,
                wrh_ref, wrl_ref, br_ref,
                x2_ref, hp_ref, ri_ref, rw_ref, cnt_ref, cnt_scr, *, tm, d, cap):
    t = pl.program_id(0)

    @pl.when(t == 0)
    def _():
        cnt_scr[...] = jnp.zeros_like(cnt_scr)

    bg = bg_ref[...]
    pa = jnp.dot(osb_ref[...], wpa_ref[...], preferred_element_type=F32)
    pb = jnp.dot(ofx_ref[...], wpb_ref[...], preferred_element_type=F32)
    ga = jax.nn.sigmoid(ga_ref[...].astype(F32) + bg[:, :d])
    gb = jax.nn.sigmoid(gb_ref[...].astype(F32) + bg[:, d:])
    mixed = (ga * pa + gb * pb).astype(BF16)
    x2 = x_ref[...] + jnp.dot(mixed, wo_ref[...], preferred_element_type=F32)
    x2_ref[...] = x2
    h2 = _rms(x2, g_ref[...])
    hp_ref[...] = _bf16_pair_pack(h2)

    hh = h2.astype(BF16)
    hl = (h2 - hh.astype(F32)).astype(BF16)
    logits = (jnp.dot(hh, wrh_ref[...], preferred_element_type=F32)
              + jnp.dot(hh, wrl_ref[...], preferred_element_type=F32)
              + jnp.dot(hl, wrh_ref[...], preferred_element_type=F32)
              + br_ref[...])

    lane = lax.broadcasted_iota(I32, (tm, LANES), 1).astype(F32)
    work = logits
    sel = jnp.zeros((tm, LANES), F32)
    vals, idxs = [], []
    for _ in range(TOP_K):
        mx = jnp.max(work, axis=1, keepdims=True)
        idx = jnp.min(jnp.where(work == mx, lane, float(LANES)), axis=1, keepdims=True)
        hit = lane == idx
        vals.append(mx)
        idxs.append(idx)
        work = jnp.where(hit, -jnp.inf, work)
        sel = jnp.where(hit, 1.0, sel)
    exps = [jnp.exp(v - vals[0]) for v in vals]
    denom = exps[0] + exps[1] + exps[2] + exps[3]

    row = lax.broadcasted_iota(I32, (tm, tm), 0)
    col = lax.broadcasted_iota(I32, (tm, tm), 1)
    tri = (col < row).astype(BF16)
    rank = jnp.dot(tri, sel.astype(BF16), preferred_element_type=F32) + cnt_scr[...]
    cnt_new = cnt_scr[...] + jnp.sum(sel, axis=0, keepdims=True)
    cnt_scr[...] = cnt_new
    cnt_ref[...] = cnt_new
    slot = rank + lane * float(cap)
    ri = jnp.zeros((tm, LANES), F32)
    rw = jnp.zeros((tm, LANES), F32)
    for r in range(TOP_K):
        slot_r = jnp.sum(jnp.where(lane == idxs[r], slot, 0.0), axis=1, keepdims=True)
        ri = jnp.where(lane == float(r), slot_r, ri)
        rw = jnp.where(lane == float(r), exps[r] / denom, rw)
    ri_ref[...] = ri.astype(I32)
    rw_ref[...] = rw


def _mix_route(o_sb, o_fx, p, gate_col, b_gate, w_pa, w_pb, w_out, x, g_ffn, wr_hi, wr_lo, br, cap):
    s, d = x.shape
    tm = _pick(s, (256, 128))
    const = lambda shape: pl.BlockSpec(shape, lambda t: (0, 0), pipeline_mode=pl.Buffered(1))
    kern = functools.partial(_mix_kernel, tm=tm, d=d, cap=cap)
    return pl.pallas_call(
        kern,
        grid=(s // tm,),
        in_specs=[
            pl.BlockSpec((tm, D_HEADS), lambda t: (t, 0)),
            pl.BlockSpec((tm, D_HEADS), lambda t: (t, 0)),
            pl.BlockSpec((tm, d), lambda t: (t, gate_col)),
            pl.BlockSpec((tm, d), lambda t: (t, gate_col + 1)),
            const((1, 2 * d)),
            const((D_HEADS, d)),
            const((D_HEADS, d)),
            const((d, d)),
            pl.BlockSpec((tm, d), lambda t: (t, 0)),
            const((1, d)),
            const((d, LANES)),
            const((d, LANES)),
            const((1, LANES)),
        ],
        out_specs=[
            pl.BlockSpec((tm, d), lambda t: (t, 0)),
            pl.BlockSpec((tm, d // 2), lambda t: (t, 0)),
            pl.BlockSpec((tm, LANES), lambda t: (t, 0)),
            pl.BlockSpec((tm, LANES), lambda t: (t, 0)),
            pl.BlockSpec((1, LANES), lambda t: (0, 0)),
        ],
        out_shape=[
            jax.ShapeDtypeStruct((s, d), F32),
            jax.ShapeDtypeStruct((s, d // 2), U32),
            jax.ShapeDtypeStruct((s, LANES), I32),
            jax.ShapeDtypeStruct((s, LANES), F32),
            jax.ShapeDtypeStruct((1, LANES), F32),
        ],
        scratch_shapes=[pltpu.VMEM((1, LANES), F32)],
        compiler_params=_params(("arbitrary",)),
        name="mix_route",
    )(o_sb, o_fx, p, p, b_gate, w_pa, w_pb, w_out, x, g_ffn, wr_hi, wr_lo, br)


def _dispatch_kernel(slot_ref, cnt_ref, hp_ref, xs_ref, zero_scr, sem, zsem, *, tm, ch, cap):
    t = pl.program_id(0)
    base = t * tm

    def row_copy(r, k):
        return pltpu.make_async_copy(
            hp_ref.at[pl.ds(r, 1), :],
            xs_ref.at[pl.ds(slot_ref[(base + r) * TOP_K + k], 1), :],
            sem)

    def start(r, c):
        for k in range(TOP_K):
            row_copy(r, k).start()
        return c

    def wait(r, c):
        for k in range(TOP_K):
            row_copy(r, k).wait()
        return c

    @pl.when(t == 0)
    def _():
        zero_scr[...] = jnp.zeros_like(zero_scr)

        def pad_copy(e):
            first = pl.multiple_of(e * cap + (cnt_ref[e] // SUBLANES) * SUBLANES, SUBLANES)
            return pltpu.make_async_copy(zero_scr, xs_ref.at[pl.ds(first, ch), :], zsem)

        def zstart(e, c):
            pad_copy(e).start()
            return c

        def zwait(e, c):
            pad_copy(e).wait()
            return c

        lax.fori_loop(0, N_EXPERTS, zstart, 0)
        lax.fori_loop(0, N_EXPERTS, zwait, 0)

    lax.fori_loop(0, tm, start, 0, unroll=ROW_DMA_UNROLL)
    lax.fori_loop(0, tm, wait, 0, unroll=ROW_DMA_UNROLL)


def _dispatch(slots, counts, hp, cap, ch):
    s, half = hp.shape
    tm = _pick(s, (512, 256, 128))
    kern = functools.partial(_dispatch_kernel, tm=tm, ch=ch, cap=cap)
    return pl.pallas_call(
        kern,
        grid_spec=pltpu.PrefetchScalarGridSpec(
            num_scalar_prefetch=2,
            grid=(s // tm,),
            in_specs=[pl.BlockSpec((tm, half), lambda t, sl, cn: (t, 0))],
            out_specs=pl.BlockSpec(memory_space=pl.ANY),
            scratch_shapes=[
                pltpu.VMEM((ch, half), U32),
                pltpu.SemaphoreType.DMA(()),
                pltpu.SemaphoreType.DMA(()),
            ],
        ),
        out_shape=jax.ShapeDtypeStruct((N_EXPERTS * cap, half), U32),
        compiler_params=_params(("arbitrary",)),
        name="dispatch",
    )(slots, counts, hp)


def _expert_kernel(sbe_ref, sbblk_ref, sbnch_ref, x_ref, w1a_ref, w1c_ref, b1_ref, w2a_ref, w2c_ref,
                   b2_ref, y_ref, xb_scr, w1b_scr, w2b_scr, *, ch, ff):
    s = pl.program_id(0)
    j = pl.program_id(1)
    nch = sbnch_ref[s]
    half = LANES // 2

    def half_act(g):
        lin = pltpu.roll(g, LANES - 1, 1)
        gate = jnp.minimum(g, SWIGLU_LIMIT)
        lin = jnp.clip(lin, -SWIGLU_LIMIT, SWIGLU_LIMIT)
        return gate * jax.nn.sigmoid(SWIGLU_ALPHA * gate) * (lin + 1.0)

    @pl.when(nch > 0)
    def _():
        @pl.when(j == 0)
        def _():
            xb_scr[...] = _bf16_pair_unpack(x_ref[...])

            def init(c, carry):
                y_ref[pl.ds(pl.multiple_of(c * ch, ch), ch), :] = jnp.broadcast_to(
                    b2_ref[...], (ch, y_ref.shape[1]))
                return carry

            lax.fori_loop(0, nch, init, 0)

        w1b_scr[:, :ff] = w1a_ref[...].astype(BF16)
        w1b_scr[:, ff:] = w1c_ref[...].astype(BF16)
        prow = lax.broadcasted_iota(I32, (LANES, LANES), 0)
        pcol = lax.broadcasted_iota(I32, (LANES, LANES), 1)
        perm = (pcol == prow // 2 + half * (prow % 2)).astype(BF16)
        groups = ff // LANES
        for u in range(groups):
            src_ref, v = (w2a_ref, u) if u < groups // 2 else (w2c_ref, u - groups // 2)
            grp = src_ref[pl.ds(v * LANES, LANES), :].astype(BF16)
            w2b_scr[pl.ds(u * LANES, LANES), :] = jnp.dot(
                perm, grp, preferred_element_type=F32).astype(BF16)

        def up(r0, rows):
            gu = jnp.dot(xb_scr[pl.ds(r0, rows), :], w1b_scr[...],
                         preferred_element_type=F32) + b1_ref[...]
            even = (lax.broadcasted_iota(I32, (rows, LANES), 1) % 2) == 0
            pieces = []
            for u in range(ff // LANES):
                a = half_act(gu[:, (2 * u) * LANES:(2 * u + 1) * LANES])
                b = half_act(gu[:, (2 * u + 1) * LANES:(2 * u + 2) * LANES])
                pieces.append(jnp.where(even, a, pltpu.roll(b, 1, 1)))
            return jnp.concatenate(pieces, axis=1).astype(BF16)

        def down(r0, rows, act):
            y_ref[pl.ds(r0, rows), :] += jnp.dot(act, w2b_scr[...], preferred_element_type=F32)

        big = 2 * ch
        nbig = nch // 2

        @pl.when(nbig > 0)
        def _():
            def trip(c, act):
                nxt = up(pl.multiple_of(c * big, big), big)
                down(pl.multiple_of((c - 1) * big, big), big, act)
                return nxt

            act = lax.fori_loop(1, nbig, trip, up(0, big))
            down(pl.multiple_of((nbig - 1) * big, big), big, act)

        @pl.when(nch % 2 == 1)
        def _():
            r0 = pl.multiple_of(nbig * big, ch)
            down(r0, ch, up(r0, ch))


def _experts(sb_e, sb_blk, sb_nch, xs, w1, b1, w2, b2, r_rows, ch, cap):
    ne, d, ff2 = w1.shape
    dff = ff2 // 2
    ff = _pick(dff, (256, 128))
    nj = dff // ff
    ns = sb_e.shape[0]
    half = d // 2
    blocks_per_expert = cap // r_rows

    def jmap(s, j, nch):
        return jnp.where(nch[s] > 0, j, nj - 1)

    kern = functools.partial(_expert_kernel, ch=ch, ff=ff)
    return pl.pallas_call(
        kern,
        grid_spec=pltpu.PrefetchScalarGridSpec(
            num_scalar_prefetch=3,
            grid=(ns, nj),
            in_specs=[
                pl.BlockSpec((r_rows, half),
                             lambda s, j, e, b, n: (e[s] * blocks_per_expert + b[s], 0)),
                pl.BlockSpec((None, d, ff), lambda s, j, e, b, n: (e[s], 0, 2 * jmap(s, j, n))),
                pl.BlockSpec((None, d, ff), lambda s, j, e, b, n: (e[s], 0, 2 * jmap(s, j, n) + 1)),
                pl.BlockSpec((None, 1, 2 * ff), lambda s, j, e, b, n: (e[s], 0, jmap(s, j, n))),
                pl.BlockSpec((None, ff // 2, d), lambda s, j, e, b, n: (e[s], 2 * jmap(s, j, n), 0)),
                pl.BlockSpec((None, ff // 2, d),
                             lambda s, j, e, b, n: (e[s], 2 * jmap(s, j, n) + 1, 0)),
                pl.BlockSpec((None, 1, d), lambda s, j, e, b, n: (e[s], 0, 0)),
            ],
            out_specs=pl.BlockSpec((r_rows, d),
                                   lambda s, j, e, b, n: (e[s] * blocks_per_expert + b[s], 0)),
            scratch_shapes=[
                pltpu.VMEM((r_rows, d), BF16),
                pltpu.VMEM((d, 2 * ff), BF16),
                pltpu.VMEM((ff, d), BF16),
            ],
        ),
        out_shape=jax.ShapeDtypeStruct((ne * cap, d), F32),
        compiler_params=_params(("arbitrary", "arbitrary")),
        name="experts",
    )(sb_e, sb_blk, sb_nch, xs, w1, w1, b1.reshape(ne, 1, ff2), w2, w2, b2.reshape(ne, 1, d))


def _combine_kernel(slot_ref, x2_ref, rw_ref, g_ref, ys_ref, o_ref, gbuf, sem, *, tm, final_norm):
    t = pl.program_id(0)
    base = t * tm

    def row_copy(r, k):
        return pltpu.make_async_copy(
            ys_ref.at[pl.ds(slot_ref[(base + r) * TOP_K + k], 1), :],
            gbuf.at[pl.ds(k * tm + r, 1), :],
            sem)

    def start(r, c):
        for k in range(TOP_K):
            row_copy(r, k).start()
        return c

    def wait(r, c):
        for k in range(TOP_K):
            row_copy(r, k).wait()
        return c

    lax.fori_loop(0, tm, start, 0, unroll=ROW_DMA_UNROLL)
    lax.fori_loop(0, tm, wait, 0, unroll=ROW_DMA_UNROLL)

    rw = rw_ref[...]
    out = x2_ref[...]
    for k in range(TOP_K):
        out = out + rw[:, k:k + 1] * gbuf[pl.ds(k * tm, tm), :]
    if final_norm:
        out = _rms(out, g_ref[...])
    o_ref[...] = out


def _combine(slots, x2, rw, g, ys, final_norm):
    s, d = x2.shape
    tm = _pick(s, (256, 128))
    kern = functools.partial(_combine_kernel, tm=tm, final_norm=final_norm)
    return pl.pallas_call(
        kern,
        grid_spec=pltpu.PrefetchScalarGridSpec(
            num_scalar_prefetch=1,
            grid=(s // tm,),
            in_specs=[
                pl.BlockSpec((tm, d), lambda t, sl: (t, 0)),
                pl.BlockSpec((tm, LANES), lambda t, sl: (t, 0)),
                pl.BlockSpec((1, d), lambda t, sl: (0, 0)),
                pl.BlockSpec(memory_space=pl.ANY),
            ],
            out_specs=pl.BlockSpec((tm, d), lambda t, sl: (t, 0)),
            scratch_shapes=[pltpu.VMEM((TOP_K * tm, d), F32), pltpu.SemaphoreType.DMA(())],
        ),
        out_shape=jax.ShapeDtypeStruct((s, d), F32),
        compiler_params=_params(("arbitrary",)),
        name="combine",
    )(slots, x2, rw, g, ys)


def _superblocks(counts, r_rows, ch, ns):
    nsb = (counts + r_rows - 1) // r_rows
    ends = jnp.cumsum(nsb)
    total = ends[-1]
    sidx = jnp.arange(ns, dtype=I32)
    last = jnp.maximum(total - 1, 0)
    sclip = jnp.minimum(sidx, last)
    e = jnp.minimum(jnp.searchsorted(ends, sclip, side="right"), N_EXPERTS - 1).astype(I32)
    blk = sclip - (ends[e] - nsb[e])
    rows = jnp.clip(counts[e] - blk * r_rows, 0, r_rows)
    nch = jnp.where(sidx < total, (rows + ch - 1) // ch, 0)
    return e, blk.astype(I32), nch.astype(I32)


def _layer(x, g_mix, w_in, b_forget, b_gate, w_pa, w_pb, w_out, g_ffn,
           w_router, b_router, w1, b1, w2, b2, g_out, final_norm):
    s, d = x.shape
    qkv = 6 * D_HEADS

    w_main = _w_prep(w_in, qkv, N_HEADS)
    w_f = jnp.pad(w_in[:, qkv:qkv + N_HEADS], ((0, 0), (0, LANES - N_HEADS))).astype(BF16)
    p, f_log, vt_sb, vt_fx = _in_proj(x, g_mix.reshape(1, d), w_main, w_f)

    log_f = jax.nn.log_sigmoid(f_log[:, :N_HEADS] + b_forget)
    cum_f = jnp.cumsum(log_f, axis=0).T

    nb = D_HEADS // HEAD_DIM
    o_sb = _attention(p, vt_sb, 0, nb)
    o_fx = _attention(p, vt_fx, 3 * nb, 4 * nb, fox=(_fox_tables(cum_f), *_fox_bias(cum_f)))

    r_rows = 1280
    ch = 256
    cap = -(-(s + ch) // r_rows) * r_rows
    ns =(s * TOP_K) // r_rows + N_EXPERTS
    wr = jnp.pad(w_router, ((0, 0), (0, LANES - N_EXPERTS)))
    wr_top = _bf16_trunc(wr)
    wr_hi = wr_top.astype(BF16)
    wr_lo = (wr - wr_top).astype(BF16)
    br = jnp.pad(b_router, (0, LANES - N_EXPERTS), constant_values=NEG_BIG).reshape(1, LANES)
    x2, hp, ri, rw, cnt = _mix_route(
        o_sb, o_fx, p, qkv // d, b_gate.reshape(1, 2 * d), w_pa.astype(BF16), w_pb.astype(BF16),
        w_out.astype(BF16), x, g_ffn.reshape(1, d), wr_hi, wr_lo, br, cap)

    slots = ri[:, :TOP_K].reshape(-1)
    counts = cnt[0, :N_EXPERTS].astype(I32)
    xs = _dispatch(slots, counts, hp, cap, ch)
    sb_e, sb_blk, sb_nch = _superblocks(counts, r_rows, ch, ns)
    ys = _experts(sb_e, sb_blk, sb_nch, xs, w1, b1, w2, b2, r_rows, ch, cap)
    return _combine(slots, x2, rw, g_out.reshape(1, d), ys, final_norm)


def kernel(x, g_mix, w_in, b_forget, b_gate, w_proj_sb, w_proj_fox, w_out, g_ffn, w_router,
           b_router, w_mlp1, b_mlp1, w_mlp2, b_mlp2, g_final):
    b, s, d = x.shape
    depth = g_mix.shape[0]
    outs = []
    for bi in range(b):
        xb = x[bi]
        for layer in range(depth):
            last = layer == depth - 1
            xb = _layer(xb, g_mix[layer], w_in[layer], b_forget[layer], b_gate[layer],
                        w_proj_sb[layer], w_proj_fox[layer], w_out[layer], g_ffn[layer],
                        w_router[layer], b_router[layer], w_mlp1[layer], b_mlp1[layer],
                        w_mlp2[layer], b_mlp2[layer], g_final, last)
        outs.append(xb.reshape(1, s, d))
    return outs[0] if b == 1 else jnp.concatenate(outs, axis=0)
```

```python
import functools

import jax
import jax.numpy as jnp
from jax import lax
from jax.experimental import pallas as pl
from jax.experimental.pallas import tpu as pltpu

F32 = jnp.float32
BF16 = jnp.bfloat16
U32 = jnp.uint32
I32 = jnp.int32

LANES = 128
SUBLANES = 8
HEAD_DIM = 128
N_HEADS = 8
D_HEADS = N_HEADS * HEAD_DIM
N_EXPERTS = 32
TOP_K = 4
SWIGLU_LIMIT = 7.0
SWIGLU_ALPHA = 1.702
RMS_EPS = 1e-5
NEG_BIG = -1e30
SB_VALUE_TILE = 256
ROW_DMA_UNROLL = 8
LOG2_E = 1.4426950408889634
BF16_ROWS = 16
F32_EXP2_UNDERFLOW = 152.0
NORM_SLACK = 1.001
F32_EXP_UNDERFLOW = 105.0
VMEM_LIMIT = 56 * 1024 * 1024


def _pick(n, candidates):
    for c in candidates:
        if n % c == 0:
            return c
    raise ValueError(f"no tile in {candidates} divides {n}")


def _params(sem, vmem=VMEM_LIMIT):
    return pltpu.CompilerParams(dimension_semantics=sem, vmem_limit_bytes=vmem)


def _rms(x, g):
    ms = jnp.mean(x * x, axis=-1, keepdims=True)
    return x * lax.rsqrt(ms + RMS_EPS) * g


def _inproj_kernel(x_ref, g_ref, w_ref, wf_ref, p_ref, f_ref, vta_ref, vtb_ref, h_scr, *, tn):
    n = pl.program_id(1)

    @pl.when(n == 0)
    def _():
        hb = _rms(x_ref[...], g_ref[...]).astype(BF16)
        h_scr[...] = hb
        wf = jnp.concatenate(
            [wf_ref[...], jnp.zeros((LANES - SUBLANES, wf_ref.shape[1]), F32)], axis=0).astype(BF16)
        f_ref[...] = lax.dot_general(hb, wf, _NT, preferred_element_type=F32)

    res = jnp.dot(h_scr[...], w_ref[...], preferred_element_type=F32)
    p_ref[...] = res.astype(BF16)

    heads_per_tile = tn // HEAD_DIM
    for section, vt_ref in ((2, vta_ref), (5, vtb_ref)):
        for tile in range(section * D_HEADS // tn, (section + 1) * D_HEADS // tn):
            @pl.when(n == tile)
            def _(vt_ref=vt_ref, head0=(tile * tn - section * D_HEADS) // HEAD_DIM):
                tv = vt_ref.shape[3]
                for hh in range(heads_per_tile):
                    vt = res[:, hh * HEAD_DIM:(hh + 1) * HEAD_DIM].T.astype(BF16)
                    for c in range(vt.shape[1] // tv):
                        vt_ref[head0 + hh, c, pl.ds(0, HEAD_DIM), :] = vt[:, c * tv:(c + 1) * tv]
                        if vt_ref.shape[2] > HEAD_DIM:
                            extra = lax.broadcasted_iota(I32, (BF16_ROWS, tv), 0) == 0
                            vt_ref[head0 + hh, c, pl.ds(HEAD_DIM, BF16_ROWS), :] = extra.astype(BF16)


def _wprep_kernel(a_ref, b_ref, o_ref, *, first_tail):
    n = pl.program_id(1)

    @pl.when(n < first_tail)
    def _():
        o_ref[...] = a_ref[...].T.astype(BF16)

    @pl.when(n >= first_tail)
    def _():
        ext = jnp.concatenate([a_ref[...], b_ref[...]], axis=0)
        o_ref[...] = ext[SUBLANES:, :].T.astype(BF16)


def _w_prep(w_t, qkv):
    n_in, d = w_t.shape
    n_out = n_in - SUBLANES
    tn = _pick(n_out, (1024, 512, 256, 128))
    tr = _pick(d, (512, 256, 128))
    per = tn // SUBLANES
    kern = functools.partial(_wprep_kernel, first_tail=qkv // tn)
    return pl.pallas_call(
        kern,
        grid=(d // tr, n_out // tn),
        in_specs=[
            pl.BlockSpec((tn, tr), lambda r, n: (n, r)),
            pl.BlockSpec((SUBLANES, tr), lambda r, n: ((n + 1) * per, r)),
        ],
        out_specs=pl.BlockSpec((tr, tn), lambda r, n: (r, n)),
        out_shape=jax.ShapeDtypeStruct((d, n_out), BF16),
        compiler_params=_params(("parallel", "parallel")),
        name="w_prep",
    )(w_t, w_t)


def _in_proj(x, g, w_main, w_t, f_row):
    s, d = x.shape
    n = w_main.shape[1]
    tm = _pick(s, (1024, 512, 256, 128))
    tn = _pick(n, (1024, 512, 256, 128))
    tq = _attn_tile(s)
    rows_b = HEAD_DIM + BF16_ROWS
    tv_a = min(tq, SB_VALUE_TILE)
    vt_spec = lambda rows, tv: pl.BlockSpec((N_HEADS, tm // tv, rows, tv), lambda m, j: (0, m, 0, 0))
    return pl.pallas_call(
        functools.partial(_inproj_kernel, tn=tn),
        grid=(s // tm, n // tn),
        in_specs=[
            pl.BlockSpec((tm, d), lambda m, j: (m, 0)),
            pl.BlockSpec((1, d), lambda m, j: (0, 0)),
            pl.BlockSpec((d, tn), lambda m, j: (0, j)),
            pl.BlockSpec((SUBLANES, d), lambda m, j: (f_row // SUBLANES, 0)),
        ],
        out_specs=[
            pl.BlockSpec((tm, tn), lambda m, j: (m, j)),
            pl.BlockSpec((tm, LANES), lambda m, j: (m, 0)),
            vt_spec(HEAD_DIM, tv_a),
            vt_spec(rows_b, tq),
        ],
        out_shape=[
            jax.ShapeDtypeStruct((s, n), BF16),
            jax.ShapeDtypeStruct((s, LANES), F32),
            jax.ShapeDtypeStruct((N_HEADS, s // tv_a, HEAD_DIM, tv_a), BF16),
            jax.ShapeDtypeStruct((N_HEADS, s // tq, rows_b, tq), BF16),
        ],
        scratch_shapes=[pltpu.VMEM((tm, d), BF16)],
        compiler_params=_params(("parallel", "arbitrary")),
        name="in_proj",
    )(x, g, w_main, w_t)


_NT = (((1,), (1,)), ((), ()))


def _sb_kernel(q_ref, k_ref, vt_ref, o_ref, acc_scr, *, tq, tk, tv, scale):
    i = pl.program_id(1)
    nsub = tq // tk
    per_tv = tv // tk
    q = q_ref[...]
    row = lax.broadcasted_iota(I32, (tk, 2 * tk), 0)
    col = lax.broadcasted_iota(I32, (tk, 2 * tk), 1)
    ut2 = ((col % tk) >= row).astype(BF16)
    diff = lax.broadcasted_iota(I32, (tk, tq), 1) - lax.broadcasted_iota(I32, (tk, tq), 0)

    def sub_block(kb, qs, csum, mask):
        k = k_ref[pl.ds(pl.multiple_of(kb * tk, tk), tk), :]
        z = lax.dot_general(k, qs, _NT, preferred_element_type=F32) * scale
        lk = -(jnp.maximum(z, 0.0) + jnp.log(1.0 + jnp.exp(-jnp.abs(z))))
        if mask is not None:
            lk = jnp.where(mask, lk, 0.0)
        hi = lk.astype(BF16)
        lo = (lk - hi.astype(F32)).astype(BF16)
        cs = jnp.dot(ut2, jnp.concatenate([hi, lo], axis=0), preferred_element_type=F32)
        a = jnp.exp(z + cs + csum)
        if mask is not None:
            a = jnp.where(mask, a, 0.0)
        return a.astype(BF16), cs[0:1, :]

    acc_scr[...] = jnp.zeros_like(acc_scr)
    csum = jnp.zeros((1, tq), F32)
    for u in reversed(range(nsub)):
        lo = u * tk
        a, inc = sub_block(i * nsub + u, q[lo:, :], csum[:, lo:], diff[:, lo:] > lo)
        vt = vt_ref[i * (tq // tv) + u // per_tv, :, pl.ds((u % per_tv) * tk, tk)]
        acc_scr[:, lo:] += jnp.dot(vt, a, preferred_element_type=F32)
        csum = jnp.concatenate([csum[:, :lo], csum[:, lo:] + inc], axis=1) if lo else csum + inc

    def far(g, csum):
        parts = [None] * per_tv
        for u in reversed(range(per_tv)):
            parts[u], inc = sub_block(g * per_tv + u, q, csum, None)
            csum = csum + inc
        acc_scr[...] += jnp.dot(vt_ref[g], jnp.concatenate(parts, axis=0),
                                preferred_element_type=F32)
        return csum

    first_far = i * (tq // tv) - 1

    def live(state):
        n, c = state
        return jnp.logical_and(n <= first_far, jnp.max(c) > -F32_EXP_UNDERFLOW)

    lax.while_loop(live, lambda st: (st[0] + 1, far(first_far - st[0], st[1])),
                   (jnp.int32(0), csum))
    o_ref[...] = acc_scr[...].T.astype(o_ref.dtype)


def _fox_kernel(ff_ref, fl_ref, q_ref, qa_ref, k_ref, ka_ref, vt_ref, o_ref,
                acc_scr, s_scr, kn_smem, *, tq, tk, scale):
    i = pl.program_id(1)
    nsub = tq // tk
    q = jnp.concatenate([q_ref[...], qa_ref[...]], axis=1)
    diff = lax.broadcasted_iota(I32, (tk, tq), 1) - lax.broadcasted_iota(I32, (tk, tq), 0)
    to_log2 = scale * LOG2_E

    def scores(t, slot, masked):
        for u in range(nsub):
            ks = pl.multiple_of((t * nsub + u) * tk, tk)
            k = jnp.concatenate([k_ref[pl.ds(ks, tk), :], ka_ref[pl.ds(ks, tk), :]], axis=1)
            s = lax.dot_general(k, q, _NT, preferred_element_type=F32) * to_log2
            if masked:
                s = jnp.where(diff >= u * tk, s, NEG_BIG)
            s_scr[slot, pl.ds(u * tk, tk), :] = s

    def consume(t, slot, m):
        top = s_scr[slot, pl.ds(0, tk), :]
        for u in range(1, nsub):
            top = jnp.maximum(top, s_scr[slot, pl.ds(u * tk, tk), :])
        m_new = jnp.maximum(m, jnp.max(top, axis=0, keepdims=True))
        alpha = jnp.exp2(m - m_new)
        pcat = jnp.exp2(s_scr[slot] - m_new).astype(BF16)
        acc_scr[...] = alpha * acc_scr[...] + jnp.dot(vt_ref[t], pcat, preferred_element_type=F32)
        return m_new

    def step(t, slot, m):
        scores(t - 1, 1 - slot, False)
        return consume(t, slot, m)

    def max_norm(x):
        xf = x.astype(F32)
        return jnp.max(jnp.sqrt(jnp.sum(xf * xf, axis=1, keepdims=True)))

    kn_prev = jnp.where(i > 0, kn_smem[jnp.maximum(i - 1, 0)], 0.0)
    kn_smem[i] = jnp.maximum(kn_prev, max_norm(k_ref[pl.ds(pl.multiple_of(i * tq, tq), tq), :]))

    acc_scr[...] = jnp.zeros_like(acc_scr)
    scores(i, 0, True)

    @pl.when(i > 0)
    def _():
        scores(i - 1, 1, False)

    m = consume(i, 0, jnp.full((1, tq), NEG_BIG, F32))

    @pl.when(i > 0)
    def _():
        nt = pl.num_programs(1)
        row = pl.program_id(0) * nt
        floor = jnp.min(m) - F32_EXP2_UNDERFLOW
        qk = to_log2 * NORM_SLACK * max_norm(q_ref[...])
        f_first = LOG2_E * ff_ref[row + i]

        def live(extra):
            t = i - 2 - extra
            bound = qk * kn_smem[jnp.maximum(t, 0)] + (
                f_first - LOG2_E * fl_ref[row + jnp.maximum(t, 0)])
            return jnp.logical_and(t >= 0, bound >= floor)

        extra = lax.while_loop(live, lambda e: e + 1, jnp.int32(0))
        last = i - 1 - extra

        def trip(k, m):
            t = i - 1 - 2 * k
            return step(t - 1, 0, step(t, 1, m))

        m2 = lax.fori_loop(0, extra // 2, trip, m)

        @pl.when(extra % 2 == 0)
        def _():
            consume(last, 1, m2)

        @pl.when(extra % 2 == 1)
        def _():
            consume(last, 0, step(last + 1, 1, m2))

    acc = acc_scr[...]
    o_ref[...] = (acc[:HEAD_DIM] / acc[HEAD_DIM:HEAD_DIM + 1]).T.astype(o_ref.dtype)


def _attn_tile(s):
    return _pick(s, (512, 256, 128))


def _attention(p, vt, q_col, k_col, fox=None):
    s = p.shape[0]
    tq = _attn_tile(s)
    tk = 128
    scale = HEAD_DIM ** -0.5
    rows = vt.shape[2]
    q_spec = pl.BlockSpec((tq, HEAD_DIM), lambda h, i, *_: (i, q_col + h))
    k_spec = pl.BlockSpec((s, HEAD_DIM), lambda h, i, *_: (0, k_col + h))
    vt_spec = pl.BlockSpec((None,) + vt.shape[1:], lambda h, i, *_: (h, 0, 0, 0))
    scratch = [pltpu.VMEM((rows, tq), F32)]
    order = ("parallel", "arbitrary")
    if fox is None:
        kern = functools.partial(_sb_kernel, tq=tq, tk=tk, tv=vt.shape[3], scale=scale)
        name = "sb_attn"
        tables = []
        in_specs = [q_spec, k_spec, vt_spec]
        args = [p, p, vt]
    else:
        tables, qa, ka = fox
        scratch.append(pltpu.VMEM((2, tq, tq), F32))
        scratch.append(pltpu.SMEM((s // tq,), F32))
        order = ("arbitrary", "arbitrary")
        kern = functools.partial(_fox_kernel, tq=tq, tk=tk, scale=scale)
        name = "fox_attn"
        in_specs = [
            q_spec,
            pl.BlockSpec((None, tq, HEAD_DIM), lambda h, i, *_: (h, i, 0)),
            k_spec,
            pl.BlockSpec((None, s, HEAD_DIM), lambda h, i, *_: (h, 0, 0)),
            vt_spec,
        ]
        args = [p, qa, p, ka, vt]
    return pl.pallas_call(
        kern,
        grid_spec=pltpu.PrefetchScalarGridSpec(
            num_scalar_prefetch=len(tables),
            grid=(N_HEADS, s // tq),
            in_specs=in_specs,
            out_specs=pl.BlockSpec((tq, HEAD_DIM), lambda h, i, *_: (i, h)),
            scratch_shapes=scratch,
        ),
        out_shape=jax.ShapeDtypeStruct((s, D_HEADS), BF16),
        compiler_params=_params(order),
        name=name,
    )(*tables, *args)


def _fox_tables(cum_f):
    nt = cum_f.shape[1] // _attn_tile(cum_f.shape[1])
    f_tiles = cum_f.reshape(N_HEADS, nt, -1)
    return [f_tiles[:, :, 0].reshape(-1), f_tiles[:, :, -1].reshape(-1)]


def _bf16_trunc(x):
    bits = lax.bitcast_convert_type(x, U32) & jnp.uint32(0xFFFF0000)
    return lax.bitcast_convert_type(bits, F32)


def _split3(x):
    a = _bf16_trunc(x)
    b = _bf16_trunc(x - a)
    c = _bf16_trunc(x - a - b)
    return [a.astype(BF16), b.astype(BF16), c.astype(BF16)]


def _fox_bias(cum_f):
    parts = _split3(cum_f * (HEAD_DIM ** 0.5))
    ones = jnp.ones_like(parts[0])
    zeros = jnp.zeros_like(parts[0])
    widen = jnp.eye(SUBLANES, HEAD_DIM, dtype=BF16)

    def spread(cols):
        packed = jnp.stack(cols + [zeros, zeros], axis=-1)
        return jnp.einsum("hsc,cl->hsl", packed, widen,
                          preferred_element_type=F32).astype(BF16)

    return spread([ones] * 3 + parts), spread([-t for t in parts] + [ones] * 3)


def _bf16_pair_pack(h):
    bits = lax.bitcast_convert_type(h, U32)
    rnd = bits + jnp.uint32(0x7FFF) + ((bits >> 16) & jnp.uint32(1))
    half = h.shape[1] // 2
    return (rnd[:, half:] & jnp.uint32(0xFFFF0000)) | (rnd[:, :half] >> 16)


def _bf16_pair_unpack(w):
    lo = lax.bitcast_convert_type(w << 16, F32)
    hi = lax.bitcast_convert_type(w & jnp.uint32(0xFFFF0000), F32)
    return jnp.concatenate([lo, hi], axis=1).astype(BF16)


def _mix_kernel(osb_ref, ofx_ref, ga_ref, gb_ref, bg_ref, wpa_ref, wpb_ref, wo_ref, x_ref, g_ref,
                wrh_ref, wrl_ref, br_ref,
                x2_ref, hp_ref, ri_ref, rw_ref, cnt_ref, cnt_scr, *, tm, d, cap):
    t = pl.program_id(0)

    @pl.when(t == 0)
    def _():
        cnt_scr[...] = jnp.zeros_like(cnt_scr)

    bg = bg_ref[...]
    pa = jnp.dot(osb_ref[...], wpa_ref[...], preferred_element_type=F32)
    pb = jnp.dot(ofx_ref[...], wpb_ref[...], preferred_element_type=F32)
    ga = jax.nn.sigmoid(ga_ref[...].astype(F32) + bg[:, :d])
    gb = jax.nn.sigmoid(gb_ref[...].astype(F32) + bg[:, d:])
    mixed = (ga * pa + gb * pb).astype(BF16)
    x2 = x_ref[...] + jnp.dot(mixed, wo_ref[...], preferred_element_type=F32)
    x2_ref[...] = x2
    h2 = _rms(x2, g_ref[...])
    hp_ref[...] = _bf16_pair_pack(h2)

    hh = h2.astype(BF16)
    hl = (h2 - hh.astype(F32)).astype(BF16)
    logits = (jnp.dot(hh, wrh_ref[...], preferred_element_type=F32)
              + jnp.dot(hh, wrl_ref[...], preferred_element_type=F32)
              + jnp.dot(hl, wrh_ref[...], preferred_element_type=F32)
              + br_ref[...])

    lane = lax.broadcasted_iota(I32, (tm, LANES), 1).astype(F32)
    work = logits
    sel = jnp.zeros((tm, LANES), F32)
    vals, idxs = [], []
    for _ in range(TOP_K):
        mx = jnp.max(work, axis=1, keepdims=True)
        idx = jnp.min(jnp.where(work == mx, lane, float(LANES)), axis=1, keepdims=True)
        hit = lane == idx
        vals.append(mx)
        idxs.append(idx)
        work = jnp.where(hit, -jnp.inf, work)
        sel = jnp.where(hit, 1.0, sel)
    exps = [jnp.exp(v - vals[0]) for v in vals]
    denom = exps[0] + exps[1] + exps[2] + exps[3]

    row = lax.broadcasted_iota(I32, (tm, tm), 0)
    col = lax.broadcasted_iota(I32, (tm, tm), 1)
    tri = (col < row).astype(BF16)
    rank = jnp.dot(tri, sel.astype(BF16), preferred_element_type=F32) + cnt_scr[...]
    cnt_new = cnt_scr[...] + jnp.sum(sel, axis=0, keepdims=True)
    cnt_scr[...] = cnt_new
    cnt_ref[...] = cnt_new
    slot = rank + lane * float(cap)
    ri = jnp.zeros((tm, LANES), F32)
    rw = jnp.zeros((tm, LANES), F32)
    for r in range(TOP_K):
        slot_r = jnp.sum(jnp.where(lane == idxs[r], slot, 0.0), axis=1, keepdims=True)
        ri = jnp.where(lane == float(r), slot_r, ri)
        rw = jnp.where(lane == float(r), exps[r] / denom, rw)
    ri_ref[...] = ri.astype(I32)
    rw_ref[...] = rw


def _mix_route(o_sb, o_fx, p, gate_col, b_gate, w_pa, w_pb, w_out, x, g_ffn, wr_hi, wr_lo, br, cap):
    s, d = x.shape
    tm = _pick(s, (256, 128))
    const = lambda shape: pl.BlockSpec(shape, lambda t: (0, 0), pipeline_mode=pl.Buffered(1))
    kern = functools.partial(_mix_kernel, tm=tm, d=d, cap=cap)
    return pl.pallas_call(
        kern,
        grid=(s // tm,),
        in_specs=[
            pl.BlockSpec((tm, D_HEADS), lambda t: (t, 0)),
            pl.BlockSpec((tm, D_HEADS), lambda t: (t, 0)),
            pl.BlockSpec((tm, d), lambda t: (t, gate_col)),
            pl.BlockSpec((tm, d), lambda t: (t, gate_col + 1)),
            const((1, 2 * d)),
            const((D_HEADS, d)),
            const((D_HEADS, d)),
            const((d, d)),
            pl.BlockSpec((tm, d), lambda t: (t, 0)),
            const((1, d)),
            const((d, LANES)),
            const((d, LANES)),
            const((1, LANES)),
        ],
        out_specs=[
            pl.BlockSpec((tm, d), lambda t: (t, 0)),
            pl.BlockSpec((tm, d // 2), lambda t: (t, 0)),
            pl.BlockSpec((tm, LANES), lambda t: (t, 0)),
            pl.BlockSpec((tm, LANES), lambda t: (t, 0)),
            pl.BlockSpec((1, LANES), lambda t: (0, 0)),
        ],
        out_shape=[
            jax.ShapeDtypeStruct((s, d), F32),
            jax.ShapeDtypeStruct((s, d // 2), U32),
            jax.ShapeDtypeStruct((s, LANES), I32),
            jax.ShapeDtypeStruct((s, LANES), F32),
            jax.ShapeDtypeStruct((1, LANES), F32),
        ],
        scratch_shapes=[pltpu.VMEM((1, LANES), F32)],
        compiler_params=_params(("arbitrary",)),
        name="mix_route",
    )(o_sb, o_fx, p, p, b_gate, w_pa, w_pb, w_out, x, g_ffn, wr_hi, wr_lo, br)


def _dispatch_kernel(slot_ref, cnt_ref, hp_ref, xs_ref, zero_scr, sem, zsem, *, tm, ch, cap):
    t = pl.program_id(0)
    base = t * tm

    def row_copy(r, k):
        return pltpu.make_async_copy(
            hp_ref.at[pl.ds(r, 1), :],
            xs_ref.at[pl.ds(slot_ref[(base + r) * TOP_K + k], 1), :],
            sem)

    def start(r, c):
        for k in range(TOP_K):
            row_copy(r, k).start()
        return c

    def wait(r, c):
        for k in range(TOP_K):
            row_copy(r, k).wait()
        return c

    @pl.when(t == 0)
    def _():
        zero_scr[...] = jnp.zeros_like(zero_scr)

        def pad_copy(e):
            first = pl.multiple_of(e * cap + (cnt_ref[e] // SUBLANES) * SUBLANES, SUBLANES)
            return pltpu.make_async_copy(zero_scr, xs_ref.at[pl.ds(first, ch), :], zsem)

        def zstart(e, c):
            pad_copy(e).start()
            return c

        def zwait(e, c):
            pad_copy(e).wait()
            return c

        lax.fori_loop(0, N_EXPERTS, zstart, 0)
        lax.fori_loop(0, N_EXPERTS, zwait, 0)

    lax.fori_loop(0, tm, start, 0, unroll=ROW_DMA_UNROLL)
    lax.fori_loop(0, tm, wait, 0, unroll=ROW_DMA_UNROLL)


def _dispatch(slots, counts, hp, cap, ch):
    s, half = hp.shape
    tm = _pick(s, (512, 256, 128))
    kern = functools.partial(_dispatch_kernel, tm=tm, ch=ch, cap=cap)
    return pl.pallas_call(
        kern,
        grid_spec=pltpu.PrefetchScalarGridSpec(
            num_scalar_prefetch=2,
            grid=(s // tm,),
            in_specs=[pl.BlockSpec((tm, half), lambda t, sl, cn: (t, 0))],
            out_specs=pl.BlockSpec(memory_space=pl.ANY),
            scratch_shapes=[
                pltpu.VMEM((ch, half), U32),
                pltpu.SemaphoreType.DMA(()),
                pltpu.SemaphoreType.DMA(()),
            ],
        ),
        out_shape=jax.ShapeDtypeStruct((N_EXPERTS * cap, half), U32),
        compiler_params=_params(("arbitrary",)),
        name="dispatch",
    )(slots, counts, hp)


def _expert_kernel(sbe_ref, sbblk_ref, sbnch_ref, x_ref, w1a_ref, w1c_ref, b1_ref, w2a_ref, w2c_ref,
                   b2_ref, y_ref, xb_scr, w1b_scr, w2b_scr, *, ch, ff):
    s = pl.program_id(0)
    j = pl.program_id(1)
    nch = sbnch_ref[s]
    half = LANES // 2

    def half_act(g):
        lin = pltpu.roll(g, LANES - 1, 1)
        gate = jnp.minimum(g, SWIGLU_LIMIT)
        lin = jnp.clip(lin, -SWIGLU_LIMIT, SWIGLU_LIMIT)
        return gate * jax.nn.sigmoid(SWIGLU_ALPHA * gate) * (lin + 1.0)

    @pl.when(nch > 0)
    def _():
        @pl.when(j == 0)
        def _():
            xb_scr[...] = _bf16_pair_unpack(x_ref[...])

            def init(c, carry):
                y_ref[pl.ds(pl.multiple_of(c * ch, ch), ch), :] = jnp.broadcast_to(
                    b2_ref[...], (ch, y_ref.shape[1]))
                return carry

            lax.fori_loop(0, nch, init, 0)

        w1b_scr[:, :ff] = w1a_ref[...].astype(BF16)
        w1b_scr[:, ff:] = w1c_ref[...].astype(BF16)
        prow = lax.broadcasted_iota(I32, (LANES, LANES), 0)
        pcol = lax.broadcasted_iota(I32, (LANES, LANES), 1)
        perm = (pcol == prow // 2 + half * (prow % 2)).astype(BF16)
        groups = ff // LANES
        for u in range(groups):
            src_ref, v = (w2a_ref, u) if u < groups // 2 else (w2c_ref, u - groups // 2)
            grp = src_ref[pl.ds(v * LANES, LANES), :].astype(BF16)
            w2b_scr[pl.ds(u * LANES, LANES), :] = jnp.dot(
                perm, grp, preferred_element_type=F32).astype(BF16)

        def up(r0, rows):
            gu = jnp.dot(xb_scr[pl.ds(r0, rows), :], w1b_scr[...],
                         preferred_element_type=F32) + b1_ref[...]
            even = (lax.broadcasted_iota(I32, (rows, LANES), 1) % 2) == 0
            pieces = []
            for u in range(ff // LANES):
                a = half_act(gu[:, (2 * u) * LANES:(2 * u + 1) * LANES])
                b = half_act(gu[:, (2 * u + 1) * LANES:(2 * u + 2) * LANES])
                pieces.append(jnp.where(even, a, pltpu.roll(b, 1, 1)))
            return jnp.concatenate(pieces, axis=1).astype(BF16)

        def down(r0, rows, act):
            y_ref[pl.ds(r0, rows), :] += jnp.dot(act, w2b_scr[...], preferred_element_type=F32)

        big = 2 * ch
        nbig = nch // 2

        @pl.when(nbig > 0)
        def _():
            def trip(c, act):
                nxt = up(pl.multiple_of(c * big, big), big)
                down(pl.multiple_of((c - 1) * big, big), big, act)
                return nxt

            act = lax.fori_loop(1, nbig, trip, up(0, big))
            down(pl.multiple_of((nbig - 1) * big, big), big, act)

        @pl.when(nch % 2 == 1)
        def _():
            r0 = pl.multiple_of(nbig * big, ch)
            down(r0, ch, up(r0, ch))


def _experts(sb_e, sb_blk, sb_nch, xs, w1, b1, w2, b2, r_rows, ch, cap):
    ne, d, ff2 = w1.shape
    dff = ff2 // 2
    ff = _pick(dff, (256, 128))
    nj = dff // ff
    ns = sb_e.shape[0]
    half = d // 2
    blocks_per_expert = cap // r_rows

    def jmap(s, j, nch):
        return jnp.where(nch[s] > 0, j, nj - 1)

    kern = functools.partial(_expert_kernel, ch=ch, ff=ff)
    return pl.pallas_call(
        kern,
        grid_spec=pltpu.PrefetchScalarGridSpec(
            num_scalar_prefetch=3,
            grid=(ns, nj),
            in_specs=[
                pl.BlockSpec((r_rows, half),
                             lambda s, j, e, b, n: (e[s] * blocks_per_expert + b[s], 0)),
                pl.BlockSpec((None, d, ff), lambda s, j, e, b, n: (e[s], 0, 2 * jmap(s, j, n))),
                pl.BlockSpec((None, d, ff), lambda s, j, e, b, n: (e[s], 0, 2 * jmap(s, j, n) + 1)),
                pl.BlockSpec((None, 1, 2 * ff), lambda s, j, e, b, n: (e[s], 0, jmap(s, j, n))),
                pl.BlockSpec((None, ff // 2, d), lambda s, j, e, b, n: (e[s], 2 * jmap(s, j, n), 0)),
                pl.BlockSpec((None, ff // 2, d),
                             lambda s, j, e, b, n: (e[s], 2 * jmap(s, j, n) + 1, 0)),
                pl.BlockSpec((None, 1, d), lambda s, j, e, b, n: (e[s], 0, 0)),
            ],
            out_specs=pl.BlockSpec((r_rows, d),
                                   lambda s, j, e, b, n: (e[s] * blocks_per_expert + b[s], 0)),
            scratch_shapes=[
                pltpu.VMEM((r_rows, d), BF16),
                pltpu.VMEM((d, 2 * ff), BF16),
                pltpu.VMEM((ff, d), BF16),
            ],
        ),
        out_shape=jax.ShapeDtypeStruct((ne * cap, d), F32),
        compiler_params=_params(("arbitrary", "arbitrary")),
        name="experts",
    )(sb_e, sb_blk, sb_nch, xs, w1, w1, b1.reshape(ne, 1, ff2), w2, w2, b2.reshape(ne, 1, d))


def _combine_kernel(slot_ref, x2_ref, rw_ref, g_ref, ys_ref, o_ref, gbuf, sem, *, tm, final_norm):
    t = pl.program_id(0)
    base = t * tm

    def row_copy(r, k):
        return pltpu.make_async_copy(
            ys_ref.at[pl.ds(slot_ref[(base + r) * TOP_K + k], 1), :],
            gbuf.at[pl.ds(k * tm + r, 1), :],
            sem)

    def start(r, c):
        for k in range(TOP_K):
            row_copy(r, k).start()
        return c

    def wait(r, c):
        for k in range(TOP_K):
            row_copy(r, k).wait()
        return c

    lax.fori_loop(0, tm, start, 0, unroll=ROW_DMA_UNROLL)
    lax.fori_loop(0, tm, wait, 0, unroll=ROW_DMA_UNROLL)

    rw = rw_ref[...]
    out = x2_ref[...]
    for k in range(TOP_K):
        out = out + rw[:, k:k + 1] * gbuf[pl.ds(k * tm, tm), :]
    if final_norm:
        out = _rms(out, g_ref[...])
    o_ref[...] = out


def _combine(slots, x2, rw, g, ys, final_norm):
    s, d = x2.shape
    tm = _pick(s, (256, 128))
    kern = functools.partial(_combine_kernel, tm=tm, final_norm=final_norm)
    return pl.pallas_call(
        kern,
        grid_spec=pltpu.PrefetchScalarGridSpec(
            num_scalar_prefetch=1,
            grid=(s // tm,),
            in_specs=[
                pl.BlockSpec((tm, d), lambda t, sl: (t, 0)),
                pl.BlockSpec((tm, LANES), lambda t, sl: (t, 0)),
                pl.BlockSpec((1, d), lambda t, sl: (0, 0)),
                pl.BlockSpec(memory_space=pl.ANY),
            ],
            out_specs=pl.BlockSpec((tm, d), lambda t, sl: (t, 0)),
            scratch_shapes=[pltpu.VMEM((TOP_K * tm, d), F32), pltpu.SemaphoreType.DMA(())],
        ),
        out_shape=jax.ShapeDtypeStruct((s, d), F32),
        compiler_params=_params(("arbitrary",)),
        name="combine",
    )(slots, x2, rw, g, ys)


def _superblocks(counts, r_rows, ch, ns):
    nsb = (counts + r_rows - 1) // r_rows
    ends = jnp.cumsum(nsb)
    total = ends[-1]
    sidx = jnp.arange(ns, dtype=I32)
    last = jnp.maximum(total - 1, 0)
    sclip = jnp.minimum(sidx, last)
    e = jnp.minimum(jnp.searchsorted(ends, sclip, side="right"), N_EXPERTS - 1).astype(I32)
    blk = sclip - (ends[e] - nsb[e])
    rows = jnp.clip(counts[e] - blk * r_rows, 0, r_rows)
    nch = jnp.where(sidx < total, (rows + ch - 1) // ch, 0)
    return e, blk.astype(I32), nch.astype(I32)


def _layer(layer, x, g_mix, w_in, b_forget, b_gate, w_pa, w_pb, w_out, g_ffn,
           w_router, b_router, w1, b1, w2, b2, g_out, final_norm):
    s, d = x.shape
    qkv = 6 * D_HEADS

    assert N_HEADS == SUBLANES
    w_t = jnp.swapaxes(w_in, 1, 2)[layer]
    w_main = _w_prep(w_t, qkv)
    p, f_log, vt_sb, vt_fx = _in_proj(x, g_mix.reshape(1, d), w_main, w_t, qkv)

    log_f = jax.nn.log_sigmoid(f_log[:, :N_HEADS] + b_forget)
    cum_f = jnp.cumsum(log_f, axis=0).T

    nb = D_HEADS // HEAD_DIM
    o_sb = _attention(p, vt_sb, 0, nb)
    o_fx = _attention(p, vt_fx, 3 * nb, 4 * nb, fox=(_fox_tables(cum_f), *_fox_bias(cum_f)))

    r_rows = 1280
    ch = 256
    cap = -(-(s + ch) // r_rows) * r_rows
    ns =(s * TOP_K) // r_rows + N_EXPERTS
    wr = jnp.pad(w_router, ((0, 0), (0, LANES - N_EXPERTS)))
    wr_top = _bf16_trunc(wr)
    wr_hi = wr_top.astype(BF16)
    wr_lo = (wr - wr_top).astype(BF16)
    br = jnp.pad(b_router, (0, LANES - N_EXPERTS), constant_values=NEG_BIG).reshape(1, LANES)
    x2, hp, ri, rw, cnt = _mix_route(
        o_sb, o_fx, p, qkv // d, b_gate.reshape(1, 2 * d), w_pa.astype(BF16), w_pb.astype(BF16),
        w_out.astype(BF16), x, g_ffn.reshape(1, d), wr_hi, wr_lo, br, cap)

    slots = ri[:, :TOP_K].reshape(-1)
    counts = cnt[0, :N_EXPERTS].astype(I32)
    xs = _dispatch(slots, counts, hp, cap, ch)
    sb_e, sb_blk, sb_nch = _superblocks(counts, r_rows, ch, ns)
    ys = _experts(sb_e, sb_blk, sb_nch, xs, w1, b1, w2, b2, r_rows, ch, cap)
    return _combine(slots, x2, rw, g_out.reshape(1, d), ys, final_norm)


def kernel(x, g_mix, w_in, b_forget, b_gate, w_proj_sb, w_proj_fox, w_out, g_ffn, w_router,
           b_router, w_mlp1, b_mlp1, w_mlp2, b_mlp2, g_final):
    b, s, d = x.shape
    depth = g_mix.shape[0]
    outs = []
    for bi in range(b):
        xb = x[bi]
        for layer in range(depth):
            last = layer == depth - 1
            xb = _layer(layer, xb, g_mix[layer], w_in, b_forget[layer], b_gate[layer],
                        w_proj_sb[layer], w_proj_fox[layer], w_out[layer], g_ffn[layer],
                        w_router[layer], b_router[layer], w_mlp1[layer], b_mlp1[layer],
                        w_mlp2[layer], b_mlp2[layer], g_final, last)
        outs.append(xb.reshape(1, s, d))
    return outs[0] if b == 1 else jnp.concatenate(outs, axis=0)
```

```python
import functools

import jax
import jax.numpy as jnp
from jax import lax
from jax.experimental import pallas as pl
from jax.experimental.pallas import tpu as pltpu

F32 = jnp.float32
BF16 = jnp.bfloat16
U32 = jnp.uint32
I32 = jnp.int32

LANES = 128
SUBLANES = 8
HEAD_DIM = 128
N_HEADS = 8
D_HEADS = N_HEADS * HEAD_DIM
N_EXPERTS = 32
TOP_K = 4
SWIGLU_LIMIT = 7.0
SWIGLU_ALPHA = 1.702
RMS_EPS = 1e-5
NEG_BIG = -1e30
SB_VALUE_TILE = 256
ROW_DMA_UNROLL = 8
LOG2_E = 1.4426950408889634
BF16_ROWS = 16
F32_EXP2_UNDERFLOW = 152.0
NORM_SLACK = 1.001
F32_EXP_UNDERFLOW = 105.0
VMEM_LIMIT = 56 * 1024 * 1024


def _pick(n, candidates):
    for c in candidates:
        if n % c == 0:
            return c
    raise ValueError(f"no tile in {candidates} divides {n}")


def _params(sem, vmem=VMEM_LIMIT):
    return pltpu.CompilerParams(dimension_semantics=sem, vmem_limit_bytes=vmem)


def _rms(x, g):
    ms = jnp.mean(x * x, axis=-1, keepdims=True)
    return x * lax.rsqrt(ms + RMS_EPS) * g


def _inproj_kernel(x_ref, g_ref, w_ref, wf_ref, p_ref, f_ref, vta_ref, vtb_ref, h_scr, *, tn):
    n = pl.program_id(1)

    @pl.when(n == 0)
    def _():
        hb = _rms(x_ref[...], g_ref[...]).astype(BF16)
        h_scr[...] = hb
        wf = jnp.concatenate(
            [wf_ref[...], jnp.zeros((LANES - SUBLANES, wf_ref.shape[1]), F32)], axis=0).astype(BF16)
        f_ref[...] = lax.dot_general(hb, wf, _NT, preferred_element_type=F32)

    res = jnp.dot(h_scr[...], w_ref[...], preferred_element_type=F32)
    p_ref[...] = res.astype(BF16)

    heads_per_tile = tn // HEAD_DIM
    for section, vt_ref in ((2, vta_ref), (5, vtb_ref)):
        for tile in range(section * D_HEADS // tn, (section + 1) * D_HEADS // tn):
            @pl.when(n == tile)
            def _(vt_ref=vt_ref, head0=(tile * tn - section * D_HEADS) // HEAD_DIM):
                tv = vt_ref.shape[3]
                for hh in range(heads_per_tile):
                    vt = res[:, hh * HEAD_DIM:(hh + 1) * HEAD_DIM].T.astype(BF16)
                    for c in range(vt.shape[1] // tv):
                        vt_ref[head0 + hh, c, pl.ds(0, HEAD_DIM), :] = vt[:, c * tv:(c + 1) * tv]
                        if vt_ref.shape[2] > HEAD_DIM:
                            extra = lax.broadcasted_iota(I32, (BF16_ROWS, tv), 0) == 0
                            vt_ref[head0 + hh, c, pl.ds(HEAD_DIM, BF16_ROWS), :] = extra.astype(BF16)


def _wprep_kernel(a_ref, b_ref, o_ref, *, first_tail):
    n = pl.program_id(1)

    @pl.when(n < first_tail)
    def _():
        o_ref[...] = a_ref[...].T.astype(BF16)

    @pl.when(n >= first_tail)
    def _():
        ext = jnp.concatenate([a_ref[...], b_ref[...]], axis=0)
        o_ref[...] = ext[SUBLANES:, :].T.astype(BF16)


def _w_prep(w_t, qkv):
    n_in, d = w_t.shape
    n_out = n_in - SUBLANES
    tn = _pick(n_out, (1024, 512, 256, 128))
    tr = _pick(d, (512, 256, 128))
    per = tn // SUBLANES
    kern = functools.partial(_wprep_kernel, first_tail=qkv // tn)
    return pl.pallas_call(
        kern,
        grid=(d // tr, n_out // tn),
        in_specs=[
            pl.BlockSpec((tn, tr), lambda r, n: (n, r)),
            pl.BlockSpec((SUBLANES, tr), lambda r, n: ((n + 1) * per, r)),
        ],
        out_specs=pl.BlockSpec((tr, tn), lambda r, n: (r, n)),
        out_shape=jax.ShapeDtypeStruct((d, n_out), BF16),
        compiler_params=_params(("parallel", "parallel")),
        name="w_prep",
    )(w_t, w_t)


def _in_proj(x, g, w_main, w_t, f_row):
    s, d = x.shape
    n = w_main.shape[1]
    tm = _pick(s, (1024, 512, 256, 128))
    tn = _pick(n, (1024, 512, 256, 128))
    tq = _attn_tile(s)
    rows_b = HEAD_DIM + BF16_ROWS
    tv_a = min(tq, SB_VALUE_TILE)
    vt_spec = lambda rows, tv: pl.BlockSpec((N_HEADS, tm // tv, rows, tv), lambda m, j: (0, m, 0, 0))
    return pl.pallas_call(
        functools.partial(_inproj_kernel, tn=tn),
        grid=(s // tm, n // tn),
        in_specs=[
            pl.BlockSpec((tm, d), lambda m, j: (m, 0)),
            pl.BlockSpec((1, d), lambda m, j: (0, 0)),
            pl.BlockSpec((d, tn), lambda m, j: (0, j)),
            pl.BlockSpec((SUBLANES, d), lambda m, j: (f_row // SUBLANES, 0)),
        ],
        out_specs=[
            pl.BlockSpec((tm, tn), lambda m, j: (m, j)),
            pl.BlockSpec((tm, LANES), lambda m, j: (m, 0)),
            vt_spec(HEAD_DIM, tv_a),
            vt_spec(rows_b, tq),
        ],
        out_shape=[
            jax.ShapeDtypeStruct((s, n), BF16),
            jax.ShapeDtypeStruct((s, LANES), F32),
            jax.ShapeDtypeStruct((N_HEADS, s // tv_a, HEAD_DIM, tv_a), BF16),
            jax.ShapeDtypeStruct((N_HEADS, s // tq, rows_b, tq), BF16),
        ],
        scratch_shapes=[pltpu.VMEM((tm, d), BF16)],
        compiler_params=_params(("parallel", "arbitrary")),
        name="in_proj",
    )(x, g, w_main, w_t)


_NT = (((1,), (1,)), ((), ()))


def _sb_kernel(q_ref, k_ref, vt_ref, o_ref, acc_scr, *, tq, tk, tv, scale):
    i = pl.program_id(1)
    nsub = tq // tk
    per_tv = tv // tk
    q = q_ref[...]
    row = lax.broadcasted_iota(I32, (tk, 2 * tk), 0)
    col = lax.broadcasted_iota(I32, (tk, 2 * tk), 1)
    ut2 = ((col % tk) >= row).astype(BF16)
    diff = lax.broadcasted_iota(I32, (tk, tq), 1) - lax.broadcasted_iota(I32, (tk, tq), 0)

    def sub_block(kb, qs, csum, mask):
        k = k_ref[pl.ds(pl.multiple_of(kb * tk, tk), tk), :]
        z = lax.dot_general(k, qs, _NT, preferred_element_type=F32) * scale
        lk = -(jnp.maximum(z, 0.0) + jnp.log(1.0 + jnp.exp(-jnp.abs(z))))
        if mask is not None:
            lk = jnp.where(mask, lk, 0.0)
        hi = lk.astype(BF16)
        lo = (lk - hi.astype(F32)).astype(BF16)
        cs = jnp.dot(ut2, jnp.concatenate([hi, lo], axis=0), preferred_element_type=F32)
        a = jnp.exp(z + cs + csum)
        if mask is not None:
            a = jnp.where(mask, a, 0.0)
        return a.astype(BF16), cs[0:1, :]

    acc_scr[...] = jnp.zeros_like(acc_scr)
    csum = jnp.zeros((1, tq), F32)
    for u in reversed(range(nsub)):
        lo = u * tk
        a, inc = sub_block(i * nsub + u, q[lo:, :], csum[:, lo:], diff[:, lo:] > lo)
        vt = vt_ref[i * (tq // tv) + u // per_tv, :, pl.ds((u % per_tv) * tk, tk)]
        acc_scr[:, lo:] += jnp.dot(vt, a, preferred_element_type=F32)
        csum = jnp.concatenate([csum[:, :lo], csum[:, lo:] + inc], axis=1) if lo else csum + inc

    def far(g, csum):
        parts = [None] * per_tv
        for u in reversed(range(per_tv)):
            parts[u], inc = sub_block(g * per_tv + u, q, csum, None)
            csum = csum + inc
        acc_scr[...] += jnp.dot(vt_ref[g], jnp.concatenate(parts, axis=0),
                                preferred_element_type=F32)
        return csum

    first_far = i * (tq // tv) - 1

    def live(state):
        n, c = state
        return jnp.logical_and(n <= first_far, jnp.max(c) > -F32_EXP_UNDERFLOW)

    lax.while_loop(live, lambda st: (st[0] + 1, far(first_far - st[0], st[1])),
                   (jnp.int32(0), csum))
    o_ref[...] = acc_scr[...].T.astype(o_ref.dtype)


def _fox_kernel(ff_ref, fl_ref, q_ref, qa_ref, k_ref, ka_ref, vt_ref, o_ref,
                acc_scr, s_scr, kn_smem, *, tq, tk, scale):
    i = pl.program_id(1)
    nsub = tq // tk
    q = jnp.concatenate([q_ref[...], qa_ref[...]], axis=1)
    diff = lax.broadcasted_iota(I32, (tk, tq), 1) - lax.broadcasted_iota(I32, (tk, tq), 0)
    to_log2 = scale * LOG2_E

    def scores(t, slot, masked):
        for u in range(nsub):
            ks = pl.multiple_of((t * nsub + u) * tk, tk)
            k = jnp.concatenate([k_ref[pl.ds(ks, tk), :], ka_ref[pl.ds(ks, tk), :]], axis=1)
            s = lax.dot_general(k, q, _NT, preferred_element_type=F32) * to_log2
            if masked:
                s = jnp.where(diff >= u * tk, s, NEG_BIG)
            s_scr[slot, pl.ds(u * tk, tk), :] = s

    def consume(t, slot, m):
        top = s_scr[slot, pl.ds(0, tk), :]
        for u in range(1, nsub):
            top = jnp.maximum(top, s_scr[slot, pl.ds(u * tk, tk), :])
        m_new = jnp.maximum(m, jnp.max(top, axis=0, keepdims=True))
        alpha = jnp.exp2(m - m_new)
        pcat = jnp.exp2(s_scr[slot] - m_new).astype(BF16)
        acc_scr[...] = alpha * acc_scr[...] + jnp.dot(vt_ref[t], pcat, preferred_element_type=F32)
        return m_new

    def step(t, slot, m):
        scores(t - 1, 1 - slot, False)
        return consume(t, slot, m)

    def max_norm(x):
        xf = x.astype(F32)
        return jnp.max(jnp.sqrt(jnp.sum(xf * xf, axis=1, keepdims=True)))

    kn_prev = jnp.where(i > 0, kn_smem[jnp.maximum(i - 1, 0)], 0.0)
    kn_smem[i] = jnp.maximum(kn_prev, max_norm(k_ref[pl.ds(pl.multiple_of(i * tq, tq), tq), :]))

    acc_scr[...] = jnp.zeros_like(acc_scr)
    scores(i, 0, True)

    @pl.when(i > 0)
    def _():
        scores(i - 1, 1, False)

    m = consume(i, 0, jnp.full((1, tq), NEG_BIG, F32))

    @pl.when(i > 0)
    def _():
        nt = pl.num_programs(1)
        row = pl.program_id(0) * nt
        floor = jnp.min(m) - F32_EXP2_UNDERFLOW
        qk = to_log2 * NORM_SLACK * max_norm(q_ref[...])
        f_first = LOG2_E * ff_ref[row + i]

        def live(extra):
            t = i - 2 - extra
            bound = qk * kn_smem[jnp.maximum(t, 0)] + (
                f_first - LOG2_E * fl_ref[row + jnp.maximum(t, 0)])
            return jnp.logical_and(t >= 0, bound >= floor)

        extra = lax.while_loop(live, lambda e: e + 1, jnp.int32(0))
        last = i - 1 - extra

        def trip(k, m):
            t = i - 1 - 2 * k
            return step(t - 1, 0, step(t, 1, m))

        m2 = lax.fori_loop(0, extra // 2, trip, m)

        @pl.when(extra % 2 == 0)
        def _():
            consume(last, 1, m2)

        @pl.when(extra % 2 == 1)
        def _():
            consume(last, 0, step(last + 1, 1, m2))

    acc = acc_scr[...]
    o_ref[...] = (acc[:HEAD_DIM] / acc[HEAD_DIM:HEAD_DIM + 1]).T.astype(o_ref.dtype)


def _attn_tile(s):
    return _pick(s, (512, 256, 128))


def _attention(p, vt, q_col, k_col, fox=None):
    s = p.shape[0]
    tq = _attn_tile(s)
    tk = 128
    scale = HEAD_DIM ** -0.5
    rows = vt.shape[2]
    q_spec = pl.BlockSpec((tq, HEAD_DIM), lambda h, i, *_: (i, q_col + h))
    k_spec = pl.BlockSpec((s, HEAD_DIM), lambda h, i, *_: (0, k_col + h))
    vt_spec = pl.BlockSpec((None,) + vt.shape[1:], lambda h, i, *_: (h, 0, 0, 0))
    scratch = [pltpu.VMEM((rows, tq), F32)]
    order = ("parallel", "arbitrary")
    if fox is None:
        kern = functools.partial(_sb_kernel, tq=tq, tk=tk, tv=vt.shape[3], scale=scale)
        name = "sb_attn"
        tables = []
        in_specs = [q_spec, k_spec, vt_spec]
        args = [p, p, vt]
    else:
        tables, qa, ka = fox
        scratch.append(pltpu.VMEM((2, tq, tq), F32))
        scratch.append(pltpu.SMEM((s // tq,), F32))
        order = ("arbitrary", "arbitrary")
        kern = functools.partial(_fox_kernel, tq=tq, tk=tk, scale=scale)
        name = "fox_attn"
        in_specs = [
            q_spec,
            pl.BlockSpec((None, tq, HEAD_DIM), lambda h, i, *_: (h, i, 0)),
            k_spec,
            pl.BlockSpec((None, s, HEAD_DIM), lambda h, i, *_: (h, 0, 0)),
            vt_spec,
        ]
        args = [p, qa, p, ka, vt]
    return pl.pallas_call(
        kern,
        grid_spec=pltpu.PrefetchScalarGridSpec(
            num_scalar_prefetch=len(tables),
            grid=(N_HEADS, s // tq),
            in_specs=in_specs,
            out_specs=pl.BlockSpec((tq, HEAD_DIM), lambda h, i, *_: (i, h)),
            scratch_shapes=scratch,
        ),
        out_shape=jax.ShapeDtypeStruct((s, D_HEADS), BF16),
        compiler_params=_params(order),
        name=name,
    )(*tables, *args)


def _fox_tables(cum_f):
    nt = cum_f.shape[1] // _attn_tile(cum_f.shape[1])
    f_tiles = cum_f.reshape(N_HEADS, nt, -1)
    return [f_tiles[:, :, 0].reshape(-1), f_tiles[:, :, -1].reshape(-1)]


def _bf16_trunc(x):
    bits = lax.bitcast_convert_type(x, U32) & jnp.uint32(0xFFFF0000)
    return lax.bitcast_convert_type(bits, F32)


def _split3(x):
    a = _bf16_trunc(x)
    b = _bf16_trunc(x - a)
    c = _bf16_trunc(x - a - b)
    return [a.astype(BF16), b.astype(BF16), c.astype(BF16)]


def _fox_bias(cum_f):
    parts = _split3(cum_f * (HEAD_DIM ** 0.5))
    ones = jnp.ones_like(parts[0])
    zeros = jnp.zeros_like(parts[0])
    widen = jnp.eye(SUBLANES, HEAD_DIM, dtype=BF16)

    def spread(cols):
        packed = jnp.stack(cols + [zeros, zeros], axis=-1)
        return jnp.einsum("hsc,cl->hsl", packed, widen,
                          preferred_element_type=F32).astype(BF16)

    return spread([ones] * 3 + parts), spread([-t for t in parts] + [ones] * 3)


def _bf16_pair_pack(h):
    bits = lax.bitcast_convert_type(h, U32)
    rnd = bits + jnp.uint32(0x7FFF) + ((bits >> 16) & jnp.uint32(1))
    half = h.shape[1] // 2
    return (rnd[:, half:] & jnp.uint32(0xFFFF0000)) | (rnd[:, :half] >> 16)


def _bf16_pair_unpack(w):
    lo = lax.bitcast_convert_type(w << 16, F32)
    hi = lax.bitcast_convert_type(w & jnp.uint32(0xFFFF0000), F32)
    return jnp.concatenate([lo, hi], axis=1).astype(BF16)


def _mix_kernel(osb_ref, ofx_ref, ga_ref, gb_ref, bg_ref, wpa_ref, wpb_ref, wo_ref, x_ref, g_ref,
                wrh_ref, wrl_ref, br_ref,
                x2_ref, hp_ref, ri_ref, rw_ref, cnt_ref, cnt_scr, *, tm, d, cap):
    t = pl.program_id(0)

    @pl.when(t == 0)
    def _():
        cnt_scr[...] = jnp.zeros_like(cnt_scr)

    bg = bg_ref[...]
    pa = jnp.dot(osb_ref[...], wpa_ref[...], preferred_element_type=F32)
    pb = jnp.dot(ofx_ref[...], wpb_ref[...], preferred_element_type=F32)
    ga = jax.nn.sigmoid(ga_ref[...].astype(F32) + bg[:, :d])
    gb = jax.nn.sigmoid(gb_ref[...].astype(F32) + bg[:, d:])
    mixed = (ga * pa + gb * pb).astype(BF16)
    x2 = x_ref[...] + jnp.dot(mixed, wo_ref[...], preferred_element_type=F32)
    x2_ref[...] = x2
    h2 = _rms(x2, g_ref[...])
    hp_ref[...] = _bf16_pair_pack(h2)

    hh = h2.astype(BF16)
    hl = (h2 - hh.astype(F32)).astype(BF16)
    logits = (jnp.dot(hh, wrh_ref[...], preferred_element_type=F32)
              + jnp.dot(hh, wrl_ref[...], preferred_element_type=F32)
              + jnp.dot(hl, wrh_ref[...], preferred_element_type=F32)
              + br_ref[...])

    lane = lax.broadcasted_iota(I32, (tm, LANES), 1).astype(F32)
    work = logits
    sel = jnp.zeros((tm, LANES), F32)
    vals, idxs = [], []
    for _ in range(TOP_K):
        mx = jnp.max(work, axis=1, keepdims=True)
        idx = jnp.min(jnp.where(work == mx, lane, float(LANES)), axis=1, keepdims=True)
        hit = lane == idx
        vals.append(mx)
        idxs.append(idx)
        work = jnp.where(hit, -jnp.inf, work)
        sel = jnp.where(hit, 1.0, sel)
    exps = [jnp.exp(v - vals[0]) for v in vals]
    denom = exps[0] + exps[1] + exps[2] + exps[3]

    row = lax.broadcasted_iota(I32, (tm, tm), 0)
    col = lax.broadcasted_iota(I32, (tm, tm), 1)
    tri = (col < row).astype(BF16)
    rank = jnp.dot(tri, sel.astype(BF16), preferred_element_type=F32) + cnt_scr[...]
    cnt_new = cnt_scr[...] + jnp.sum(sel, axis=0, keepdims=True)
    cnt_scr[...] = cnt_new
    cnt_ref[...] = cnt_new
    slot = rank + lane * float(cap)
    ri = jnp.zeros((tm, LANES), F32)
    rw = jnp.zeros((tm, LANES), F32)
    for r in range(TOP_K):
        slot_r = jnp.sum(jnp.where(lane == idxs[r], slot, 0.0), axis=1, keepdims=True)
        ri = jnp.where(lane == float(r), slot_r, ri)
        rw = jnp.where(lane == float(r), exps[r] / denom, rw)
    ri_ref[...] = ri.astype(I32)
    rw_ref[...] = rw


def _mix_route(o_sb, o_fx, p, gate_col, b_gate, w_pa, w_pb, w_out, x, g_ffn, wr_hi, wr_lo, br, cap):
    s, d = x.shape
    tm = _pick(s, (256, 128))
    const = lambda shape: pl.BlockSpec(shape, lambda t: (0, 0), pipeline_mode=pl.Buffered(1))
    kern = functools.partial(_mix_kernel, tm=tm, d=d, cap=cap)
    return pl.pallas_call(
        kern,
        grid=(s // tm,),
        in_specs=[
            pl.BlockSpec((tm, D_HEADS), lambda t: (t, 0)),
            pl.BlockSpec((tm, D_HEADS), lambda t: (t, 0)),
            pl.BlockSpec((tm, d), lambda t: (t, gate_col)),
            pl.BlockSpec((tm, d), lambda t: (t, gate_col + 1)),
            const((1, 2 * d)),
            const((D_HEADS, d)),
            const((D_HEADS, d)),
            const((d, d)),
            pl.BlockSpec((tm, d), lambda t: (t, 0)),
            const((1, d)),
            const((d, LANES)),
            const((d, LANES)),
            const((1, LANES)),
        ],
        out_specs=[
            pl.BlockSpec((tm, d), lambda t: (t, 0)),
            pl.BlockSpec((tm, d // 2), lambda t: (t, 0)),
            pl.BlockSpec((tm, LANES), lambda t: (t, 0)),
            pl.BlockSpec((tm, LANES), lambda t: (t, 0)),
            pl.BlockSpec((1, LANES), lambda t: (0, 0)),
        ],
        out_shape=[
            jax.ShapeDtypeStruct((s, d), F32),
            jax.ShapeDtypeStruct((s, d // 2), U32),
            jax.ShapeDtypeStruct((s, LANES), I32),
            jax.ShapeDtypeStruct((s, LANES), F32),
            jax.ShapeDtypeStruct((1, LANES), F32),
        ],
        scratch_shapes=[pltpu.VMEM((1, LANES), F32)],
        compiler_params=_params(("arbitrary",)),
        name="mix_route",
    )(o_sb, o_fx, p, p, b_gate, w_pa, w_pb, w_out, x, g_ffn, wr_hi, wr_lo, br)


def _dispatch_kernel(slot_ref, cnt_ref, hp_ref, xs_ref, zero_scr, sem, zsem, *, tm, ch, cap):
    t = pl.program_id(0)
    base = t * tm

    def row_copy(r, k):
        return pltpu.make_async_copy(
            hp_ref.at[pl.ds(r, 1), :],
            xs_ref.at[pl.ds(slot_ref[(base + r) * TOP_K + k], 1), :],
            sem)

    def start(r, c):
        for k in range(TOP_K):
            row_copy(r, k).start()
        return c

    def wait(r, c):
        for k in range(TOP_K):
            row_copy(r, k).wait()
        return c

    @pl.when(t == 0)
    def _():
        zero_scr[...] = jnp.zeros_like(zero_scr)

        def pad_copy(e):
            first = pl.multiple_of(e * cap + (cnt_ref[e] // SUBLANES) * SUBLANES, SUBLANES)
            return pltpu.make_async_copy(zero_scr, xs_ref.at[pl.ds(first, ch), :], zsem)

        def zstart(e, c):
            pad_copy(e).start()
            return c

        def zwait(e, c):
            pad_copy(e).wait()
            return c

        lax.fori_loop(0, N_EXPERTS, zstart, 0)
        lax.fori_loop(0, N_EXPERTS, zwait, 0)

    lax.fori_loop(0, tm, start, 0, unroll=ROW_DMA_UNROLL)
    lax.fori_loop(0, tm, wait, 0, unroll=ROW_DMA_UNROLL)


def _dispatch(slots, counts, hp, cap, ch):
    s, half = hp.shape
    tm = _pick(s, (512, 256, 128))
    kern = functools.partial(_dispatch_kernel, tm=tm, ch=ch, cap=cap)
    return pl.pallas_call(
        kern,
        grid_spec=pltpu.PrefetchScalarGridSpec(
            num_scalar_prefetch=2,
            grid=(s // tm,),
            in_specs=[pl.BlockSpec((tm, half), lambda t, sl, cn: (t, 0))],
            out_specs=pl.BlockSpec(memory_space=pl.ANY),
            scratch_shapes=[
                pltpu.VMEM((ch, half), U32),
                pltpu.SemaphoreType.DMA(()),
                pltpu.SemaphoreType.DMA(()),
            ],
        ),
        out_shape=jax.ShapeDtypeStruct((N_EXPERTS * cap, half), U32),
        compiler_params=_params(("arbitrary",)),
        name="dispatch",
    )(slots, counts, hp)


def _expert_kernel(sbe_ref, sbblk_ref, sbnch_ref, x_ref, w1a_ref, w1c_ref, b1_ref, w2a_ref, w2c_ref,
                   b2_ref, y_ref, xb_scr, w1b_scr, w2b_scr, *, ch, ff):
    s = pl.program_id(0)
    j = pl.program_id(1)
    nch = sbnch_ref[s]
    half = LANES // 2

    def half_act(g):
        lin = pltpu.roll(g, LANES - 1, 1)
        gate = jnp.minimum(g, SWIGLU_LIMIT)
        lin = jnp.clip(lin, -SWIGLU_LIMIT, SWIGLU_LIMIT)
        return gate * jax.nn.sigmoid(SWIGLU_ALPHA * gate) * (lin + 1.0)

    @pl.when(nch > 0)
    def _():
        @pl.when(j == 0)
        def _():
            xb_scr[...] = _bf16_pair_unpack(x_ref[...])

            def init(c, carry):
                y_ref[pl.ds(pl.multiple_of(c * ch, ch), ch), :] = jnp.broadcast_to(
                    b2_ref[...], (ch, y_ref.shape[1]))
                return carry

            lax.fori_loop(0, nch, init, 0)

        def prep():
            w1b_scr[:, :ff] = w1a_ref[...].astype(BF16)
            w1b_scr[:, ff:] = w1c_ref[...].astype(BF16)
            prow = lax.broadcasted_iota(I32, (LANES, LANES), 0)
            pcol = lax.broadcasted_iota(I32, (LANES, LANES), 1)
            perm = (pcol == prow // 2 + half * (prow % 2)).astype(BF16)
            groups = ff // LANES
            for u in range(groups):
                src_ref, v = (w2a_ref, u) if u < groups // 2 else (w2c_ref, u - groups // 2)
                grp = src_ref[pl.ds(v * LANES, LANES), :].astype(BF16)
                w2b_scr[pl.ds(u * LANES, LANES), :] = jnp.dot(
                    perm, grp, preferred_element_type=F32).astype(BF16)

        def up(r0, rows):
            gu = jnp.dot(xb_scr[pl.ds(r0, rows), :], w1b_scr[...],
                         preferred_element_type=F32) + b1_ref[...]
            even = (lax.broadcasted_iota(I32, (rows, LANES), 1) % 2) == 0
            pieces = []
            for u in range(ff // LANES):
                a = half_act(gu[:, (2 * u) * LANES:(2 * u + 1) * LANES])
                b = half_act(gu[:, (2 * u + 1) * LANES:(2 * u + 2) * LANES])
                pieces.append(jnp.where(even, a, pltpu.roll(b, 1, 1)))
            return jnp.concatenate(pieces, axis=1).astype(BF16)

        def down(r0, rows, act):
            y_ref[pl.ds(r0, rows), :] += jnp.dot(act, w2b_scr[...], preferred_element_type=F32)

        full = xb_scr.shape[0] // ch
        for n in (full, full - 1):
            @pl.when(nch == n)
            def _(n=n):
                prep()
                down(0, n * ch, up(0, n * ch))

        big = 2 * ch
        nbig = nch // 2
        short = nch < full - 1

        @pl.when(short)
        def _():
            prep()

        @pl.when(jnp.logical_and(short, nbig > 0))
        def _():
            def trip(c, act):
                nxt = up(pl.multiple_of(c * big, big), big)
                down(pl.multiple_of((c - 1) * big, big), big, act)
                return nxt

            act = lax.fori_loop(1, nbig, trip, up(0, big))
            down(pl.multiple_of((nbig - 1) * big, big), big, act)

        @pl.when(jnp.logical_and(short, nch % 2 == 1))
        def _():
            r0 = pl.multiple_of(nbig * big, ch)
            down(r0, ch, up(r0, ch))


def _experts(sb_e, sb_blk, sb_nch, xs, w1, b1, w2, b2, r_rows, ch, cap):
    ne, d, ff2 = w1.shape
    dff = ff2 // 2
    ff = _pick(dff, (256, 128))
    nj = dff // ff
    ns = sb_e.shape[0]
    half = d // 2
    blocks_per_expert = cap // r_rows

    def jmap(s, j, nch):
        return jnp.where(nch[s] > 0, j, nj - 1)

    kern = functools.partial(_expert_kernel, ch=ch, ff=ff)
    return pl.pallas_call(
        kern,
        grid_spec=pltpu.PrefetchScalarGridSpec(
            num_scalar_prefetch=3,
            grid=(ns, nj),
            in_specs=[
                pl.BlockSpec((r_rows, half),
                             lambda s, j, e, b, n: (e[s] * blocks_per_expert + b[s], 0)),
                pl.BlockSpec((None, d, ff), lambda s, j, e, b, n: (e[s], 0, 2 * jmap(s, j, n))),
                pl.BlockSpec((None, d, ff), lambda s, j, e, b, n: (e[s], 0, 2 * jmap(s, j, n) + 1)),
                pl.BlockSpec((None, 1, 2 * ff), lambda s, j, e, b, n: (e[s], 0, jmap(s, j, n))),
                pl.BlockSpec((None, ff // 2, d), lambda s, j, e, b, n: (e[s], 2 * jmap(s, j, n), 0)),
                pl.BlockSpec((None, ff // 2, d),
                             lambda s, j, e, b, n: (e[s], 2 * jmap(s, j, n) + 1, 0)),
                pl.BlockSpec((None, 1, d), lambda s, j, e, b, n: (e[s], 0, 0)),
            ],
            out_specs=pl.BlockSpec((r_rows, d),
                                   lambda s, j, e, b, n: (e[s] * blocks_per_expert + b[s], 0)),
            scratch_shapes=[
                pltpu.VMEM((r_rows, d), BF16),
                pltpu.VMEM((d, 2 * ff), BF16),
                pltpu.VMEM((ff, d), BF16),
            ],
        ),
        out_shape=jax.ShapeDtypeStruct((ne * cap, d), F32),
        compiler_params=_params(("arbitrary", "arbitrary")),
        name="experts",
    )(sb_e, sb_blk, sb_nch, xs, w1, w1, b1.reshape(ne, 1, ff2), w2, w2, b2.reshape(ne, 1, d))


def _combine_kernel(slot_ref, x2_ref, rw_ref, g_ref, ys_ref, o_ref, gbuf, sem, *, tm, final_norm):
    t = pl.program_id(0)
    base = t * tm

    def row_copy(r, k):
        return pltpu.make_async_copy(
            ys_ref.at[pl.ds(slot_ref[(base + r) * TOP_K + k], 1), :],
            gbuf.at[pl.ds(k * tm + r, 1), :],
            sem)

    def start(r, c):
        for k in range(TOP_K):
            row_copy(r, k).start()
        return c

    def wait(r, c):
        for k in range(TOP_K):
            row_copy(r, k).wait()
        return c

    lax.fori_loop(0, tm, start, 0, unroll=ROW_DMA_UNROLL)
    lax.fori_loop(0, tm, wait, 0, unroll=ROW_DMA_UNROLL)

    rw = rw_ref[...]
    out = x2_ref[...]
    for k in range(TOP_K):
        out = out + rw[:, k:k + 1] * gbuf[pl.ds(k * tm, tm), :]
    if final_norm:
        out = _rms(out, g_ref[...])
    o_ref[...] = out


def _combine(slots, x2, rw, g, ys, final_norm):
    s, d = x2.shape
    tm = _pick(s, (256, 128))
    kern = functools.partial(_combine_kernel, tm=tm, final_norm=final_norm)
    return pl.pallas_call(
        kern,
        grid_spec=pltpu.PrefetchScalarGridSpec(
            num_scalar_prefetch=1,
            grid=(s // tm,),
            in_specs=[
                pl.BlockSpec((tm, d), lambda t, sl: (t, 0)),
                pl.BlockSpec((tm, LANES), lambda t, sl: (t, 0)),
                pl.BlockSpec((1, d), lambda t, sl: (0, 0)),
                pl.BlockSpec(memory_space=pl.ANY),
            ],
            out_specs=pl.BlockSpec((tm, d), lambda t, sl: (t, 0)),
            scratch_shapes=[pltpu.VMEM((TOP_K * tm, d), F32), pltpu.SemaphoreType.DMA(())],
        ),
        out_shape=jax.ShapeDtypeStruct((s, d), F32),
        compiler_params=_params(("arbitrary",)),
        name="combine",
    )(slots, x2, rw, g, ys)


def _superblocks(counts, r_rows, ch, ns):
    nsb = (counts + r_rows - 1) // r_rows
    ends = jnp.cumsum(nsb)
    total = ends[-1]
    sidx = jnp.arange(ns, dtype=I32)
    last = jnp.maximum(total - 1, 0)
    sclip = jnp.minimum(sidx, last)
    e = jnp.minimum(jnp.searchsorted(ends, sclip, side="right"), N_EXPERTS - 1).astype(I32)
    blk = sclip - (ends[e] - nsb[e])
    rows = jnp.clip(counts[e] - blk * r_rows, 0, r_rows)
    nch = jnp.where(sidx < total, (rows + ch - 1) // ch, 0)
    return e, blk.astype(I32), nch.astype(I32)


def _layer(layer, x, g_mix, w_in, b_forget, b_gate, w_pa, w_pb, w_out, g_ffn,
           w_router, b_router, w1, b1, w2, b2, g_out, final_norm):
    s, d = x.shape
    qkv = 6 * D_HEADS

    assert N_HEADS == SUBLANES
    w_t = jnp.swapaxes(w_in, 1, 2)[layer]
    w_main = _w_prep(w_t, qkv)
    p, f_log, vt_sb, vt_fx = _in_proj(x, g_mix.reshape(1, d), w_main, w_t, qkv)

    log_f = jax.nn.log_sigmoid(f_log[:, :N_HEADS] + b_forget)
    cum_f = jnp.cumsum(log_f, axis=0).T

    nb = D_HEADS // HEAD_DIM
    o_sb = _attention(p, vt_sb, 0, nb)
    o_fx = _attention(p, vt_fx, 3 * nb, 4 * nb, fox=(_fox_tables(cum_f), *_fox_bias(cum_f)))

    r_rows = 1280
    ch = 256
    cap = -(-(s + ch) // r_rows) * r_rows
    ns =(s * TOP_K) // r_rows + N_EXPERTS
    wr = jnp.pad(w_router, ((0, 0), (0, LANES - N_EXPERTS)))
    wr_top = _bf16_trunc(wr)
    wr_hi = wr_top.astype(BF16)
    wr_lo = (wr - wr_top).astype(BF16)
    br = jnp.pad(b_router, (0, LANES - N_EXPERTS), constant_values=NEG_BIG).reshape(1, LANES)
    x2, hp, ri, rw, cnt = _mix_route(
        o_sb, o_fx, p, qkv // d, b_gate.reshape(1, 2 * d), w_pa.astype(BF16), w_pb.astype(BF16),
        w_out.astype(BF16), x, g_ffn.reshape(1, d), wr_hi, wr_lo, br, cap)

    slots = ri[:, :TOP_K].reshape(-1)
    counts = cnt[0, :N_EXPERTS].astype(I32)
    xs = _dispatch(slots, counts, hp, cap, ch)
    sb_e, sb_blk, sb_nch = _superblocks(counts, r_rows, ch, ns)
    ys = _experts(sb_e, sb_blk, sb_nch, xs, w1, b1, w2, b2, r_rows, ch, cap)
    return _combine(slots, x2, rw, g_out.reshape(1, d), ys, final_norm)


def kernel(x, g_mix, w_in, b_forget, b_gate, w_proj_sb, w_proj_fox, w_out, g_ffn, w_router,
           b_router, w_mlp1, b_mlp1, w_mlp2, b_mlp2, g_final):
    b, s, d = x.shape
    depth = g_mix.shape[0]
    outs = []
    for bi in range(b):
        xb = x[bi]
        for layer in range(depth):
            last = layer == depth - 1
            xb = _layer(layer, xb, g_mix[layer], w_in, b_forget[layer], b_gate[layer],
                        w_proj_sb[layer], w_proj_fox[layer], w_out[layer], g_ffn[layer],
                        w_router[layer], b_router[layer], w_mlp1[layer], b_mlp1[layer],
                        w_mlp2[layer], b_mlp2[layer], g_final, last)
        outs.append(xb.reshape(1, s, d))
    return outs[0] if b == 1 else jnp.concatenate(outs, axis=0)
```

```python
import functools

import jax
import jax.numpy as jnp
from jax import lax
from jax.experimental import pallas as pl
from jax.experimental.pallas import tpu as pltpu

F32 = jnp.float32
BF16 = jnp.bfloat16
U32 = jnp.uint32
I32 = jnp.int32

LANES = 128
SUBLANES = 8
HEAD_DIM = 128
N_HEADS = 8
D_HEADS = N_HEADS * HEAD_DIM
N_EXPERTS = 32
TOP_K = 4
SWIGLU_LIMIT = 7.0
SWIGLU_ALPHA = 1.702
RMS_EPS = 1e-5
NEG_BIG = -1e30
WHOLE_RUN_SIZES = 3
SB_VALUE_TILE = 256
ROW_DMA_UNROLL = 8
LOG2_E = 1.4426950408889634
BF16_ROWS = 16
F32_EXP2_UNDERFLOW = 152.0
NORM_SLACK = 1.001
F32_EXP_UNDERFLOW = 105.0
VMEM_LIMIT = 56 * 1024 * 1024


def _pick(n, candidates):
    for c in candidates:
        if n % c == 0:
            return c
    raise ValueError(f"no tile in {candidates} divides {n}")


def _params(sem, vmem=VMEM_LIMIT):
    return pltpu.CompilerParams(dimension_semantics=sem, vmem_limit_bytes=vmem)


def _rms(x, g):
    ms = jnp.mean(x * x, axis=-1, keepdims=True)
    return x * lax.rsqrt(ms + RMS_EPS) * g


def _inproj_kernel(x_ref, g_ref, w_ref, wf_ref, p_ref, f_ref, vta_ref, vtb_ref, h_scr, *, tn):
    n = pl.program_id(1)

    @pl.when(n == 0)
    def _():
        hb = _rms(x_ref[...], g_ref[...]).astype(BF16)
        h_scr[...] = hb
        wf = jnp.concatenate(
            [wf_ref[...], jnp.zeros((LANES - SUBLANES, wf_ref.shape[1]), F32)], axis=0).astype(BF16)
        f_ref[...] = lax.dot_general(hb, wf, _NT, preferred_element_type=F32)

    res = jnp.dot(h_scr[...], w_ref[...], preferred_element_type=F32)
    p_ref[...] = res.astype(BF16)

    heads_per_tile = tn // HEAD_DIM
    for section, vt_ref in ((2, vta_ref), (5, vtb_ref)):
        for tile in range(section * D_HEADS // tn, (section + 1) * D_HEADS // tn):
            @pl.when(n == tile)
            def _(vt_ref=vt_ref, head0=(tile * tn - section * D_HEADS) // HEAD_DIM):
                tv = vt_ref.shape[3]
                for hh in range(heads_per_tile):
                    vt = res[:, hh * HEAD_DIM:(hh + 1) * HEAD_DIM].T.astype(BF16)
                    for c in range(vt.shape[1] // tv):
                        vt_ref[head0 + hh, c, pl.ds(0, HEAD_DIM), :] = vt[:, c * tv:(c + 1) * tv]
                        if vt_ref.shape[2] > HEAD_DIM:
                            extra = lax.broadcasted_iota(I32, (BF16_ROWS, tv), 0) == 0
                            vt_ref[head0 + hh, c, pl.ds(HEAD_DIM, BF16_ROWS), :] = extra.astype(BF16)


def _wprep_kernel(a_ref, b_ref, o_ref, *, first_tail):
    n = pl.program_id(1)

    @pl.when(n < first_tail)
    def _():
        o_ref[...] = a_ref[...].T.astype(BF16)

    @pl.when(n >= first_tail)
    def _():
        ext = jnp.concatenate([a_ref[...], b_ref[...]], axis=0)
        o_ref[...] = ext[SUBLANES:, :].T.astype(BF16)


def _w_prep(w_t, qkv):
    n_in, d = w_t.shape
    n_out = n_in - SUBLANES
    tn = _pick(n_out, (1024, 512, 256, 128))
    tr = _pick(d, (512, 256, 128))
    per = tn // SUBLANES
    kern = functools.partial(_wprep_kernel, first_tail=qkv // tn)
    return pl.pallas_call(
        kern,
        grid=(d // tr, n_out // tn),
        in_specs=[
            pl.BlockSpec((tn, tr), lambda r, n: (n, r)),
            pl.BlockSpec((SUBLANES, tr), lambda r, n: ((n + 1) * per, r)),
        ],
        out_specs=pl.BlockSpec((tr, tn), lambda r, n: (r, n)),
        out_shape=jax.ShapeDtypeStruct((d, n_out), BF16),
        compiler_params=_params(("parallel", "parallel")),
        name="w_prep",
    )(w_t, w_t)


def _in_proj(x, g, w_main, w_t, f_row):
    s, d = x.shape
    n = w_main.shape[1]
    tm = _pick(s, (1024, 512, 256, 128))
    tn = _pick(n, (1024, 512, 256, 128))
    tq = _attn_tile(s)
    rows_b = HEAD_DIM + BF16_ROWS
    tv_a = min(tq, SB_VALUE_TILE)
    vt_spec = lambda rows, tv: pl.BlockSpec((N_HEADS, tm // tv, rows, tv), lambda m, j: (0, m, 0, 0))
    return pl.pallas_call(
        functools.partial(_inproj_kernel, tn=tn),
        grid=(s // tm, n // tn),
        in_specs=[
            pl.BlockSpec((tm, d), lambda m, j: (m, 0)),
            pl.BlockSpec((1, d), lambda m, j: (0, 0)),
            pl.BlockSpec((d, tn), lambda m, j: (0, j)),
            pl.BlockSpec((SUBLANES, d), lambda m, j: (f_row // SUBLANES, 0)),
        ],
        out_specs=[
            pl.BlockSpec((tm, tn), lambda m, j: (m, j)),
            pl.BlockSpec((tm, LANES), lambda m, j: (m, 0)),
            vt_spec(HEAD_DIM, tv_a),
            vt_spec(rows_b, tq),
        ],
        out_shape=[
            jax.ShapeDtypeStruct((s, n), BF16),
            jax.ShapeDtypeStruct((s, LANES), F32),
            jax.ShapeDtypeStruct((N_HEADS, s // tv_a, HEAD_DIM, tv_a), BF16),
            jax.ShapeDtypeStruct((N_HEADS, s // tq, rows_b, tq), BF16),
        ],
        scratch_shapes=[pltpu.VMEM((tm, d), BF16)],
        compiler_params=_params(("parallel", "arbitrary")),
        name="in_proj",
    )(x, g, w_main, w_t)


_NT = (((1,), (1,)), ((), ()))


def _sb_kernel(q_ref, k_ref, vt_ref, o_ref, acc_scr, *, tq, tk, tv, scale):
    i = pl.program_id(1)
    nsub = tq // tk
    per_tv = tv // tk
    q = q_ref[...]
    row = lax.broadcasted_iota(I32, (tk, 2 * tk), 0)
    col = lax.broadcasted_iota(I32, (tk, 2 * tk), 1)
    ut2 = ((col % tk) >= row).astype(BF16)
    diff = lax.broadcasted_iota(I32, (tk, tq), 1) - lax.broadcasted_iota(I32, (tk, tq), 0)

    def sub_block(kb, qs, csum, mask):
        k = k_ref[pl.ds(pl.multiple_of(kb * tk, tk), tk), :]
        z = lax.dot_general(k, qs, _NT, preferred_element_type=F32) * scale
        lk = -(jnp.maximum(z, 0.0) + jnp.log(1.0 + jnp.exp(-jnp.abs(z))))
        if mask is not None:
            lk = jnp.where(mask, lk, 0.0)
        hi = lk.astype(BF16)
        lo = (lk - hi.astype(F32)).astype(BF16)
        cs = jnp.dot(ut2, jnp.concatenate([hi, lo], axis=0), preferred_element_type=F32)
        a = jnp.exp(z + cs + csum)
        if mask is not None:
            a = jnp.where(mask, a, 0.0)
        return a.astype(BF16), cs[0:1, :]

    acc_scr[...] = jnp.zeros_like(acc_scr)
    csum = jnp.zeros((1, tq), F32)
    for u in reversed(range(nsub)):
        lo = u * tk
        a, inc = sub_block(i * nsub + u, q[lo:, :], csum[:, lo:], diff[:, lo:] > lo)
        vt = vt_ref[i * (tq // tv) + u // per_tv, :, pl.ds((u % per_tv) * tk, tk)]
        acc_scr[:, lo:] += jnp.dot(vt, a, preferred_element_type=F32)
        csum = jnp.concatenate([csum[:, :lo], csum[:, lo:] + inc], axis=1) if lo else csum + inc

    def far(g, csum):
        parts = [None] * per_tv
        for u in reversed(range(per_tv)):
            parts[u], inc = sub_block(g * per_tv + u, q, csum, None)
            csum = csum + inc
        acc_scr[...] += jnp.dot(vt_ref[g], jnp.concatenate(parts, axis=0),
                                preferred_element_type=F32)
        return csum

    first_far = i * (tq // tv) - 1

    def live(state):
        n, c = state
        return jnp.logical_and(n <= first_far, jnp.max(c) > -F32_EXP_UNDERFLOW)

    lax.while_loop(live, lambda st: (st[0] + 1, far(first_far - st[0], st[1])),
                   (jnp.int32(0), csum))
    o_ref[...] = acc_scr[...].T.astype(o_ref.dtype)


def _fox_kernel(ff_ref, fl_ref, q_ref, qa_ref, k_ref, ka_ref, vt_ref, o_ref,
                acc_scr, s_scr, kn_smem, *, tq, tk, scale):
    i = pl.program_id(1)
    nsub = tq // tk
    q = jnp.concatenate([q_ref[...], qa_ref[...]], axis=1)
    diff = lax.broadcasted_iota(I32, (tk, tq), 1) - lax.broadcasted_iota(I32, (tk, tq), 0)
    to_log2 = scale * LOG2_E

    def scores(t, slot, masked):
        for u in range(nsub):
            ks = pl.multiple_of((t * nsub + u) * tk, tk)
            k = jnp.concatenate([k_ref[pl.ds(ks, tk), :], ka_ref[pl.ds(ks, tk), :]], axis=1)
            s = lax.dot_general(k, q, _NT, preferred_element_type=F32) * to_log2
            if masked:
                s = jnp.where(diff >= u * tk, s, NEG_BIG)
            s_scr[slot, pl.ds(u * tk, tk), :] = s

    def consume(t, slot, m):
        top = s_scr[slot, pl.ds(0, tk), :]
        for u in range(1, nsub):
            top = jnp.maximum(top, s_scr[slot, pl.ds(u * tk, tk), :])
        m_new = jnp.maximum(m, jnp.max(top, axis=0, keepdims=True))
        alpha = jnp.exp2(m - m_new)
        pcat = jnp.exp2(s_scr[slot] - m_new).astype(BF16)
        acc_scr[...] = alpha * acc_scr[...] + jnp.dot(vt_ref[t], pcat, preferred_element_type=F32)
        return m_new

    def step(t, slot, m):
        scores(t - 1, 1 - slot, False)
        return consume(t, slot, m)

    def max_norm(x):
        xf = x.astype(F32)
        return jnp.max(jnp.sqrt(jnp.sum(xf * xf, axis=1, keepdims=True)))

    kn_prev = jnp.where(i > 0, kn_smem[jnp.maximum(i - 1, 0)], 0.0)
    kn_smem[i] = jnp.maximum(kn_prev, max_norm(k_ref[pl.ds(pl.multiple_of(i * tq, tq), tq), :]))

    acc_scr[...] = jnp.zeros_like(acc_scr)
    scores(i, 0, True)

    @pl.when(i > 0)
    def _():
        scores(i - 1, 1, False)

    m = consume(i, 0, jnp.full((1, tq), NEG_BIG, F32))

    @pl.when(i > 0)
    def _():
        nt = pl.num_programs(1)
        row = pl.program_id(0) * nt
        floor = jnp.min(m) - F32_EXP2_UNDERFLOW
        qk = to_log2 * NORM_SLACK * max_norm(q_ref[...])
        f_first = LOG2_E * ff_ref[row + i]

        def live(extra):
            t = i - 2 - extra
            bound = qk * kn_smem[jnp.maximum(t, 0)] + (
                f_first - LOG2_E * fl_ref[row + jnp.maximum(t, 0)])
            return jnp.logical_and(t >= 0, bound >= floor)

        extra = lax.while_loop(live, lambda e: e + 1, jnp.int32(0))
        last = i - 1 - extra

        def trip(k, m):
            t = i - 1 - 2 * k
            return step(t - 1, 0, step(t, 1, m))

        m2 = lax.fori_loop(0, extra // 2, trip, m)

        @pl.when(extra % 2 == 0)
        def _():
            consume(last, 1, m2)

        @pl.when(extra % 2 == 1)
        def _():
            consume(last, 0, step(last + 1, 1, m2))

    acc = acc_scr[...]
    o_ref[...] = (acc[:HEAD_DIM] / acc[HEAD_DIM:HEAD_DIM + 1]).T.astype(o_ref.dtype)


def _attn_tile(s):
    return _pick(s, (512, 256, 128))


def _attention(p, vt, q_col, k_col, fox=None):
    s = p.shape[0]
    tq = _attn_tile(s)
    tk = 128
    scale = HEAD_DIM ** -0.5
    rows = vt.shape[2]
    q_spec = pl.BlockSpec((tq, HEAD_DIM), lambda h, i, *_: (i, q_col + h))
    k_spec = pl.BlockSpec((s, HEAD_DIM), lambda h, i, *_: (0, k_col + h))
    vt_spec = pl.BlockSpec((None,) + vt.shape[1:], lambda h, i, *_: (h, 0, 0, 0))
    scratch = [pltpu.VMEM((rows, tq), F32)]
    order = ("parallel", "arbitrary")
    if fox is None:
        kern = functools.partial(_sb_kernel, tq=tq, tk=tk, tv=vt.shape[3], scale=scale)
        name = "sb_attn"
        tables = []
        in_specs = [q_spec, k_spec, vt_spec]
        args = [p, p, vt]
    else:
        tables, qa, ka = fox
        scratch.append(pltpu.VMEM((2, tq, tq), F32))
        scratch.append(pltpu.SMEM((s // tq,), F32))
        order = ("arbitrary", "arbitrary")
        kern = functools.partial(_fox_kernel, tq=tq, tk=tk, scale=scale)
        name = "fox_attn"
        in_specs = [
            q_spec,
            pl.BlockSpec((None, tq, HEAD_DIM), lambda h, i, *_: (h, i, 0)),
            k_spec,
            pl.BlockSpec((None, s, HEAD_DIM), lambda h, i, *_: (h, 0, 0)),
            vt_spec,
        ]
        args = [p, qa, p, ka, vt]
    return pl.pallas_call(
        kern,
        grid_spec=pltpu.PrefetchScalarGridSpec(
            num_scalar_prefetch=len(tables),
            grid=(N_HEADS, s // tq),
            in_specs=in_specs,
            out_specs=pl.BlockSpec((tq, HEAD_DIM), lambda h, i, *_: (i, h)),
            scratch_shapes=scratch,
        ),
        out_shape=jax.ShapeDtypeStruct((s, D_HEADS), BF16),
        compiler_params=_params(order),
        name=name,
    )(*tables, *args)


def _fox_tables(cum_f):
    nt = cum_f.shape[1] // _attn_tile(cum_f.shape[1])
    f_tiles = cum_f.reshape(N_HEADS, nt, -1)
    return [f_tiles[:, :, 0].reshape(-1), f_tiles[:, :, -1].reshape(-1)]


def _bf16_trunc(x):
    bits = lax.bitcast_convert_type(x, U32) & jnp.uint32(0xFFFF0000)
    return lax.bitcast_convert_type(bits, F32)


def _split3(x):
    a = _bf16_trunc(x)
    b = _bf16_trunc(x - a)
    c = _bf16_trunc(x - a - b)
    return [a.astype(BF16), b.astype(BF16), c.astype(BF16)]


def _fox_bias(cum_f):
    parts = _split3(cum_f * (HEAD_DIM ** 0.5))
    ones = jnp.ones_like(parts[0])
    zeros = jnp.zeros_like(parts[0])
    widen = jnp.eye(SUBLANES, HEAD_DIM, dtype=BF16)

    def spread(cols):
        packed = jnp.stack(cols + [zeros, zeros], axis=-1)
        return jnp.einsum("hsc,cl->hsl", packed, widen,
                          preferred_element_type=F32).astype(BF16)

    return spread([ones] * 3 + parts), spread([-t for t in parts] + [ones] * 3)


def _bf16_pair_pack(h):
    bits = lax.bitcast_convert_type(h, U32)
    rnd = bits + jnp.uint32(0x7FFF) + ((bits >> 16) & jnp.uint32(1))
    half = h.shape[1] // 2
    return (rnd[:, half:] & jnp.uint32(0xFFFF0000)) | (rnd[:, :half] >> 16)


def _bf16_pair_unpack(w):
    lo = lax.bitcast_convert_type(w << 16, F32)
    hi = lax.bitcast_convert_type(w & jnp.uint32(0xFFFF0000), F32)
    return jnp.concatenate([lo, hi], axis=1).astype(BF16)


def _mix_kernel(osb_ref, ofx_ref, ga_ref, gb_ref, bg_ref, wpa_ref, wpb_ref, wo_ref, x_ref, g_ref,
                wrh_ref, wrl_ref, br_ref,
                x2_ref, hp_ref, ri_ref, rw_ref, cnt_ref, cnt_scr, *, tm, d, cap):
    t = pl.program_id(0)

    @pl.when(t == 0)
    def _():
        cnt_scr[...] = jnp.zeros_like(cnt_scr)

    bg = bg_ref[...]
    pa = jnp.dot(osb_ref[...], wpa_ref[...], preferred_element_type=F32)
    pb = jnp.dot(ofx_ref[...], wpb_ref[...], preferred_element_type=F32)
    ga = jax.nn.sigmoid(ga_ref[...].astype(F32) + bg[:, :d])
    gb = jax.nn.sigmoid(gb_ref[...].astype(F32) + bg[:, d:])
    mixed = (ga * pa + gb * pb).astype(BF16)
    x2 = x_ref[...] + jnp.dot(mixed, wo_ref[...], preferred_element_type=F32)
    x2_ref[...] = x2
    h2 = _rms(x2, g_ref[...])
    hp_ref[...] = _bf16_pair_pack(h2)

    hh = h2.astype(BF16)
    hl = (h2 - hh.astype(F32)).astype(BF16)
    logits = (jnp.dot(hh, wrh_ref[...], preferred_element_type=F32)
              + jnp.dot(hh, wrl_ref[...], preferred_element_type=F32)
              + jnp.dot(hl, wrh_ref[...], preferred_element_type=F32)
              + br_ref[...])

    lane = lax.broadcasted_iota(I32, (tm, LANES), 1).astype(F32)
    work = logits
    sel = jnp.zeros((tm, LANES), F32)
    vals, idxs = [], []
    for _ in range(TOP_K):
        mx = jnp.max(work, axis=1, keepdims=True)
        idx = jnp.min(jnp.where(work == mx, lane, float(LANES)), axis=1, keepdims=True)
        hit = lane == idx
        vals.append(mx)
        idxs.append(idx)
        work = jnp.where(hit, -jnp.inf, work)
        sel = jnp.where(hit, 1.0, sel)
    exps = [jnp.exp(v - vals[0]) for v in vals]
    denom = exps[0] + exps[1] + exps[2] + exps[3]

    row = lax.broadcasted_iota(I32, (tm, tm), 0)
    col = lax.broadcasted_iota(I32, (tm, tm), 1)
    tri = (col < row).astype(BF16)
    rank = jnp.dot(tri, sel.astype(BF16), preferred_element_type=F32) + cnt_scr[...]
    cnt_new = cnt_scr[...] + jnp.sum(sel, axis=0, keepdims=True)
    cnt_scr[...] = cnt_new
    cnt_ref[...] = cnt_new
    slot = rank + lane * float(cap)
    ri = jnp.zeros((tm, LANES), F32)
    rw = jnp.zeros((tm, LANES), F32)
    for r in range(TOP_K):
        slot_r = jnp.sum(jnp.where(lane == idxs[r], slot, 0.0), axis=1, keepdims=True)
        ri = jnp.where(lane == float(r), slot_r, ri)
        rw = jnp.where(lane == float(r), exps[r] / denom, rw)
    ri_ref[...] = ri.astype(I32)
    rw_ref[...] = rw


def _mix_route(o_sb, o_fx, p, gate_col, b_gate, w_pa, w_pb, w_out, x, g_ffn, wr_hi, wr_lo, br, cap):
    s, d = x.shape
    tm = _pick(s, (256, 128))
    const = lambda shape: pl.BlockSpec(shape, lambda t: (0, 0), pipeline_mode=pl.Buffered(1))
    kern = functools.partial(_mix_kernel, tm=tm, d=d, cap=cap)
    return pl.pallas_call(
        kern,
        grid=(s // tm,),
        in_specs=[
            pl.BlockSpec((tm, D_HEADS), lambda t: (t, 0)),
            pl.BlockSpec((tm, D_HEADS), lambda t: (t, 0)),
            pl.BlockSpec((tm, d), lambda t: (t, gate_col)),
            pl.BlockSpec((tm, d), lambda t: (t, gate_col + 1)),
            const((1, 2 * d)),
            const((D_HEADS, d)),
            const((D_HEADS, d)),
            const((d, d)),
            pl.BlockSpec((tm, d), lambda t: (t, 0)),
            const((1, d)),
            const((d, LANES)),
            const((d, LANES)),
            const((1, LANES)),
        ],
        out_specs=[
            pl.BlockSpec((tm, d), lambda t: (t, 0)),
            pl.BlockSpec((tm, d // 2), lambda t: (t, 0)),
            pl.BlockSpec((tm, LANES), lambda t: (t, 0)),
            pl.BlockSpec((tm, LANES), lambda t: (t, 0)),
            pl.BlockSpec((1, LANES), lambda t: (0, 0)),
        ],
        out_shape=[
            jax.ShapeDtypeStruct((s, d), F32),
            jax.ShapeDtypeStruct((s, d // 2), U32),
            jax.ShapeDtypeStruct((s, LANES), I32),
            jax.ShapeDtypeStruct((s, LANES), F32),
            jax.ShapeDtypeStruct((1, LANES), F32),
        ],
        scratch_shapes=[pltpu.VMEM((1, LANES), F32)],
        compiler_params=_params(("arbitrary",)),
        name="mix_route",
    )(o_sb, o_fx, p, p, b_gate, w_pa, w_pb, w_out, x, g_ffn, wr_hi, wr_lo, br)


def _dispatch_kernel(slot_ref, cnt_ref, hp_ref, xs_ref, zero_scr, sem, zsem, *, tm, ch, cap):
    t = pl.program_id(0)
    base = t * tm

    def row_copy(r, k):
        return pltpu.make_async_copy(
            hp_ref.at[pl.ds(r, 1), :],
            xs_ref.at[pl.ds(slot_ref[(base + r) * TOP_K + k], 1), :],
            sem)

    def start(r, c):
        for k in range(TOP_K):
            row_copy(r, k).start()
        return c

    def wait(r, c):
        for k in range(TOP_K):
            row_copy(r, k).wait()
        return c

    @pl.when(t == 0)
    def _():
        zero_scr[...] = jnp.zeros_like(zero_scr)

        def pad_copy(e):
            first = pl.multiple_of(e * cap + (cnt_ref[e] // SUBLANES) * SUBLANES, SUBLANES)
            return pltpu.make_async_copy(zero_scr, xs_ref.at[pl.ds(first, ch), :], zsem)

        def zstart(e, c):
            pad_copy(e).start()
            return c

        def zwait(e, c):
            pad_copy(e).wait()
            return c

        lax.fori_loop(0, N_EXPERTS, zstart, 0)
        lax.fori_loop(0, N_EXPERTS, zwait, 0)

    lax.fori_loop(0, tm, start, 0, unroll=ROW_DMA_UNROLL)
    lax.fori_loop(0, tm, wait, 0, unroll=ROW_DMA_UNROLL)


def _dispatch(slots, counts, hp, cap, ch):
    s, half = hp.shape
    tm = _pick(s, (512, 256, 128))
    kern = functools.partial(_dispatch_kernel, tm=tm, ch=ch, cap=cap)
    return pl.pallas_call(
        kern,
        grid_spec=pltpu.PrefetchScalarGridSpec(
            num_scalar_prefetch=2,
            grid=(s // tm,),
            in_specs=[pl.BlockSpec((tm, half), lambda t, sl, cn: (t, 0))],
            out_specs=pl.BlockSpec(memory_space=pl.ANY),
            scratch_shapes=[
                pltpu.VMEM((ch, half), U32),
                pltpu.SemaphoreType.DMA(()),
                pltpu.SemaphoreType.DMA(()),
            ],
        ),
        out_shape=jax.ShapeDtypeStruct((N_EXPERTS * cap, half), U32),
        compiler_params=_params(("arbitrary",)),
        name="dispatch",
    )(slots, counts, hp)


def _expert_kernel(sbe_ref, sbblk_ref, sbnch_ref, x_ref, w1a_ref, w1c_ref, b1_ref, w2a_ref, w2c_ref,
                   b2_ref, y_ref, xb_scr, w1b_scr, w2b_scr, *, ch, ff):
    s = pl.program_id(0)
    j = pl.program_id(1)
    nch = sbnch_ref[s]
    half = LANES // 2

    def half_act(g):
        lin = pltpu.roll(g, LANES - 1, 1)
        gate = jnp.minimum(g, SWIGLU_LIMIT)
        lin = jnp.clip(lin, -SWIGLU_LIMIT, SWIGLU_LIMIT)
        return gate * jax.nn.sigmoid(SWIGLU_ALPHA * gate) * (lin + 1.0)

    @pl.when(nch > 0)
    def _():
        @pl.when(j == 0)
        def _():
            xb_scr[...] = _bf16_pair_unpack(x_ref[...])

            def init(c, carry):
                y_ref[pl.ds(pl.multiple_of(c * ch, ch), ch), :] = jnp.broadcast_to(
                    b2_ref[...], (ch, y_ref.shape[1]))
                return carry

            lax.fori_loop(0, nch, init, 0)

        def prep():
            w1b_scr[:, :ff] = w1a_ref[...].astype(BF16)
            w1b_scr[:, ff:] = w1c_ref[...].astype(BF16)
            prow = lax.broadcasted_iota(I32, (LANES, LANES), 0)
            pcol = lax.broadcasted_iota(I32, (LANES, LANES), 1)
            perm = (pcol == prow // 2 + half * (prow % 2)).astype(BF16)
            groups = ff // LANES
            for u in range(groups):
                src_ref, v = (w2a_ref, u) if u < groups // 2 else (w2c_ref, u - groups // 2)
                grp = src_ref[pl.ds(v * LANES, LANES), :].astype(BF16)
                w2b_scr[pl.ds(u * LANES, LANES), :] = jnp.dot(
                    perm, grp, preferred_element_type=F32).astype(BF16)

        def up(r0, rows):
            gu = jnp.dot(xb_scr[pl.ds(r0, rows), :], w1b_scr[...],
                         preferred_element_type=F32) + b1_ref[...]
            even = (lax.broadcasted_iota(I32, (rows, LANES), 1) % 2) == 0
            pieces = []
            for u in range(ff // LANES):
                a = half_act(gu[:, (2 * u) * LANES:(2 * u + 1) * LANES])
                b = half_act(gu[:, (2 * u + 1) * LANES:(2 * u + 2) * LANES])
                pieces.append(jnp.where(even, a, pltpu.roll(b, 1, 1)))
            return jnp.concatenate(pieces, axis=1).astype(BF16)

        def down(r0, rows, act):
            y_ref[pl.ds(r0, rows), :] += jnp.dot(act, w2b_scr[...], preferred_element_type=F32)

        full = xb_scr.shape[0] // ch
        for n in range(full - WHOLE_RUN_SIZES + 1, full + 1):
            @pl.when(nch == n)
            def _(n=n):
                prep()
                down(0, n * ch, up(0, n * ch))

        big = 2 * ch
        nbig = nch // 2
        short = nch <= full - WHOLE_RUN_SIZES

        @pl.when(short)
        def _():
            prep()

        @pl.when(jnp.logical_and(short, nbig > 0))
        def _():
            def trip(c, act):
                nxt = up(pl.multiple_of(c * big, big), big)
                down(pl.multiple_of((c - 1) * big, big), big, act)
                return nxt

            act = lax.fori_loop(1, nbig, trip, up(0, big))
            down(pl.multiple_of((nbig - 1) * big, big), big, act)

        @pl.when(jnp.logical_and(short, nch % 2 == 1))
        def _():
            r0 = pl.multiple_of(nbig * big, ch)
            down(r0, ch, up(r0, ch))


def _experts(sb_e, sb_blk, sb_nch, xs, w1, b1, w2, b2, r_rows, ch, cap):
    ne, d, ff2 = w1.shape
    dff = ff2 // 2
    ff = _pick(dff, (256, 128))
    nj = dff // ff
    ns = sb_e.shape[0]
    half = d // 2
    blocks_per_expert = cap // r_rows

    def jmap(s, j, nch):
        return jnp.where(nch[s] > 0, j, nj - 1)

    kern = functools.partial(_expert_kernel, ch=ch, ff=ff)
    return pl.pallas_call(
        kern,
        grid_spec=pltpu.PrefetchScalarGridSpec(
            num_scalar_prefetch=3,
            grid=(ns, nj),
            in_specs=[
                pl.BlockSpec((r_rows, half),
                             lambda s, j, e, b, n: (e[s] * blocks_per_expert + b[s], 0)),
                pl.BlockSpec((None, d, ff), lambda s, j, e, b, n: (e[s], 0, 2 * jmap(s, j, n))),
                pl.BlockSpec((None, d, ff), lambda s, j, e, b, n: (e[s], 0, 2 * jmap(s, j, n) + 1)),
                pl.BlockSpec((None, 1, 2 * ff), lambda s, j, e, b, n: (e[s], 0, jmap(s, j, n))),
                pl.BlockSpec((None, ff // 2, d), lambda s, j, e, b, n: (e[s], 2 * jmap(s, j, n), 0)),
                pl.BlockSpec((None, ff // 2, d),
                             lambda s, j, e, b, n: (e[s], 2 * jmap(s, j, n) + 1, 0)),
                pl.BlockSpec((None, 1, d), lambda s, j, e, b, n: (e[s], 0, 0)),
            ],
            out_specs=pl.BlockSpec((r_rows, d),
                                   lambda s, j, e, b, n: (e[s] * blocks_per_expert + b[s], 0)),
            scratch_shapes=[
                pltpu.VMEM((r_rows, d), BF16),
                pltpu.VMEM((d, 2 * ff), BF16),
                pltpu.VMEM((ff, d), BF16),
            ],
        ),
        out_shape=jax.ShapeDtypeStruct((ne * cap, d), F32),
        compiler_params=_params(("arbitrary", "arbitrary")),
        name="experts",
    )(sb_e, sb_blk, sb_nch, xs, w1, w1, b1.reshape(ne, 1, ff2), w2, w2, b2.reshape(ne, 1, d))


def _combine_kernel(slot_ref, x2_ref, rw_ref, g_ref, ys_ref, o_ref, gbuf, sem, *, tm, final_norm):
    t = pl.program_id(0)
    base = t * tm

    def row_copy(r, k):
        return pltpu.make_async_copy(
            ys_ref.at[pl.ds(slot_ref[(base + r) * TOP_K + k], 1), :],
            gbuf.at[pl.ds(k * tm + r, 1), :],
            sem)

    def start(r, c):
        for k in range(TOP_K):
            row_copy(r, k).start()
        return c

    def wait(r, c):
        for k in range(TOP_K):
            row_copy(r, k).wait()
        return c

    lax.fori_loop(0, tm, start, 0, unroll=ROW_DMA_UNROLL)
    lax.fori_loop(0, tm, wait, 0, unroll=ROW_DMA_UNROLL)

    rw = rw_ref[...]
    out = x2_ref[...]
    for k in range(TOP_K):
        out = out + rw[:, k:k + 1] * gbuf[pl.ds(k * tm, tm), :]
    if final_norm:
        out = _rms(out, g_ref[...])
    o_ref[...] = out


def _combine(slots, x2, rw, g, ys, final_norm):
    s, d = x2.shape
    tm = _pick(s, (256, 128))
    kern = functools.partial(_combine_kernel, tm=tm, final_norm=final_norm)
    return pl.pallas_call(
        kern,
        grid_spec=pltpu.PrefetchScalarGridSpec(
            num_scalar_prefetch=1,
            grid=(s // tm,),
            in_specs=[
                pl.BlockSpec((tm, d), lambda t, sl: (t, 0)),
                pl.BlockSpec((tm, LANES), lambda t, sl: (t, 0)),
                pl.BlockSpec((1, d), lambda t, sl: (0, 0)),
                pl.BlockSpec(memory_space=pl.ANY),
            ],
            out_specs=pl.BlockSpec((tm, d), lambda t, sl: (t, 0)),
            scratch_shapes=[pltpu.VMEM((TOP_K * tm, d), F32), pltpu.SemaphoreType.DMA(())],
        ),
        out_shape=jax.ShapeDtypeStruct((s, d), F32),
        compiler_params=_params(("arbitrary",)),
        name="combine",
    )(slots, x2, rw, g, ys)


def _superblocks(counts, r_rows, ch, ns):
    nsb = (counts + r_rows - 1) // r_rows
    ends = jnp.cumsum(nsb)
    total = ends[-1]
    sidx = jnp.arange(ns, dtype=I32)
    last = jnp.maximum(total - 1, 0)
    sclip = jnp.minimum(sidx, last)
    e = jnp.minimum(jnp.searchsorted(ends, sclip, side="right"), N_EXPERTS - 1).astype(I32)
    blk = sclip - (ends[e] - nsb[e])
    rows = jnp.clip(counts[e] - blk * r_rows, 0, r_rows)
    nch = jnp.where(sidx < total, (rows + ch - 1) // ch, 0)
    return e, blk.astype(I32), nch.astype(I32)


def _layer(layer, x, g_mix, w_in, b_forget, b_gate, w_pa, w_pb, w_out, g_ffn,
           w_router, b_router, w1, b1, w2, b2, g_out, final_norm):
    s, d = x.shape
    qkv = 6 * D_HEADS

    assert N_HEADS == SUBLANES
    w_t = jnp.swapaxes(w_in, 1, 2)[layer]
    w_main = _w_prep(w_t, qkv)
    p, f_log, vt_sb, vt_fx = _in_proj(x, g_mix.reshape(1, d), w_main, w_t, qkv)

    log_f = jax.nn.log_sigmoid(f_log[:, :N_HEADS] + b_forget)
    cum_f = jnp.cumsum(log_f, axis=0).T

    nb = D_HEADS // HEAD_DIM
    o_sb = _attention(p, vt_sb, 0, nb)
    o_fx = _attention(p, vt_fx, 3 * nb, 4 * nb, fox=(_fox_tables(cum_f), *_fox_bias(cum_f)))

    r_rows = 1280
    ch = 128
    cap = -(-(s + ch) // r_rows) * r_rows
    ns =(s * TOP_K) // r_rows + N_EXPERTS
    wr = jnp.pad(w_router, ((0, 0), (0, LANES - N_EXPERTS)))
    wr_top = _bf16_trunc(wr)
    wr_hi = wr_top.astype(BF16)
    wr_lo = (wr - wr_top).astype(BF16)
    br = jnp.pad(b_router, (0, LANES - N_EXPERTS), constant_values=NEG_BIG).reshape(1, LANES)
    x2, hp, ri, rw, cnt = _mix_route(
        o_sb, o_fx, p, qkv // d, b_gate.reshape(1, 2 * d), w_pa.astype(BF16), w_pb.astype(BF16),
        w_out.astype(BF16), x, g_ffn.reshape(1, d), wr_hi, wr_lo, br, cap)

    slots = ri[:, :TOP_K].reshape(-1)
    counts = cnt[0, :N_EXPERTS].astype(I32)
    xs = _dispatch(slots, counts, hp, cap, ch)
    sb_e, sb_blk, sb_nch = _superblocks(counts, r_rows, ch, ns)
    ys = _experts(sb_e, sb_blk, sb_nch, xs, w1, b1, w2, b2, r_rows, ch, cap)
    return _combine(slots, x2, rw, g_out.reshape(1, d), ys, final_norm)


def kernel(x, g_mix, w_in, b_forget, b_gate, w_proj_sb, w_proj_fox, w_out, g_ffn, w_router,
           b_router, w_mlp1, b_mlp1, w_mlp2, b_mlp2, g_final):
    b, s, d = x.shape
    depth = g_mix.shape[0]
    outs = []
    for bi in range(b):
        xb = x[bi]
        for layer in range(depth):
            last = layer == depth - 1
            xb = _layer(layer, xb, g_mix[layer], w_in, b_forget[layer], b_gate[layer],
                        w_proj_sb[layer], w_proj_fox[layer], w_out[layer], g_ffn[layer],
                        w_router[layer], b_router[layer], w_mlp1[layer], b_mlp1[layer],
                        w_mlp2[layer], b_mlp2[layer], g_final, last)
        outs.append(xb.reshape(1, s, d))
    return outs[0] if b == 1 else jnp.concatenate(outs, axis=0)
```

```python
import functools

import jax
import jax.numpy as jnp
from jax import lax
from jax.experimental import pallas as pl
from jax.experimental.pallas import tpu as pltpu

F32 = jnp.float32
BF16 = jnp.bfloat16
U32 = jnp.uint32
I32 = jnp.int32

LANES = 128
SUBLANES = 8
HEAD_DIM = 128
N_HEADS = 8
D_HEADS = N_HEADS * HEAD_DIM
N_EXPERTS = 32
TOP_K = 4
SWIGLU_LIMIT = 7.0
SWIGLU_ALPHA = 1.702
RMS_EPS = 1e-5
NEG_BIG = -1e30
WHOLE_RUN_SIZES = 3
SB_VALUE_TILE = 256
ROW_DMA_UNROLL = 8
LOG2_E = 1.4426950408889634
BF16_ROWS = 16
F32_EXP2_UNDERFLOW = 152.0
NORM_SLACK = 1.001
F32_EXP_UNDERFLOW = 105.0
VMEM_LIMIT = 56 * 1024 * 1024


def _pick(n, candidates):
    for c in candidates:
        if n % c == 0:
            return c
    raise ValueError(f"no tile in {candidates} divides {n}")


def _params(sem, vmem=VMEM_LIMIT):
    return pltpu.CompilerParams(dimension_semantics=sem, vmem_limit_bytes=vmem)


def _rms(x, g):
    ms = jnp.mean(x * x, axis=-1, keepdims=True)
    return x * lax.rsqrt(ms + RMS_EPS) * g


def _inproj_kernel(x_ref, g_ref, w_ref, wf_ref, p_ref, f_ref, vta_ref, vtb_ref, h_scr, *, tn):
    n = pl.program_id(1)

    @pl.when(n == 0)
    def _():
        hb = _rms(x_ref[...], g_ref[...]).astype(BF16)
        h_scr[...] = hb
        wf = jnp.concatenate(
            [wf_ref[...], jnp.zeros((LANES - SUBLANES, wf_ref.shape[1]), F32)], axis=0).astype(BF16)
        f_ref[...] = lax.dot_general(hb, wf, _NT, preferred_element_type=F32)

    res = jnp.dot(h_scr[...], w_ref[...], preferred_element_type=F32)
    p_ref[...] = res.astype(BF16)

    heads_per_tile = tn // HEAD_DIM
    for section, vt_ref in ((2, vta_ref), (5, vtb_ref)):
        for tile in range(section * D_HEADS // tn, (section + 1) * D_HEADS // tn):
            @pl.when(n == tile)
            def _(vt_ref=vt_ref, head0=(tile * tn - section * D_HEADS) // HEAD_DIM):
                tv = vt_ref.shape[3]
                for hh in range(heads_per_tile):
                    vt = res[:, hh * HEAD_DIM:(hh + 1) * HEAD_DIM].T.astype(BF16)
                    for c in range(vt.shape[1] // tv):
                        vt_ref[head0 + hh, c, pl.ds(0, HEAD_DIM), :] = vt[:, c * tv:(c + 1) * tv]
                        if vt_ref.shape[2] > HEAD_DIM:
                            extra = lax.broadcasted_iota(I32, (BF16_ROWS, tv), 0) == 0
                            vt_ref[head0 + hh, c, pl.ds(HEAD_DIM, BF16_ROWS), :] = extra.astype(BF16)


def _wprep_kernel(a_ref, b_ref, o_ref, *, first_tail):
    n = pl.program_id(1)

    @pl.when(n < first_tail)
    def _():
        o_ref[...] = a_ref[...].T.astype(BF16)

    @pl.when(n >= first_tail)
    def _():
        ext = jnp.concatenate([a_ref[...], b_ref[...]], axis=0)
        o_ref[...] = ext[SUBLANES:, :].T.astype(BF16)


def _w_prep(w_t, qkv):
    n_in, d = w_t.shape
    n_out = n_in - SUBLANES
    tn = _pick(n_out, (1024, 512, 256, 128))
    tr = _pick(d, (512, 256, 128))
    per = tn // SUBLANES
    kern = functools.partial(_wprep_kernel, first_tail=qkv // tn)
    return pl.pallas_call(
        kern,
        grid=(d // tr, n_out // tn),
        in_specs=[
            pl.BlockSpec((tn, tr), lambda r, n: (n, r)),
            pl.BlockSpec((SUBLANES, tr), lambda r, n: ((n + 1) * per, r)),
        ],
        out_specs=pl.BlockSpec((tr, tn), lambda r, n: (r, n)),
        out_shape=jax.ShapeDtypeStruct((d, n_out), BF16),
        compiler_params=_params(("parallel", "parallel")),
        name="w_prep",
    )(w_t, w_t)


def _in_proj(x, g, w_main, w_t, f_row):
    s, d = x.shape
    n = w_main.shape[1]
    tm = _pick(s, (1024, 512, 256, 128))
    tn = _pick(n, (1024, 512, 256, 128))
    tq = _attn_tile(s)
    rows_b = HEAD_DIM + BF16_ROWS
    tv_a = min(tq, SB_VALUE_TILE)
    vt_spec = lambda rows, tv: pl.BlockSpec((N_HEADS, tm // tv, rows, tv), lambda m, j: (0, m, 0, 0))
    return pl.pallas_call(
        functools.partial(_inproj_kernel, tn=tn),
        grid=(s // tm, n // tn),
        in_specs=[
            pl.BlockSpec((tm, d), lambda m, j: (m, 0)),
            pl.BlockSpec((1, d), lambda m, j: (0, 0)),
            pl.BlockSpec((d, tn), lambda m, j: (0, j)),
            pl.BlockSpec((SUBLANES, d), lambda m, j: (f_row // SUBLANES, 0)),
        ],
        out_specs=[
            pl.BlockSpec((tm, tn), lambda m, j: (m, j)),
            pl.BlockSpec((tm, LANES), lambda m, j: (m, 0)),
            vt_spec(HEAD_DIM, tv_a),
            vt_spec(rows_b, tq),
        ],
        out_shape=[
            jax.ShapeDtypeStruct((s, n), BF16),
            jax.ShapeDtypeStruct((s, LANES), F32),
            jax.ShapeDtypeStruct((N_HEADS, s // tv_a, HEAD_DIM, tv_a), BF16),
            jax.ShapeDtypeStruct((N_HEADS, s // tq, rows_b, tq), BF16),
        ],
        scratch_shapes=[pltpu.VMEM((tm, d), BF16)],
        compiler_params=_params(("parallel", "arbitrary")),
        name="in_proj",
    )(x, g, w_main, w_t)


_NT = (((1,), (1,)), ((), ()))


def _sb_kernel(q_ref, k_ref, vt_ref, o_ref, acc_scr, *, tq, tk, tv, scale):
    i = pl.program_id(1)
    nsub = tq // tk
    per_tv = tv // tk
    q = q_ref[...]
    row = lax.broadcasted_iota(I32, (tk, 2 * tk), 0)
    col = lax.broadcasted_iota(I32, (tk, 2 * tk), 1)
    ut2 = ((col % tk) >= row).astype(BF16)
    diff = lax.broadcasted_iota(I32, (tk, tq), 1) - lax.broadcasted_iota(I32, (tk, tq), 0)

    def sub_block(kb, qs, csum, mask):
        k = k_ref[pl.ds(pl.multiple_of(kb * tk, tk), tk), :]
        z = lax.dot_general(k, qs, _NT, preferred_element_type=F32) * scale
        lk = -(jnp.maximum(z, 0.0) + jnp.log(1.0 + jnp.exp(-jnp.abs(z))))
        if mask is not None:
            lk = jnp.where(mask, lk, 0.0)
        hi = lk.astype(BF16)
        lo = (lk - hi.astype(F32)).astype(BF16)
        cs = jnp.dot(ut2, jnp.concatenate([hi, lo], axis=0), preferred_element_type=F32)
        a = jnp.exp(z + cs + csum)
        if mask is not None:
            a = jnp.where(mask, a, 0.0)
        return a.astype(BF16), cs[0:1, :]

    acc_scr[...] = jnp.zeros_like(acc_scr)
    csum = jnp.zeros((1, tq), F32)
    for u in reversed(range(nsub)):
        lo = u * tk
        a, inc = sub_block(i * nsub + u, q[lo:, :], csum[:, lo:], diff[:, lo:] > lo)
        vt = vt_ref[i * (tq // tv) + u // per_tv, :, pl.ds((u % per_tv) * tk, tk)]
        acc_scr[:, lo:] += jnp.dot(vt, a, preferred_element_type=F32)
        csum = jnp.concatenate([csum[:, :lo], csum[:, lo:] + inc], axis=1) if lo else csum + inc

    def far(g, csum, mask=None):
        parts = [None] * per_tv
        for u in reversed(range(per_tv)):
            parts[u], inc = sub_block(g * per_tv + u, q, csum, mask)
            csum = csum + inc
        acc_scr[...] += jnp.dot(vt_ref[g], jnp.concatenate(parts, axis=0),
                                preferred_element_type=F32)
        return csum

    first_far = i * (tq // tv) - 1

    def live(state):
        n, c = state
        return jnp.logical_and(n <= first_far, jnp.max(c) > -F32_EXP_UNDERFLOW)

    any_far = jnp.broadcast_to(i > 0, (tk, tq))
    csum = far(jnp.maximum(first_far, 0), csum, any_far)
    lax.while_loop(live, lambda st: (st[0] + 1, far(first_far - st[0], st[1])),
                   (jnp.int32(1), csum))
    o_ref[...] = acc_scr[...].T.astype(o_ref.dtype)


def _fox_kernel(ff_ref, fl_ref, q_ref, qa_ref, k_ref, ka_ref, vt_ref, o_ref,
                acc_scr, s_scr, kn_smem, *, tq, tk, scale):
    i = pl.program_id(1)
    nsub = tq // tk
    q = jnp.concatenate([q_ref[...], qa_ref[...]], axis=1)
    diff = lax.broadcasted_iota(I32, (tk, tq), 1) - lax.broadcasted_iota(I32, (tk, tq), 0)
    to_log2 = scale * LOG2_E

    def scores(t, slot, masked):
        for u in range(nsub):
            ks = pl.multiple_of((t * nsub + u) * tk, tk)
            k = jnp.concatenate([k_ref[pl.ds(ks, tk), :], ka_ref[pl.ds(ks, tk), :]], axis=1)
            lo = u * tk if masked else 0
            s = lax.dot_general(k, q[lo:, :], _NT, preferred_element_type=F32) * to_log2
            if masked:
                s = jnp.where(diff[:, lo:] >= lo, s, NEG_BIG)
                if lo:
                    s_scr[slot, pl.ds(u * tk, tk), pl.ds(0, lo)] = jnp.full((tk, lo), NEG_BIG, F32)
            s_scr[slot, pl.ds(u * tk, tk), pl.ds(lo, tq - lo)] = s

    def consume(t, slot, m):
        top = s_scr[slot, pl.ds(0, tk), :]
        for u in range(1, nsub):
            top = jnp.maximum(top, s_scr[slot, pl.ds(u * tk, tk), :])
        m_new = jnp.maximum(m, jnp.max(top, axis=0, keepdims=True))
        alpha = jnp.exp2(m - m_new)
        pcat = jnp.exp2(s_scr[slot] - m_new).astype(BF16)
        acc_scr[...] = alpha * acc_scr[...] + jnp.dot(vt_ref[t], pcat, preferred_element_type=F32)
        return m_new

    def step(t, slot, m):
        scores(t - 1, 1 - slot, False)
        return consume(t, slot, m)

    def max_norm(x):
        xf = x.astype(F32)
        return jnp.max(jnp.sqrt(jnp.sum(xf * xf, axis=1, keepdims=True)))

    kn_prev = jnp.where(i > 0, kn_smem[jnp.maximum(i - 1, 0)], 0.0)
    kn_smem[i] = jnp.maximum(kn_prev, max_norm(k_ref[pl.ds(pl.multiple_of(i * tq, tq), tq), :]))

    acc_scr[...] = jnp.zeros_like(acc_scr)
    scores(i, 0, True)
    scores(jnp.maximum(i - 1, 0), 1, False)
    m = consume(i, 0, jnp.full((1, tq), NEG_BIG, F32))

    @pl.when(i > 0)
    def _():
        nt = pl.num_programs(1)
        row = pl.program_id(0) * nt
        floor = jnp.min(m) - F32_EXP2_UNDERFLOW
        qk = to_log2 * NORM_SLACK * max_norm(q_ref[...])
        f_first = LOG2_E * ff_ref[row + i]

        def live(extra):
            t = i - 2 - extra
            bound = qk * kn_smem[jnp.maximum(t, 0)] + (
                f_first - LOG2_E * fl_ref[row + jnp.maximum(t, 0)])
            return jnp.logical_and(t >= 0, bound >= floor)

        extra = lax.while_loop(live, lambda e: e + 1, jnp.int32(0))
        last = i - 1 - extra

        def trip(k, m):
            t = i - 1 - 2 * k
            return step(t - 1, 0, step(t, 1, m))

        m2 = lax.fori_loop(0, extra // 2, trip, m)

        @pl.when(extra % 2 == 0)
        def _():
            consume(last, 1, m2)

        @pl.when(extra % 2 == 1)
        def _():
            consume(last, 0, step(last + 1, 1, m2))

    acc = acc_scr[...]
    o_ref[...] = (acc[:HEAD_DIM] / acc[HEAD_DIM:HEAD_DIM + 1]).T.astype(o_ref.dtype)


def _attn_tile(s):
    return _pick(s, (512, 256, 128))


def _attention(p, vt, q_col, k_col, fox=None):
    s = p.shape[0]
    tq = _attn_tile(s)
    tk = 128
    scale = HEAD_DIM ** -0.5
    rows = vt.shape[2]
    q_spec = pl.BlockSpec((tq, HEAD_DIM), lambda h, i, *_: (i, q_col + h))
    k_spec = pl.BlockSpec((s, HEAD_DIM), lambda h, i, *_: (0, k_col + h))
    vt_spec = pl.BlockSpec((None,) + vt.shape[1:], lambda h, i, *_: (h, 0, 0, 0))
    scratch = [pltpu.VMEM((rows, tq), F32)]
    order = ("parallel", "arbitrary")
    if fox is None:
        kern = functools.partial(_sb_kernel, tq=tq, tk=tk, tv=vt.shape[3], scale=scale)
        name = "sb_attn"
        tables = []
        in_specs = [q_spec, k_spec, vt_spec]
        args = [p, p, vt]
    else:
        tables, qa, ka = fox
        scratch.append(pltpu.VMEM((2, tq, tq), F32))
        scratch.append(pltpu.SMEM((s // tq,), F32))
        order = ("arbitrary", "arbitrary")
        kern = functools.partial(_fox_kernel, tq=tq, tk=tk, scale=scale)
        name = "fox_attn"
        in_specs = [
            q_spec,
            pl.BlockSpec((None, tq, HEAD_DIM), lambda h, i, *_: (h, i, 0)),
            k_spec,
            pl.BlockSpec((None, s, HEAD_DIM), lambda h, i, *_: (h, 0, 0)),
            vt_spec,
        ]
        args = [p, qa, p, ka, vt]
    return pl.pallas_call(
        kern,
        grid_spec=pltpu.PrefetchScalarGridSpec(
            num_scalar_prefetch=len(tables),
            grid=(N_HEADS, s // tq),
            in_specs=in_specs,
            out_specs=pl.BlockSpec((tq, HEAD_DIM), lambda h, i, *_: (i, h)),
            scratch_shapes=scratch,
        ),
        out_shape=jax.ShapeDtypeStruct((s, D_HEADS), BF16),
        compiler_params=_params(order),
        name=name,
    )(*tables, *args)


def _fox_tables(cum_f):
    nt = cum_f.shape[1] // _attn_tile(cum_f.shape[1])
    f_tiles = cum_f.reshape(N_HEADS, nt, -1)
    return [f_tiles[:, :, 0].reshape(-1), f_tiles[:, :, -1].reshape(-1)]


def _bf16_trunc(x):
    bits = lax.bitcast_convert_type(x, U32) & jnp.uint32(0xFFFF0000)
    return lax.bitcast_convert_type(bits, F32)


def _split3(x):
    a = _bf16_trunc(x)
    b = _bf16_trunc(x - a)
    c = _bf16_trunc(x - a - b)
    return [a.astype(BF16), b.astype(BF16), c.astype(BF16)]


def _fox_bias(cum_f):
    parts = _split3(cum_f * (HEAD_DIM ** 0.5))
    ones = jnp.ones_like(parts[0])
    zeros = jnp.zeros_like(parts[0])
    widen = jnp.eye(SUBLANES, HEAD_DIM, dtype=BF16)

    def spread(cols):
        packed = jnp.stack(cols + [zeros, zeros], axis=-1)
        return jnp.einsum("hsc,cl->hsl", packed, widen,
                          preferred_element_type=F32).astype(BF16)

    return spread([ones] * 3 + parts), spread([-t for t in parts] + [ones] * 3)


def _bf16_pair_pack(h):
    bits = lax.bitcast_convert_type(h, U32)
    rnd = bits + jnp.uint32(0x7FFF) + ((bits >> 16) & jnp.uint32(1))
    half = h.shape[1] // 2
    return (rnd[:, half:] & jnp.uint32(0xFFFF0000)) | (rnd[:, :half] >> 16)


def _bf16_pair_unpack(w):
    lo = lax.bitcast_convert_type(w << 16, F32)
    hi = lax.bitcast_convert_type(w & jnp.uint32(0xFFFF0000), F32)
    return jnp.concatenate([lo, hi], axis=1).astype(BF16)


def _mix_kernel(osb_ref, ofx_ref, ga_ref, gb_ref, bg_ref, wpa_ref, wpb_ref, wo_ref, x_ref, g_ref,
                wrh_ref, wrl_ref, br_ref,
                x2_ref, hp_ref, ri_ref, rw_ref, cnt_ref, cnt_scr, *, tm, d, cap):
    t = pl.program_id(0)

    @pl.when(t == 0)
    def _():
        cnt_scr[...] = jnp.zeros_like(cnt_scr)

    bg = bg_ref[...]
    pa = jnp.dot(osb_ref[...], wpa_ref[...], preferred_element_type=F32)
    pb = jnp.dot(ofx_ref[...], wpb_ref[...], preferred_element_type=F32)
    ga = jax.nn.sigmoid(ga_ref[...].astype(F32) + bg[:, :d])
    gb = jax.nn.sigmoid(gb_ref[...].astype(F32) + bg[:, d:])
    mixed = (ga * pa + gb * pb).astype(BF16)
    x2 = x_ref[...] + jnp.dot(mixed, wo_ref[...], preferred_element_type=F32)
    x2_ref[...] = x2
    h2 = _rms(x2, g_ref[...])
    hp_ref[...] = _bf16_pair_pack(h2)

    hh = h2.astype(BF16)
    hl = (h2 - hh.astype(F32)).astype(BF16)
    logits = (jnp.dot(hh, wrh_ref[...], preferred_element_type=F32)
              + jnp.dot(hh, wrl_ref[...], preferred_element_type=F32)
              + jnp.dot(hl, wrh_ref[...], preferred_element_type=F32)
              + br_ref[...])

    lane = lax.broadcasted_iota(I32, (tm, LANES), 1).astype(F32)
    work = logits
    sel = jnp.zeros((tm, LANES), F32)
    vals, idxs = [], []
    for _ in range(TOP_K):
        mx = jnp.max(work, axis=1, keepdims=True)
        idx = jnp.min(jnp.where(work == mx, lane, float(LANES)), axis=1, keepdims=True)
        hit = lane == idx
        vals.append(mx)
        idxs.append(idx)
        work = jnp.where(hit, -jnp.inf, work)
        sel = jnp.where(hit, 1.0, sel)
    exps = [jnp.exp(v - vals[0]) for v in vals]
    denom = exps[0] + exps[1] + exps[2] + exps[3]

    row = lax.broadcasted_iota(I32, (tm, tm), 0)
    col = lax.broadcasted_iota(I32, (tm, tm), 1)
    tri = (col < row).astype(BF16)
    rank = jnp.dot(tri, sel.astype(BF16), preferred_element_type=F32) + cnt_scr[...]
    cnt_new = cnt_scr[...] + jnp.sum(sel, axis=0, keepdims=True)
    cnt_scr[...] = cnt_new
    cnt_ref[...] = cnt_new
    slot = rank + lane * float(cap)
    ri = jnp.zeros((tm, LANES), F32)
    rw = jnp.zeros((tm, LANES), F32)
    for r in range(TOP_K):
        slot_r = jnp.sum(jnp.where(lane == idxs[r], slot, 0.0), axis=1, keepdims=True)
        ri = jnp.where(lane == float(r), slot_r, ri)
        rw = jnp.where(lane == float(r), exps[r] / denom, rw)
    ri_ref[...] = ri.astype(I32)
    rw_ref[...] = rw


def _mix_route(o_sb, o_fx, p, gate_col, b_gate, w_pa, w_pb, w_out, x, g_ffn, wr_hi, wr_lo, br, cap):
    s, d = x.shape
    tm = _pick(s, (256, 128))
    const = lambda shape: pl.BlockSpec(shape, lambda t: (0, 0), pipeline_mode=pl.Buffered(1))
    kern = functools.partial(_mix_kernel, tm=tm, d=d, cap=cap)
    return pl.pallas_call(
        kern,
        grid=(s // tm,),
        in_specs=[
            pl.BlockSpec((tm, D_HEADS), lambda t: (t, 0)),
            pl.BlockSpec((tm, D_HEADS), lambda t: (t, 0)),
            pl.BlockSpec((tm, d), lambda t: (t, gate_col)),
            pl.BlockSpec((tm, d), lambda t: (t, gate_col + 1)),
            const((1, 2 * d)),
            const((D_HEADS, d)),
            const((D_HEADS, d)),
            const((d, d)),
            pl.BlockSpec((tm, d), lambda t: (t, 0)),
            const((1, d)),
            const((d, LANES)),
            const((d, LANES)),
            const((1, LANES)),
        ],
        out_specs=[
            pl.BlockSpec((tm, d), lambda t: (t, 0)),
            pl.BlockSpec((tm, d // 2), lambda t: (t, 0)),
            pl.BlockSpec((tm, LANES), lambda t: (t, 0)),
            pl.BlockSpec((tm, LANES), lambda t: (t, 0)),
            pl.BlockSpec((1, LANES), lambda t: (0, 0)),
        ],
        out_shape=[
            jax.ShapeDtypeStruct((s, d), F32),
            jax.ShapeDtypeStruct((s, d // 2), U32),
            jax.ShapeDtypeStruct((s, LANES), I32),
            jax.ShapeDtypeStruct((s, LANES), F32),
            jax.ShapeDtypeStruct((1, LANES), F32),
        ],
        scratch_shapes=[pltpu.VMEM((1, LANES), F32)],
        compiler_params=_params(("arbitrary",)),
        name="mix_route",
    )(o_sb, o_fx, p, p, b_gate, w_pa, w_pb, w_out, x, g_ffn, wr_hi, wr_lo, br)


def _dispatch_kernel(slot_ref, cnt_ref, hp_ref, xs_ref, zero_scr, sem, zsem, *, tm, ch, cap):
    t = pl.program_id(0)
    base = t * tm

    def row_copy(r, k):
        return pltpu.make_async_copy(
            hp_ref.at[pl.ds(r, 1), :],
            xs_ref.at[pl.ds(slot_ref[(base + r) * TOP_K + k], 1), :],
            sem)

    def start(r, c):
        for k in range(TOP_K):
            row_copy(r, k).start()
        return c

    def wait(r, c):
        for k in range(TOP_K):
            row_copy(r, k).wait()
        return c

    @pl.when(t == 0)
    def _():
        zero_scr[...] = jnp.zeros_like(zero_scr)

        def pad_copy(e):
            first = pl.multiple_of(e * cap + (cnt_ref[e] // SUBLANES) * SUBLANES, SUBLANES)
            return pltpu.make_async_copy(zero_scr, xs_ref.at[pl.ds(first, ch), :], zsem)

        def zstart(e, c):
            pad_copy(e).start()
            return c

        def zwait(e, c):
            pad_copy(e).wait()
            return c

        lax.fori_loop(0, N_EXPERTS, zstart, 0)
        lax.fori_loop(0, N_EXPERTS, zwait, 0)

    lax.fori_loop(0, tm, start, 0, unroll=ROW_DMA_UNROLL)
    lax.fori_loop(0, tm, wait, 0, unroll=ROW_DMA_UNROLL)


def _dispatch(slots, counts, hp, cap, ch):
    s, half = hp.shape
    tm = _pick(s, (512, 256, 128))
    kern = functools.partial(_dispatch_kernel, tm=tm, ch=ch, cap=cap)
    return pl.pallas_call(
        kern,
        grid_spec=pltpu.PrefetchScalarGridSpec(
            num_scalar_prefetch=2,
            grid=(s // tm,),
            in_specs=[pl.BlockSpec((tm, half), lambda t, sl, cn: (t, 0))],
            out_specs=pl.BlockSpec(memory_space=pl.ANY),
            scratch_shapes=[
                pltpu.VMEM((ch, half), U32),
                pltpu.SemaphoreType.DMA(()),
                pltpu.SemaphoreType.DMA(()),
            ],
        ),
        out_shape=jax.ShapeDtypeStruct((N_EXPERTS * cap, half), U32),
        compiler_params=_params(("arbitrary",)),
        name="dispatch",
    )(slots, counts, hp)


def _expert_kernel(sbe_ref, sbblk_ref, sbnch_ref, x_ref, w1a_ref, w1c_ref, b1_ref, w2a_ref, w2c_ref,
                   b2_ref, y_ref, xb_scr, w1b_scr, w2b_scr, *, ch, ff):
    s = pl.program_id(0)
    j = pl.program_id(1)
    nch = sbnch_ref[s]
    half = LANES // 2

    def half_act(g):
        lin = pltpu.roll(g, LANES - 1, 1)
        gate = jnp.minimum(g, SWIGLU_LIMIT)
        lin = jnp.clip(lin, -SWIGLU_LIMIT, SWIGLU_LIMIT)
        return gate * jax.nn.sigmoid(SWIGLU_ALPHA * gate) * (lin + 1.0)

    @pl.when(nch > 0)
    def _():
        @pl.when(j == 0)
        def _():
            xb_scr[...] = _bf16_pair_unpack(x_ref[...])

            def init(c, carry):
                y_ref[pl.ds(pl.multiple_of(c * ch, ch), ch), :] = jnp.broadcast_to(
                    b2_ref[...], (ch, y_ref.shape[1]))
                return carry

            lax.fori_loop(0, nch, init, 0)

        def prep():
            w1b_scr[:, :ff] = w1a_ref[...].astype(BF16)
            w1b_scr[:, ff:] = w1c_ref[...].astype(BF16)
            prow = lax.broadcasted_iota(I32, (LANES, LANES), 0)
            pcol = lax.broadcasted_iota(I32, (LANES, LANES), 1)
            perm = (pcol == prow // 2 + half * (prow % 2)).astype(BF16)
            groups = ff // LANES
            for u in range(groups):
                src_ref, v = (w2a_ref, u) if u < groups // 2 else (w2c_ref, u - groups // 2)
                grp = src_ref[pl.ds(v * LANES, LANES), :].astype(BF16)
                w2b_scr[pl.ds(u * LANES, LANES), :] = jnp.dot(
                    perm, grp, preferred_element_type=F32).astype(BF16)

        def up(r0, rows):
            gu = jnp.dot(xb_scr[pl.ds(r0, rows), :], w1b_scr[...],
                         preferred_element_type=F32) + b1_ref[...]
            even = (lax.broadcasted_iota(I32, (rows, LANES), 1) % 2) == 0
            pieces = []
            for u in range(ff // LANES):
                a = half_act(gu[:, (2 * u) * LANES:(2 * u + 1) * LANES])
                b = half_act(gu[:, (2 * u + 1) * LANES:(2 * u + 2) * LANES])
                pieces.append(jnp.where(even, a, pltpu.roll(b, 1, 1)))
            return jnp.concatenate(pieces, axis=1).astype(BF16)

        def down(r0, rows, act):
            y_ref[pl.ds(r0, rows), :] += jnp.dot(act, w2b_scr[...], preferred_element_type=F32)

        full = xb_scr.shape[0] // ch
        for n in range(full - WHOLE_RUN_SIZES + 1, full + 1):
            @pl.when(nch == n)
            def _(n=n):
                prep()
                down(0, n * ch, up(0, n * ch))

        big = 2 * ch
        nbig = nch // 2
        short = nch <= full - WHOLE_RUN_SIZES

        @pl.when(short)
        def _():
            prep()

        @pl.when(jnp.logical_and(short, nbig > 0))
        def _():
            def trip(c, act):
                nxt = up(pl.multiple_of(c * big, big), big)
                down(pl.multiple_of((c - 1) * big, big), big, act)
                return nxt

            act = lax.fori_loop(1, nbig, trip, up(0, big))
            down(pl.multiple_of((nbig - 1) * big, big), big, act)

        @pl.when(jnp.logical_and(short, nch % 2 == 1))
        def _():
            r0 = pl.multiple_of(nbig * big, ch)
            down(r0, ch, up(r0, ch))


def _experts(sb_e, sb_blk, sb_nch, xs, w1, b1, w2, b2, r_rows, ch, cap):
    ne, d, ff2 = w1.shape
    dff = ff2 // 2
    ff = _pick(dff, (256, 128))
    nj = dff // ff
    ns = sb_e.shape[0]
    half = d // 2
    blocks_per_expert = cap // r_rows

    def jmap(s, j, nch):
        return jnp.where(nch[s] > 0, j, nj - 1)

    kern = functools.partial(_expert_kernel, ch=ch, ff=ff)
    return pl.pallas_call(
        kern,
        grid_spec=pltpu.PrefetchScalarGridSpec(
            num_scalar_prefetch=3,
            grid=(ns, nj),
            in_specs=[
                pl.BlockSpec((r_rows, half),
                             lambda s, j, e, b, n: (e[s] * blocks_per_expert + b[s], 0)),
                pl.BlockSpec((None, d, ff), lambda s, j, e, b, n: (e[s], 0, 2 * jmap(s, j, n))),
                pl.BlockSpec((None, d, ff), lambda s, j, e, b, n: (e[s], 0, 2 * jmap(s, j, n) + 1)),
                pl.BlockSpec((None, 1, 2 * ff), lambda s, j, e, b, n: (e[s], 0, jmap(s, j, n))),
                pl.BlockSpec((None, ff // 2, d), lambda s, j, e, b, n: (e[s], 2 * jmap(s, j, n), 0)),
                pl.BlockSpec((None, ff // 2, d),
                             lambda s, j, e, b, n: (e[s], 2 * jmap(s, j, n) + 1, 0)),
                pl.BlockSpec((None, 1, d), lambda s, j, e, b, n: (e[s], 0, 0)),
            ],
            out_specs=pl.BlockSpec((r_rows, d),
                                   lambda s, j, e, b, n: (e[s] * blocks_per_expert + b[s], 0)),
            scratch_shapes=[
                pltpu.VMEM((r_rows, d), BF16),
                pltpu.VMEM((d, 2 * ff), BF16),
                pltpu.VMEM((ff, d), BF16),
            ],
        ),
        out_shape=jax.ShapeDtypeStruct((ne * cap, d), F32),
        compiler_params=_params(("arbitrary", "arbitrary")),
        name="experts",
    )(sb_e, sb_blk, sb_nch, xs, w1, w1, b1.reshape(ne, 1, ff2), w2, w2, b2.reshape(ne, 1, d))


def _combine_kernel(slot_ref, x2_ref, rw_ref, g_ref, ys_ref, o_ref, gbuf, sem, *, tm, final_norm):
    t = pl.program_id(0)
    base = t * tm

    def row_copy(r, k):
        return pltpu.make_async_copy(
            ys_ref.at[pl.ds(slot_ref[(base + r) * TOP_K + k], 1), :],
            gbuf.at[pl.ds(k * tm + r, 1), :],
            sem)

    def start(r, c):
        for k in range(TOP_K):
            row_copy(r, k).start()
        return c

    def wait(r, c):
        for k in range(TOP_K):
            row_copy(r, k).wait()
        return c

    lax.fori_loop(0, tm, start, 0, unroll=ROW_DMA_UNROLL)
    lax.fori_loop(0, tm, wait, 0, unroll=ROW_DMA_UNROLL)

    rw = rw_ref[...]
    out = x2_ref[...]
    for k in range(TOP_K):
        out = out + rw[:, k:k + 1] * gbuf[pl.ds(k * tm, tm), :]
    if final_norm:
        out = _rms(out, g_ref[...])
    o_ref[...] = out


def _combine(slots, x2, rw, g, ys, final_norm):
    s, d = x2.shape
    tm = _pick(s, (256, 128))
    kern = functools.partial(_combine_kernel, tm=tm, final_norm=final_norm)
    return pl.pallas_call(
        kern,
        grid_spec=pltpu.PrefetchScalarGridSpec(
            num_scalar_prefetch=1,
            grid=(s // tm,),
            in_specs=[
                pl.BlockSpec((tm, d), lambda t, sl: (t, 0)),
                pl.BlockSpec((tm, LANES), lambda t, sl: (t, 0)),
                pl.BlockSpec((1, d), lambda t, sl: (0, 0)),
                pl.BlockSpec(memory_space=pl.ANY),
            ],
            out_specs=pl.BlockSpec((tm, d), lambda t, sl: (t, 0)),
            scratch_shapes=[pltpu.VMEM((TOP_K * tm, d), F32), pltpu.SemaphoreType.DMA(())],
        ),
        out_shape=jax.ShapeDtypeStruct((s, d), F32),
        compiler_params=_params(("arbitrary",)),
        name="combine",
    )(slots, x2, rw, g, ys)


def _superblocks(counts, r_rows, ch, ns):
    nsb = (counts + r_rows - 1) // r_rows
    ends = jnp.cumsum(nsb)
    total = ends[-1]
    sidx = jnp.arange(ns, dtype=I32)
    last = jnp.maximum(total - 1, 0)
    sclip = jnp.minimum(sidx, last)
    e = jnp.minimum(jnp.searchsorted(ends, sclip, side="right"), N_EXPERTS - 1).astype(I32)
    blk = sclip - (ends[e] - nsb[e])
    rows = jnp.clip(counts[e] - blk * r_rows, 0, r_rows)
    nch = jnp.where(sidx < total, (rows + ch - 1) // ch, 0)
    return e, blk.astype(I32), nch.astype(I32)


def _layer(layer, x, g_mix, w_in, b_forget, b_gate, w_pa, w_pb, w_out, g_ffn,
           w_router, b_router, w1, b1, w2, b2, g_out, final_norm):
    s, d = x.shape
    qkv = 6 * D_HEADS

    assert N_HEADS == SUBLANES
    w_t = jnp.swapaxes(w_in, 1, 2)[layer]
    w_main = _w_prep(w_t, qkv)
    p, f_log, vt_sb, vt_fx = _in_proj(x, g_mix.reshape(1, d), w_main, w_t, qkv)

    log_f = jax.nn.log_sigmoid(f_log[:, :N_HEADS] + b_forget)
    cum_f = jnp.cumsum(log_f, axis=0).T

    nb = D_HEADS // HEAD_DIM
    o_sb = _attention(p, vt_sb, 0, nb)
    o_fx = _attention(p, vt_fx, 3 * nb, 4 * nb, fox=(_fox_tables(cum_f), *_fox_bias(cum_f)))

    r_rows = 1280
    ch = 128
    cap = -(-(s + ch) // r_rows) * r_rows
    ns =(s * TOP_K) // r_rows + N_EXPERTS
    wr = jnp.pad(w_router, ((0, 0), (0, LANES - N_EXPERTS)))
    wr_top = _bf16_trunc(wr)
    wr_hi = wr_top.astype(BF16)
    wr_lo = (wr - wr_top).astype(BF16)
    br = jnp.pad(b_router, (0, LANES - N_EXPERTS), constant_values=NEG_BIG).reshape(1, LANES)
    x2, hp, ri, rw, cnt = _mix_route(
        o_sb, o_fx, p, qkv // d, b_gate.reshape(1, 2 * d), w_pa.astype(BF16), w_pb.astype(BF16),
        w_out.astype(BF16), x, g_ffn.reshape(1, d), wr_hi, wr_lo, br, cap)

    slots = ri[:, :TOP_K].reshape(-1)
    counts = cnt[0, :N_EXPERTS].astype(I32)
    xs = _dispatch(slots, counts, hp, cap, ch)
    sb_e, sb_blk, sb_nch = _superblocks(counts, r_rows, ch, ns)
    ys = _experts(sb_e, sb_blk, sb_nch, xs, w1, b1, w2, b2, r_rows, ch, cap)
    return _combine(slots, x2, rw, g_out.reshape(1, d), ys, final_norm)


def kernel(x, g_mix, w_in, b_forget, b_gate, w_proj_sb, w_proj_fox, w_out, g_ffn, w_router,
           b_router, w_mlp1, b_mlp1, w_mlp2, b_mlp2, g_final):
    b, s, d = x.shape
    depth = g_mix.shape[0]
    outs = []
    for bi in range(b):
        xb = x[bi]
        for layer in range(depth):
            last = layer == depth - 1
            xb = _layer(layer, xb, g_mix[layer], w_in, b_forget[layer], b_gate[layer],
                        w_proj_sb[layer], w_proj_fox[layer], w_out[layer], g_ffn[layer],
                        w_router[layer], b_router[layer], w_mlp1[layer], b_mlp1[layer],
                        w_mlp2[layer], b_mlp2[layer], g_final, last)
        outs.append(xb.reshape(1, s, d))
    return outs[0] if b == 1 else jnp.concatenate(outs, axis=0)
```

```python
import functools

import jax
import jax.numpy as jnp
from jax import lax
from jax.experimental import pallas as pl
from jax.experimental.pallas import tpu as pltpu

F32 = jnp.float32
BF16 = jnp.bfloat16
U32 = jnp.uint32
I32 = jnp.int32

LANES = 128
SUBLANES = 8
HEAD_DIM = 128
N_HEADS = 8
D_HEADS = N_HEADS * HEAD_DIM
N_EXPERTS = 32
TOP_K = 4
SWIGLU_LIMIT = 7.0
SWIGLU_ALPHA = 1.702
RMS_EPS = 1e-5
NEG_BIG = -1e30
WHOLE_RUN_SIZES = 3
SB_VALUE_TILE = 256
ROW_DMA_UNROLL = 8
LOG2_E = 1.4426950408889634
BF16_ROWS = 16
F32_EXP2_UNDERFLOW = 152.0
NORM_SLACK = 1.001
F32_EXP_UNDERFLOW = 105.0
VMEM_LIMIT = 56 * 1024 * 1024


def _pick(n, candidates):
    for c in candidates:
        if n % c == 0:
            return c
    raise ValueError(f"no tile in {candidates} divides {n}")


def _params(sem, vmem=VMEM_LIMIT):
    return pltpu.CompilerParams(dimension_semantics=sem, vmem_limit_bytes=vmem)


def _rms(x, g):
    ms = jnp.mean(x * x, axis=-1, keepdims=True)
    return x * lax.rsqrt(ms + RMS_EPS) * g


def _inproj_kernel(x_ref, g_ref, w_ref, wf_ref, p_ref, f_ref, vta_ref, vtb_ref, h_scr, *, tn):
    n = pl.program_id(1)

    @pl.when(n == 0)
    def _():
        hb = _rms(x_ref[...], g_ref[...]).astype(BF16)
        h_scr[...] = hb
        wf = jnp.concatenate(
            [wf_ref[...], jnp.zeros((LANES - SUBLANES, wf_ref.shape[1]), F32)], axis=0).astype(BF16)
        f_ref[...] = lax.dot_general(hb, wf, _NT, preferred_element_type=F32)

    res = jnp.dot(h_scr[...], w_ref[...], preferred_element_type=F32)
    p_ref[...] = res.astype(BF16)

    heads_per_tile = tn // HEAD_DIM
    for section, vt_ref in ((2, vta_ref), (5, vtb_ref)):
        for tile in range(section * D_HEADS // tn, (section + 1) * D_HEADS // tn):
            @pl.when(n == tile)
            def _(vt_ref=vt_ref, head0=(tile * tn - section * D_HEADS) // HEAD_DIM):
                tv = vt_ref.shape[3]
                for hh in range(heads_per_tile):
                    vt = res[:, hh * HEAD_DIM:(hh + 1) * HEAD_DIM].T.astype(BF16)
                    for c in range(vt.shape[1] // tv):
                        vt_ref[head0 + hh, c, pl.ds(0, HEAD_DIM), :] = vt[:, c * tv:(c + 1) * tv]
                        if vt_ref.shape[2] > HEAD_DIM:
                            extra = lax.broadcasted_iota(I32, (BF16_ROWS, tv), 0) == 0
                            vt_ref[head0 + hh, c, pl.ds(HEAD_DIM, BF16_ROWS), :] = extra.astype(BF16)


def _wprep_kernel(a_ref, b_ref, o_ref, *, first_tail):
    n = pl.program_id(1)

    @pl.when(n < first_tail)
    def _():
        o_ref[...] = a_ref[...].T.astype(BF16)

    @pl.when(n >= first_tail)
    def _():
        ext = jnp.concatenate([a_ref[...], b_ref[...]], axis=0)
        o_ref[...] = ext[SUBLANES:, :].T.astype(BF16)


def _w_prep(w_t, qkv):
    n_in, d = w_t.shape
    n_out = n_in - SUBLANES
    tn = _pick(n_out, (1024, 512, 256, 128))
    tr = _pick(d, (512, 256, 128))
    per = tn // SUBLANES
    kern = functools.partial(_wprep_kernel, first_tail=qkv // tn)
    return pl.pallas_call(
        kern,
        grid=(d // tr, n_out // tn),
        in_specs=[
            pl.BlockSpec((tn, tr), lambda r, n: (n, r)),
            pl.BlockSpec((SUBLANES, tr), lambda r, n: ((n + 1) * per, r)),
        ],
        out_specs=pl.BlockSpec((tr, tn), lambda r, n: (r, n)),
        out_shape=jax.ShapeDtypeStruct((d, n_out), BF16),
        compiler_params=_params(("parallel", "parallel")),
        name="w_prep",
    )(w_t, w_t)


def _in_proj(x, g, w_main, w_t, f_row):
    s, d = x.shape
    n = w_main.shape[1]
    tm = _pick(s, (1024, 512, 256, 128))
    tn = _pick(n, (1024, 512, 256, 128))
    tq = _attn_tile(s)
    rows_b = HEAD_DIM + BF16_ROWS
    tv_a = min(tq, SB_VALUE_TILE)
    vt_spec = lambda rows, tv: pl.BlockSpec((N_HEADS, tm // tv, rows, tv), lambda m, j: (0, m, 0, 0))
    return pl.pallas_call(
        functools.partial(_inproj_kernel, tn=tn),
        grid=(s // tm, n // tn),
        in_specs=[
            pl.BlockSpec((tm, d), lambda m, j: (m, 0)),
            pl.BlockSpec((1, d), lambda m, j: (0, 0)),
            pl.BlockSpec((d, tn), lambda m, j: (0, j)),
            pl.BlockSpec((SUBLANES, d), lambda m, j: (f_row // SUBLANES, 0)),
        ],
        out_specs=[
            pl.BlockSpec((tm, tn), lambda m, j: (m, j)),
            pl.BlockSpec((tm, LANES), lambda m, j: (m, 0)),
            vt_spec(HEAD_DIM, tv_a),
            vt_spec(rows_b, tq),
        ],
        out_shape=[
            jax.ShapeDtypeStruct((s, n), BF16),
            jax.ShapeDtypeStruct((s, LANES), F32),
            jax.ShapeDtypeStruct((N_HEADS, s // tv_a, HEAD_DIM, tv_a), BF16),
            jax.ShapeDtypeStruct((N_HEADS, s // tq, rows_b, tq), BF16),
        ],
        scratch_shapes=[pltpu.VMEM((tm, d), BF16)],
        compiler_params=_params(("parallel", "arbitrary")),
        name="in_proj",
    )(x, g, w_main, w_t)


_NT = (((1,), (1,)), ((), ()))


def _sb_kernel(q_ref, k_ref, vt_ref, o_ref, acc_scr, *, tq, tk, tv, scale):
    i = pl.program_id(1)
    nsub = tq // tk
    per_tv = tv // tk
    q = q_ref[...]
    row = lax.broadcasted_iota(I32, (tk, 2 * tk), 0)
    col = lax.broadcasted_iota(I32, (tk, 2 * tk), 1)
    ut2 = ((col % tk) >= row).astype(BF16)
    diff = lax.broadcasted_iota(I32, (tk, tq), 1) - lax.broadcasted_iota(I32, (tk, tq), 0)

    def sub_block(kb, qs, csum, mask):
        k = k_ref[pl.ds(pl.multiple_of(kb * tk, tk), tk), :]
        z = lax.dot_general(k, qs, _NT, preferred_element_type=F32) * scale
        lk = -(jnp.maximum(z, 0.0) + jnp.log(1.0 + jnp.exp(-jnp.abs(z))))
        if mask is not None:
            lk = jnp.where(mask, lk, 0.0)
        hi = lk.astype(BF16)
        lo = (lk - hi.astype(F32)).astype(BF16)
        cs = jnp.dot(ut2, jnp.concatenate([hi, lo], axis=0), preferred_element_type=F32)
        a = jnp.exp(z + cs + csum)
        if mask is not None:
            a = jnp.where(mask, a, 0.0)
        return a.astype(BF16), cs[0:1, :]

    acc_scr[...] = jnp.zeros_like(acc_scr)
    csum = jnp.zeros((1, tq), F32)
    for u in reversed(range(nsub)):
        lo = u * tk
        a, inc = sub_block(i * nsub + u, q[lo:, :], csum[:, lo:], diff[:, lo:] > lo)
        vt = vt_ref[i * (tq // tv) + u // per_tv, :, pl.ds((u % per_tv) * tk, tk)]
        acc_scr[:, lo:] += jnp.dot(vt, a, preferred_element_type=F32)
        csum = jnp.concatenate([csum[:, :lo], csum[:, lo:] + inc], axis=1) if lo else csum + inc

    def far(g, csum, mask=None):
        parts = [None] * per_tv
        for u in reversed(range(per_tv)):
            parts[u], inc = sub_block(g * per_tv + u, q, csum, mask)
            csum = csum + inc
        acc_scr[...] += jnp.dot(vt_ref[g], jnp.concatenate(parts, axis=0),
                                preferred_element_type=F32)
        return csum

    first_far = i * (tq // tv) - 1

    def live(state):
        n, c = state
        return jnp.logical_and(n <= first_far, jnp.max(c) > -F32_EXP_UNDERFLOW)

    any_far = jnp.broadcast_to(i > 0, (tk, tq))
    csum = far(jnp.maximum(first_far, 0), csum, any_far)
    lax.while_loop(live, lambda st: (st[0] + 1, far(first_far - st[0], st[1])),
                   (jnp.int32(1), csum))
    o_ref[...] = acc_scr[...].T.astype(o_ref.dtype)


def _fox_kernel(ff_ref, fl_ref, q_ref, qa_ref, k_ref, ka_ref, vt_ref, o_ref,
                acc_scr, s_scr, kn_smem, *, tq, tk, scale):
    i = pl.program_id(1)
    nsub = tq // tk
    q = jnp.concatenate([q_ref[...], qa_ref[...]], axis=1)
    diff = lax.broadcasted_iota(I32, (tk, tq), 1) - lax.broadcasted_iota(I32, (tk, tq), 0)
    to_log2 = scale * LOG2_E

    def scores(t, slot, masked):
        for u in range(nsub):
            ks = pl.multiple_of((t * nsub + u) * tk, tk)
            k = jnp.concatenate([k_ref[pl.ds(ks, tk), :], ka_ref[pl.ds(ks, tk), :]], axis=1)
            lo = u * tk if masked else 0
            s = lax.dot_general(k, q[lo:, :], _NT, preferred_element_type=F32) * to_log2
            if masked:
                s = jnp.where(diff[:, lo:] >= lo, s, NEG_BIG)
                if lo:
                    s_scr[slot, pl.ds(u * tk, tk), pl.ds(0, lo)] = jnp.full((tk, lo), NEG_BIG, F32)
            s_scr[slot, pl.ds(u * tk, tk), pl.ds(lo, tq - lo)] = s

    def consume(t, slot, m, gate=None):
        top = s_scr[slot, pl.ds(0, tk), :]
        for u in range(1, nsub):
            top = jnp.maximum(top, s_scr[slot, pl.ds(u * tk, tk), :])
        tile_max = jnp.max(top, axis=0, keepdims=True)
        if gate is not None:
            tile_max = jnp.where(gate, tile_max, NEG_BIG)
        m_new = jnp.maximum(m, tile_max)
        alpha = jnp.exp2(m - m_new)
        p = jnp.exp2(s_scr[slot] - m_new)
        if gate is not None:
            p = jnp.where(gate, p, 0.0)
        pcat = p.astype(BF16)
        acc_scr[...] = alpha * acc_scr[...] + jnp.dot(vt_ref[t], pcat, preferred_element_type=F32)
        return m_new

    def step(t, slot, m):
        scores(t - 1, 1 - slot, False)
        return consume(t, slot, m)

    def max_norm(x):
        xf = x.astype(F32)
        return jnp.max(jnp.sqrt(jnp.sum(xf * xf, axis=1, keepdims=True)))

    kn_prev = jnp.where(i > 0, kn_smem[jnp.maximum(i - 1, 0)], 0.0)
    kn_smem[i] = jnp.maximum(kn_prev, max_norm(k_ref[pl.ds(pl.multiple_of(i * tq, tq), tq), :]))

    acc_scr[...] = jnp.zeros_like(acc_scr)
    scores(i, 0, True)
    scores(jnp.maximum(i - 1, 0), 1, False)
    m = consume(i, 0, jnp.full((1, tq), NEG_BIG, F32))
    scores(jnp.maximum(i - 2, 0), 0, False)
    m = consume(jnp.maximum(i - 1, 0), 1, m, gate=i > 0)

    nt = pl.num_programs(1)
    row = pl.program_id(0) * nt
    floor = jnp.min(m) - F32_EXP2_UNDERFLOW
    qk = to_log2 * NORM_SLACK * max_norm(q_ref[...])
    f_first = LOG2_E * ff_ref[row + i]

    def live(extra):
        t = i - 2 - extra
        bound = qk * kn_smem[jnp.maximum(t, 0)] + (
            f_first - LOG2_E * fl_ref[row + jnp.maximum(t, 0)])
        return jnp.logical_and(t >= 0, bound >= floor)

    extra = lax.while_loop(live, lambda e: e + 1, jnp.int32(0))

    @pl.when(extra > 0)
    def _():
        rest = extra - 1
        last = i - 1 - extra

        def trip(k, m):
            t = i - 2 - 2 * k
            return step(t - 1, 1, step(t, 0, m))

        m2 = lax.fori_loop(0, rest // 2, trip, m)

        @pl.when(rest % 2 == 0)
        def _():
            consume(last, 0, m2)

        @pl.when(rest % 2 == 1)
        def _():
            consume(last, 1, step(last + 1, 0, m2))

    acc = acc_scr[...]
    o_ref[...] = (acc[:HEAD_DIM] / acc[HEAD_DIM:HEAD_DIM + 1]).T.astype(o_ref.dtype)


def _attn_tile(s):
    return _pick(s, (512, 256, 128))


def _attention(p, vt, q_col, k_col, fox=None):
    s = p.shape[0]
    tq = _attn_tile(s)
    tk = 128
    scale = HEAD_DIM ** -0.5
    rows = vt.shape[2]
    q_spec = pl.BlockSpec((tq, HEAD_DIM), lambda h, i, *_: (i, q_col + h))
    k_spec = pl.BlockSpec((s, HEAD_DIM), lambda h, i, *_: (0, k_col + h))
    vt_spec = pl.BlockSpec((None,) + vt.shape[1:], lambda h, i, *_: (h, 0, 0, 0))
    scratch = [pltpu.VMEM((rows, tq), F32)]
    order = ("parallel", "arbitrary")
    if fox is None:
        kern = functools.partial(_sb_kernel, tq=tq, tk=tk, tv=vt.shape[3], scale=scale)
        name = "sb_attn"
        tables = []
        in_specs = [q_spec, k_spec, vt_spec]
        args = [p, p, vt]
    else:
        tables, qa, ka = fox
        scratch.append(pltpu.VMEM((2, tq, tq), F32))
        scratch.append(pltpu.SMEM((s // tq,), F32))
        order = ("arbitrary", "arbitrary")
        kern = functools.partial(_fox_kernel, tq=tq, tk=tk, scale=scale)
        name = "fox_attn"
        in_specs = [
            q_spec,
            pl.BlockSpec((None, tq, HEAD_DIM), lambda h, i, *_: (h, i, 0)),
            k_spec,
            pl.BlockSpec((None, s, HEAD_DIM), lambda h, i, *_: (h, 0, 0)),
            vt_spec,
        ]
        args = [p, qa, p, ka, vt]
    return pl.pallas_call(
        kern,
        grid_spec=pltpu.PrefetchScalarGridSpec(
            num_scalar_prefetch=len(tables),
            grid=(N_HEADS, s // tq),
            in_specs=in_specs,
            out_specs=pl.BlockSpec((tq, HEAD_DIM), lambda h, i, *_: (i, h)),
            scratch_shapes=scratch,
        ),
        out_shape=jax.ShapeDtypeStruct((s, D_HEADS), BF16),
        compiler_params=_params(order),
        name=name,
    )(*tables, *args)


def _fox_tables(cum_f):
    nt = cum_f.shape[1] // _attn_tile(cum_f.shape[1])
    f_tiles = cum_f.reshape(N_HEADS, nt, -1)
    return [f_tiles[:, :, 0].reshape(-1), f_tiles[:, :, -1].reshape(-1)]


def _bf16_trunc(x):
    bits = lax.bitcast_convert_type(x, U32) & jnp.uint32(0xFFFF0000)
    return lax.bitcast_convert_type(bits, F32)


def _split3(x):
    a = _bf16_trunc(x)
    b = _bf16_trunc(x - a)
    c = _bf16_trunc(x - a - b)
    return [a.astype(BF16), b.astype(BF16), c.astype(BF16)]


def _fox_bias(cum_f):
    parts = _split3(cum_f * (HEAD_DIM ** 0.5))
    ones = jnp.ones_like(parts[0])
    zeros = jnp.zeros_like(parts[0])
    widen = jnp.eye(SUBLANES, HEAD_DIM, dtype=BF16)

    def spread(cols):
        packed = jnp.stack(cols + [zeros, zeros], axis=-1)
        return jnp.einsum("hsc,cl->hsl", packed, widen,
                          preferred_element_type=F32).astype(BF16)

    return spread([ones] * 3 + parts), spread([-t for t in parts] + [ones] * 3)


def _bf16_pair_pack(h):
    bits = lax.bitcast_convert_type(h, U32)
    rnd = bits + jnp.uint32(0x7FFF) + ((bits >> 16) & jnp.uint32(1))
    half = h.shape[1] // 2
    return (rnd[:, half:] & jnp.uint32(0xFFFF0000)) | (rnd[:, :half] >> 16)


def _bf16_pair_unpack(w):
    lo = lax.bitcast_convert_type(w << 16, F32)
    hi = lax.bitcast_convert_type(w & jnp.uint32(0xFFFF0000), F32)
    return jnp.concatenate([lo, hi], axis=1).astype(BF16)


def _mix_kernel(osb_ref, ofx_ref, ga_ref, gb_ref, bg_ref, wpa_ref, wpb_ref, wo_ref, x_ref, g_ref,
                wrh_ref, wrl_ref, br_ref,
                x2_ref, hp_ref, ri_ref, rw_ref, cnt_ref, cnt_scr, *, tm, d, cap):
    t = pl.program_id(0)

    @pl.when(t == 0)
    def _():
        cnt_scr[...] = jnp.zeros_like(cnt_scr)

    bg = bg_ref[...]
    pa = jnp.dot(osb_ref[...], wpa_ref[...], preferred_element_type=F32)
    pb = jnp.dot(ofx_ref[...], wpb_ref[...], preferred_element_type=F32)
    ga = jax.nn.sigmoid(ga_ref[...].astype(F32) + bg[:, :d])
    gb = jax.nn.sigmoid(gb_ref[...].astype(F32) + bg[:, d:])
    mixed = (ga * pa + gb * pb).astype(BF16)
    x2 = x_ref[...] + jnp.dot(mixed, wo_ref[...], preferred_element_type=F32)
    x2_ref[...] = x2
    h2 = _rms(x2, g_ref[...])
    hp_ref[...] = _bf16_pair_pack(h2)

    hh = h2.astype(BF16)
    hl = (h2 - hh.astype(F32)).astype(BF16)
    logits = (jnp.dot(hh, wrh_ref[...], preferred_element_type=F32)
              + jnp.dot(hh, wrl_ref[...], preferred_element_type=F32)
              + jnp.dot(hl, wrh_ref[...], preferred_element_type=F32)
              + br_ref[...])

    lane = lax.broadcasted_iota(I32, (tm, LANES), 1).astype(F32)
    work = logits
    sel = jnp.zeros((tm, LANES), F32)
    vals, idxs = [], []
    for _ in range(TOP_K):
        mx = jnp.max(work, axis=1, keepdims=True)
        idx = jnp.min(jnp.where(work == mx, lane, float(LANES)), axis=1, keepdims=True)
        hit = lane == idx
        vals.append(mx)
        idxs.append(idx)
        work = jnp.where(hit, -jnp.inf, work)
        sel = jnp.where(hit, 1.0, sel)
    exps = [jnp.exp(v - vals[0]) for v in vals]
    denom = exps[0] + exps[1] + exps[2] + exps[3]

    row = lax.broadcasted_iota(I32, (tm, tm), 0)
    col = lax.broadcasted_iota(I32, (tm, tm), 1)
    tri = (col < row).astype(BF16)
    rank = jnp.dot(tri, sel.astype(BF16), preferred_element_type=F32) + cnt_scr[...]
    cnt_new = cnt_scr[...] + jnp.sum(sel, axis=0, keepdims=True)
    cnt_scr[...] = cnt_new
    cnt_ref[...] = cnt_new
    slot = rank + lane * float(cap)
    ri = jnp.zeros((tm, LANES), F32)
    rw = jnp.zeros((tm, LANES), F32)
    for r in range(TOP_K):
        slot_r = jnp.sum(jnp.where(lane == idxs[r], slot, 0.0), axis=1, keepdims=True)
        ri = jnp.where(lane == float(r), slot_r, ri)
        rw = jnp.where(lane == float(r), exps[r] / denom, rw)
    ri_ref[...] = ri.astype(I32)
    rw_ref[...] = rw


def _mix_route(o_sb, o_fx, p, gate_col, b_gate, w_pa, w_pb, w_out, x, g_ffn, wr_hi, wr_lo, br, cap):
    s, d = x.shape
    tm = _pick(s, (256, 128))
    const = lambda shape: pl.BlockSpec(shape, lambda t: (0, 0), pipeline_mode=pl.Buffered(1))
    kern = functools.partial(_mix_kernel, tm=tm, d=d, cap=cap)
    return pl.pallas_call(
        kern,
        grid=(s // tm,),
        in_specs=[
            pl.BlockSpec((tm, D_HEADS), lambda t: (t, 0)),
            pl.BlockSpec((tm, D_HEADS), lambda t: (t, 0)),
            pl.BlockSpec((tm, d), lambda t: (t, gate_col)),
            pl.BlockSpec((tm, d), lambda t: (t, gate_col + 1)),
            const((1, 2 * d)),
            const((D_HEADS, d)),
            const((D_HEADS, d)),
            const((d, d)),
            pl.BlockSpec((tm, d), lambda t: (t, 0)),
            const((1, d)),
            const((d, LANES)),
            const((d, LANES)),
            const((1, LANES)),
        ],
        out_specs=[
            pl.BlockSpec((tm, d), lambda t: (t, 0)),
            pl.BlockSpec((tm, d // 2), lambda t: (t, 0)),
            pl.BlockSpec((tm, LANES), lambda t: (t, 0)),
            pl.BlockSpec((tm, LANES), lambda t: (t, 0)),
            pl.BlockSpec((1, LANES), lambda t: (0, 0)),
        ],
        out_shape=[
            jax.ShapeDtypeStruct((s, d), F32),
            jax.ShapeDtypeStruct((s, d // 2), U32),
            jax.ShapeDtypeStruct((s, LANES), I32),
            jax.ShapeDtypeStruct((s, LANES), F32),
            jax.ShapeDtypeStruct((1, LANES), F32),
        ],
        scratch_shapes=[pltpu.VMEM((1, LANES), F32)],
        compiler_params=_params(("arbitrary",)),
        name="mix_route",
    )(o_sb, o_fx, p, p, b_gate, w_pa, w_pb, w_out, x, g_ffn, wr_hi, wr_lo, br)


def _dispatch_kernel(slot_ref, cnt_ref, hp_ref, xs_ref, zero_scr, sem, zsem, *, tm, ch, cap):
    t = pl.program_id(0)
    base = t * tm

    def row_copy(r, k):
        return pltpu.make_async_copy(
            hp_ref.at[pl.ds(r, 1), :],
            xs_ref.at[pl.ds(slot_ref[(base + r) * TOP_K + k], 1), :],
            sem)

    def start(r, c):
        for k in range(TOP_K):
            row_copy(r, k).start()
        return c

    def wait(r, c):
        for k in range(TOP_K):
            row_copy(r, k).wait()
        return c

    @pl.when(t == 0)
    def _():
        zero_scr[...] = jnp.zeros_like(zero_scr)

        def pad_copy(e):
            first = pl.multiple_of(e * cap + (cnt_ref[e] // SUBLANES) * SUBLANES, SUBLANES)
            return pltpu.make_async_copy(zero_scr, xs_ref.at[pl.ds(first, ch), :], zsem)

        def zstart(e, c):
            pad_copy(e).start()
            return c

        def zwait(e, c):
            pad_copy(e).wait()
            return c

        lax.fori_loop(0, N_EXPERTS, zstart, 0)
        lax.fori_loop(0, N_EXPERTS, zwait, 0)

    lax.fori_loop(0, tm, start, 0, unroll=ROW_DMA_UNROLL)
    lax.fori_loop(0, tm, wait, 0, unroll=ROW_DMA_UNROLL)


def _dispatch(slots, counts, hp, cap, ch):
    s, half = hp.shape
    tm = _pick(s, (512, 256, 128))
    kern = functools.partial(_dispatch_kernel, tm=tm, ch=ch, cap=cap)
    return pl.pallas_call(
        kern,
        grid_spec=pltpu.PrefetchScalarGridSpec(
            num_scalar_prefetch=2,
            grid=(s // tm,),
            in_specs=[pl.BlockSpec((tm, half), lambda t, sl, cn: (t, 0))],
            out_specs=pl.BlockSpec(memory_space=pl.ANY),
            scratch_shapes=[
                pltpu.VMEM((ch, half), U32),
                pltpu.SemaphoreType.DMA(()),
                pltpu.SemaphoreType.DMA(()),
            ],
        ),
        out_shape=jax.ShapeDtypeStruct((N_EXPERTS * cap, half), U32),
        compiler_params=_params(("arbitrary",)),
        name="dispatch",
    )(slots, counts, hp)


def _expert_kernel(sbe_ref, sbblk_ref, sbnch_ref, x_ref, w1a_ref, w1c_ref, b1_ref, w2a_ref, w2c_ref,
                   b2_ref, y_ref, xb_scr, w1b_scr, w2b_scr, *, ch, ff):
    s = pl.program_id(0)
    j = pl.program_id(1)
    nch = sbnch_ref[s]
    half = LANES // 2

    def half_act(g):
        lin = pltpu.roll(g, LANES - 1, 1)
        gate = jnp.minimum(g, SWIGLU_LIMIT)
        lin = jnp.clip(lin, -SWIGLU_LIMIT, SWIGLU_LIMIT)
        return gate * jax.nn.sigmoid(SWIGLU_ALPHA * gate) * (lin + 1.0)

    @pl.when(nch > 0)
    def _():
        @pl.when(j == 0)
        def _():
            xb_scr[...] = _bf16_pair_unpack(x_ref[...])

            def init(c, carry):
                y_ref[pl.ds(pl.multiple_of(c * ch, ch), ch), :] = jnp.broadcast_to(
                    b2_ref[...], (ch, y_ref.shape[1]))
                return carry

            lax.fori_loop(0, nch, init, 0)

        def prep():
            w1b_scr[:, :ff] = w1a_ref[...].astype(BF16)
            w1b_scr[:, ff:] = w1c_ref[...].astype(BF16)
            prow = lax.broadcasted_iota(I32, (LANES, LANES), 0)
            pcol = lax.broadcasted_iota(I32, (LANES, LANES), 1)
            perm = (pcol == prow // 2 + half * (prow % 2)).astype(BF16)
            groups = ff // LANES
            for u in range(groups):
                src_ref, v = (w2a_ref, u) if u < groups // 2 else (w2c_ref, u - groups // 2)
                grp = src_ref[pl.ds(v * LANES, LANES), :].astype(BF16)
                w2b_scr[pl.ds(u * LANES, LANES), :] = jnp.dot(
                    perm, grp, preferred_element_type=F32).astype(BF16)

        def up(r0, rows):
            gu = jnp.dot(xb_scr[pl.ds(r0, rows), :], w1b_scr[...],
                         preferred_element_type=F32) + b1_ref[...]
            even = (lax.broadcasted_iota(I32, (rows, LANES), 1) % 2) == 0
            pieces = []
            for u in range(ff // LANES):
                a = half_act(gu[:, (2 * u) * LANES:(2 * u + 1) * LANES])
                b = half_act(gu[:, (2 * u + 1) * LANES:(2 * u + 2) * LANES])
                pieces.append(jnp.where(even, a, pltpu.roll(b, 1, 1)))
            return jnp.concatenate(pieces, axis=1).astype(BF16)

        def down(r0, rows, act):
            y_ref[pl.ds(r0, rows), :] += jnp.dot(act, w2b_scr[...], preferred_element_type=F32)

        full = xb_scr.shape[0] // ch
        for n in range(full - WHOLE_RUN_SIZES + 1, full + 1):
            @pl.when(nch == n)
            def _(n=n):
                prep()
                down(0, n * ch, up(0, n * ch))

        big = 2 * ch
        nbig = nch // 2
        short = nch <= full - WHOLE_RUN_SIZES

        @pl.when(short)
        def _():
            prep()

        @pl.when(jnp.logical_and(short, nbig > 0))
        def _():
            def trip(c, act):
                nxt = up(pl.multiple_of(c * big, big), big)
                down(pl.multiple_of((c - 1) * big, big), big, act)
                return nxt

            act = lax.fori_loop(1, nbig, trip, up(0, big))
            down(pl.multiple_of((nbig - 1) * big, big), big, act)

        @pl.when(jnp.logical_and(short, nch % 2 == 1))
        def _():
            r0 = pl.multiple_of(nbig * big, ch)
            down(r0, ch, up(r0, ch))


def _experts(sb_e, sb_blk, sb_nch, xs, w1, b1, w2, b2, r_rows, ch, cap):
    ne, d, ff2 = w1.shape
    dff = ff2 // 2
    ff = _pick(dff, (256, 128))
    nj = dff // ff
    ns = sb_e.shape[0]
    half = d // 2
    blocks_per_expert = cap // r_rows

    def jmap(s, j, nch):
        return jnp.where(nch[s] > 0, j, nj - 1)

    kern = functools.partial(_expert_kernel, ch=ch, ff=ff)
    return pl.pallas_call(
        kern,
        grid_spec=pltpu.PrefetchScalarGridSpec(
            num_scalar_prefetch=3,
            grid=(ns, nj),
            in_specs=[
                pl.BlockSpec((r_rows, half),
                             lambda s, j, e, b, n: (e[s] * blocks_per_expert + b[s], 0)),
                pl.BlockSpec((None, d, ff), lambda s, j, e, b, n: (e[s], 0, 2 * jmap(s, j, n))),
                pl.BlockSpec((None, d, ff), lambda s, j, e, b, n: (e[s], 0, 2 * jmap(s, j, n) + 1)),
                pl.BlockSpec((None, 1, 2 * ff), lambda s, j, e, b, n: (e[s], 0, jmap(s, j, n))),
                pl.BlockSpec((None, ff // 2, d), lambda s, j, e, b, n: (e[s], 2 * jmap(s, j, n), 0)),
                pl.BlockSpec((None, ff // 2, d),
                             lambda s, j, e, b, n: (e[s], 2 * jmap(s, j, n) + 1, 0)),
                pl.BlockSpec((None, 1, d), lambda s, j, e, b, n: (e[s], 0, 0)),
            ],
            out_specs=pl.BlockSpec((r_rows, d),
                                   lambda s, j, e, b, n: (e[s] * blocks_per_expert + b[s], 0)),
            scratch_shapes=[
                pltpu.VMEM((r_rows, d), BF16),
                pltpu.VMEM((d, 2 * ff), BF16),
                pltpu.VMEM((ff, d), BF16),
            ],
        ),
        out_shape=jax.ShapeDtypeStruct((ne * cap, d), F32),
        compiler_params=_params(("arbitrary", "arbitrary")),
        name="experts",
    )(sb_e, sb_blk, sb_nch, xs, w1, w1, b1.reshape(ne, 1, ff2), w2, w2, b2.reshape(ne, 1, d))


def _combine_kernel(slot_ref, x2_ref, rw_ref, g_ref, ys_ref, o_ref, gbuf, sem, *, tm, final_norm):
    t = pl.program_id(0)
    base = t * tm

    def row_copy(r, k):
        return pltpu.make_async_copy(
            ys_ref.at[pl.ds(slot_ref[(base + r) * TOP_K + k], 1), :],
            gbuf.at[pl.ds(k * tm + r, 1), :],
            sem)

    def start(r, c):
        for k in range(TOP_K):
            row_copy(r, k).start()
        return c

    def wait(r, c):
        for k in range(TOP_K):
            row_copy(r, k).wait()
        return c

    lax.fori_loop(0, tm, start, 0, unroll=ROW_DMA_UNROLL)
    lax.fori_loop(0, tm, wait, 0, unroll=ROW_DMA_UNROLL)

    rw = rw_ref[...]
    out = x2_ref[...]
    for k in range(TOP_K):
        out = out + rw[:, k:k + 1] * gbuf[pl.ds(k * tm, tm), :]
    if final_norm:
        out = _rms(out, g_ref[...])
    o_ref[...] = out


def _combine(slots, x2, rw, g, ys, final_norm):
    s, d = x2.shape
    tm = _pick(s, (256, 128))
    kern = functools.partial(_combine_kernel, tm=tm, final_norm=final_norm)
    return pl.pallas_call(
        kern,
        grid_spec=pltpu.PrefetchScalarGridSpec(
            num_scalar_prefetch=1,
            grid=(s // tm,),
            in_specs=[
                pl.BlockSpec((tm, d), lambda t, sl: (t, 0)),
                pl.BlockSpec((tm, LANES), lambda t, sl: (t, 0)),
                pl.BlockSpec((1, d), lambda t, sl: (0, 0)),
                pl.BlockSpec(memory_space=pl.ANY),
            ],
            out_specs=pl.BlockSpec((tm, d), lambda t, sl: (t, 0)),
            scratch_shapes=[pltpu.VMEM((TOP_K * tm, d), F32), pltpu.SemaphoreType.DMA(())],
        ),
        out_shape=jax.ShapeDtypeStruct((s, d), F32),
        compiler_params=_params(("arbitrary",)),
        name="combine",
    )(slots, x2, rw, g, ys)


def _superblocks(counts, r_rows, ch, ns):
    nsb = (counts + r_rows - 1) // r_rows
    ends = jnp.cumsum(nsb)
    total = ends[-1]
    sidx = jnp.arange(ns, dtype=I32)
    last = jnp.maximum(total - 1, 0)
    sclip = jnp.minimum(sidx, last)
    e = jnp.minimum(jnp.searchsorted(ends, sclip, side="right"), N_EXPERTS - 1).astype(I32)
    blk = sclip - (ends[e] - nsb[e])
    rows = jnp.clip(counts[e] - blk * r_rows, 0, r_rows)
    nch = jnp.where(sidx < total, (rows + ch - 1) // ch, 0)
    return e, blk.astype(I32), nch.astype(I32)


def _layer(layer, x, g_mix, w_in, b_forget, b_gate, w_pa, w_pb, w_out, g_ffn,
           w_router, b_router, w1, b1, w2, b2, g_out, final_norm):
    s, d = x.shape
    qkv = 6 * D_HEADS

    assert N_HEADS == SUBLANES
    w_t = jnp.swapaxes(w_in, 1, 2)[layer]
    w_main = _w_prep(w_t, qkv)
    p, f_log, vt_sb, vt_fx = _in_proj(x, g_mix.reshape(1, d), w_main, w_t, qkv)

    log_f = jax.nn.log_sigmoid(f_log[:, :N_HEADS] + b_forget)
    cum_f = jnp.cumsum(log_f, axis=0).T

    nb = D_HEADS // HEAD_DIM
    o_sb = _attention(p, vt_sb, 0, nb)
    o_fx = _attention(p, vt_fx, 3 * nb, 4 * nb, fox=(_fox_tables(cum_f), *_fox_bias(cum_f)))

    r_rows = 1280
    ch = 128
    cap = -(-(s + ch) // r_rows) * r_rows
    ns =(s * TOP_K) // r_rows + N_EXPERTS
    wr = jnp.pad(w_router, ((0, 0), (0, LANES - N_EXPERTS)))
    wr_top = _bf16_trunc(wr)
    wr_hi = wr_top.astype(BF16)
    wr_lo = (wr - wr_top).astype(BF16)
    br = jnp.pad(b_router, (0, LANES - N_EXPERTS), constant_values=NEG_BIG).reshape(1, LANES)
    x2, hp, ri, rw, cnt = _mix_route(
        o_sb, o_fx, p, qkv // d, b_gate.reshape(1, 2 * d), w_pa.astype(BF16), w_pb.astype(BF16),
        w_out.astype(BF16), x, g_ffn.reshape(1, d), wr_hi, wr_lo, br, cap)

    slots = ri[:, :TOP_K].reshape(-1)
    counts = cnt[0, :N_EXPERTS].astype(I32)
    xs = _dispatch(slots, counts, hp, cap, ch)
    sb_e, sb_blk, sb_nch = _superblocks(counts, r_rows, ch, ns)
    ys = _experts(sb_e, sb_blk, sb_nch, xs, w1, b1, w2, b2, r_rows, ch, cap)
    return _combine(slots, x2, rw, g_out.reshape(1, d), ys, final_norm)


def kernel(x, g_mix, w_in, b_forget, b_gate, w_proj_sb, w_proj_fox, w_out, g_ffn, w_router,
           b_router, w_mlp1, b_mlp1, w_mlp2, b_mlp2, g_final):
    b, s, d = x.shape
    depth = g_mix.shape[0]
    outs = []
    for bi in range(b):
        xb = x[bi]
        for layer in range(depth):
            last = layer == depth - 1
            xb = _layer(layer, xb, g_mix[layer], w_in, b_forget[layer], b_gate[layer],
                        w_proj_sb[layer], w_proj_fox[layer], w_out[layer], g_ffn[layer],
                        w_router[layer], b_router[layer], w_mlp1[layer], b_mlp1[layer],
                        w_mlp2[layer], b_mlp2[layer], g_final, last)
        outs.append(xb.reshape(1, s, d))
    return outs[0] if b == 1 else jnp.concatenate(outs, axis=0)
```
